```python
import math
import jax, jax.numpy as jnp
from jax import lax
import numpy as np

D_MODEL = 1024
BATCH = 8
SEQ = 2048
DEPTH = 1
DEC_BATCH = 16
DEC_SEQ = 2048
PAST_LEN = 128

GRID_W = 64
N_MEM = 256
MLSTM_HEADS = 4
MLSTM_DH = D_MODEL // 8
MLSTM_W = MLSTM_HEADS * MLSTM_DH
MLSTM_CHUNK = 128
ATTN_DH = 128
ATTN_HEADS = D_MODEL // ATTN_DH
KV_HEADS = 2
ATTN_GROUP = ATTN_HEADS // KV_HEADS
ATTN_W = ATTN_HEADS * ATTN_DH
Q_BLOCK = 128
ROPE_AXIS_DIM = ATTN_DH // 2
ROPE_THETA = 10000.0
MEM_HEADS = 4
MEM_DH = 128
MEM_W = MEM_HEADS * MEM_DH
N_BRANCH = 3
D_FF = ((8 * D_MODEL // 3 + 127) // 128) * 128
CONV_W = 3
DEEPNORM_ALPHA = (2.0 * DEPTH) ** 0.25
DEEPNORM_BETA = (8.0 * DEPTH) ** -0.25
LN_EPS = 1e-5
IN_SPLITS = (MLSTM_W, MLSTM_W, MLSTM_W, MLSTM_W, 4 * MLSTM_HEADS,
             ATTN_W, KV_HEADS * ATTN_DH, KV_HEADS * ATTN_DH, MEM_W, N_BRANCH * D_MODEL)
IN_W = sum(IN_SPLITS)

kernel_name = "hybrid_mlstm_gqa_mem_encoder"


def layer_norm(x, g, b):
    xf = x.astype(jnp.float32)
    mu = jnp.mean(xf, axis=-1, keepdims=True)
    xc = xf - mu
    var = jnp.mean(xc * xc, axis=-1, keepdims=True)
    return (xc * lax.rsqrt(var + LN_EPS) * g.astype(jnp.float32) + b.astype(jnp.float32)).astype(x.dtype)


def rms_norm(x, g):
    xf = x.astype(jnp.float32)
    return (xf * lax.rsqrt(jnp.mean(xf * xf, axis=-1, keepdims=True) + LN_EPS) * g.astype(jnp.float32)).astype(x.dtype)


def split_cols(x, sizes):
    out, start = [], 0
    for s in sizes:
        out.append(x[..., start:start + s])
        start += s
    return out


def dwconv_centred(x, w, b):
    T = x.shape[1]
    pad = CONV_W // 2
    xp = jnp.pad(x, ((0, 0), (pad, pad), (0, 0)))
    y = xp[:, 0:T] * w[0] + b
    for j in range(1, CONV_W):
        y = y + xp[:, j:j + T] * w[j]
    return y


def mlstm_chunkwise(q, k, v, log_i, log_f):
    B, T, H, dk = q.shape
    L = MLSTM_CHUNK
    NC = T // L

    def chunks(a):
        return jnp.moveaxis(a.reshape(B, NC, L, H, -1), 3, 1)

    q = chunks(q)
    k = chunks(k) * (dk ** -0.5)
    v = chunks(v)
    li = chunks(log_i[..., None])[..., 0]
    lf = chunks(log_f[..., None])[..., 0]
    bl = jnp.cumsum(lf, axis=-1)
    g = bl[..., -1]

    a = g[..., None] - bl + li
    ma = jnp.max(a, axis=-1)
    wa = jnp.exp(a - ma[..., None])
    kv_chunk = jnp.einsum('bhcl,bhcld,bhcle->bhcde', wa, k, v)
    n_chunk = jnp.einsum('bhcl,bhcld->bhcd', wa, k)

    def step(carry, inp):
        C, n, m = carry
        kv_c, n_c, g_c, ma_c = inp
        m_new = jnp.maximum(g_c + m, ma_c)
        sp = jnp.exp(g_c + m - m_new)
        sc = jnp.exp(ma_c - m_new)
        C_new = sp[..., None, None] * C + sc[..., None, None] * kv_c
        n_new = sp[..., None] * n + sc[..., None] * n_c
        return (C_new, n_new, m_new), (C, n, m)

    init = (jnp.zeros((B, H, dk, v.shape[-1]), jnp.float32),
            jnp.zeros((B, H, dk), jnp.float32),
            jnp.zeros((B, H), jnp.float32))
    xs = (jnp.moveaxis(kv_chunk, 2, 0), jnp.moveaxis(n_chunk, 2, 0),
          jnp.moveaxis(g, 2, 0), jnp.moveaxis(ma, 2, 0))
    _, (C_prev, n_prev, m_prev) = lax.scan(step, init, xs)
    C_prev = jnp.moveaxis(C_prev, 0, 2)
    n_prev = jnp.moveaxis(n_prev, 0, 2)
    m_prev = jnp.moveaxis(m_prev, 0, 2)

    D = bl[..., :, None] - bl[..., None, :] + li[..., None, :]
    mask = jnp.tril(jnp.ones((L, L), dtype=bool))
    D = jnp.where(mask, D, -jnp.inf)
    inter_log = bl + m_prev[..., None]
    m_t = jnp.maximum(inter_log, jnp.max(D, axis=-1))
    s = jnp.einsum('bhctd,bhcsd->bhcts', q, k) * jnp.exp(D - m_t[..., None])
    w_inter = jnp.exp(inter_log - m_t)
    num = (jnp.einsum('bhcts,bhcse->bhcte', s, v)
           + w_inter[..., None] * jnp.einsum('bhctd,bhcde->bhcte', q, C_prev))
    den = jnp.sum(s, axis=-1) + w_inter * jnp.einsum('bhctd,bhcd->bhct', q, n_prev)
    h = num / jnp.maximum(jnp.abs(den), jnp.exp(-m_t))[..., None]
    return jnp.moveaxis(h, 1, 3).reshape(B, T, H, -1)


def bidirectional_mlstm(q, k, v, gate_pre):
    B, T, H, _ = q.shape
    gt = gate_pre.reshape(B, T, 4, H)
    h_f = mlstm_chunkwise(q, k, v, gt[:, :, 0], jax.nn.log_sigmoid(gt[:, :, 1]))
    flip = lambda a: a[:, ::-1]
    h_b = flip(mlstm_chunkwise(flip(q), flip(k), flip(v), flip(gt[:, :, 2]),
                               flip(jax.nn.log_sigmoid(gt[:, :, 3]))))
    return h_f + h_b


def head_layer_norm(h, g):
    B, T, H, dv = h.shape
    mu = jnp.mean(h, axis=-1, keepdims=True)
    hc = h - mu
    var = jnp.mean(hc * hc, axis=-1, keepdims=True)
    return (hc * lax.rsqrt(var + LN_EPS)).reshape(B, T, H * dv) * g.astype(jnp.float32)


def axial_rope_angles(T):
    rows = T // GRID_W
    row = jnp.repeat(jnp.arange(rows, dtype=jnp.float32), GRID_W)
    col = jnp.tile(jnp.arange(GRID_W, dtype=jnp.float32), rows)
    inv_freq = ROPE_THETA ** (-jnp.arange(0, ROPE_AXIS_DIM, 2, dtype=jnp.float32) / ROPE_AXIS_DIM)
    return row[:, None] * inv_freq, col[:, None] * inv_freq


def rotate_axis(x, ang):
    cos = jnp.cos(ang)[None, :, None, :]
    sin = jnp.sin(ang)[None, :, None, :]
    half = ROPE_AXIS_DIM // 2
    x1, x2 = x[..., :half], x[..., half:]
    return jnp.concatenate([x1 * cos - x2 * sin, x2 * cos + x1 * sin], axis=-1)


def apply_axial_rope(x, ang_row, ang_col):
    xf = x.astype(jnp.float32)
    out = jnp.concatenate([rotate_axis(xf[..., :ROPE_AXIS_DIM], ang_row),
                           rotate_axis(xf[..., ROPE_AXIS_DIM:], ang_col)], axis=-1)
    return out.astype(x.dtype)


def blocked_attention(q, k, v):
    B, T, Hkv, G, dh = q.shape
    NB = T // Q_BLOCK
    qb = jnp.moveaxis(q.reshape(B, NB, Q_BLOCK, Hkv, G, dh), 1, 0)
    scale = dh ** -0.5

    def one_block(qblk):
        s = jnp.einsum('bqhgd,bkhd->bhgqk', qblk, k).astype(jnp.float32) * scale
        p = jax.nn.softmax(s, axis=-1).astype(v.dtype)
        return jnp.einsum('bhgqk,bkhd->bqhgd', p, v)

    o = lax.map(one_block, qb)
    return jnp.moveaxis(o, 0, 1).reshape(B, T, Hkv * G * dh)


def token_mixing(x, mem, w_in, mlstm_gate_bias, mlstm_conv_w, mlstm_conv_b, mlstm_norm_g,
                 attn_q_norm_g, attn_k_norm_g, w_mem_kv, w_branch_mlstm, w_branch_attn,
                 w_branch_mem, w_out):
    B, T, _ = x.shape
    proj = x @ w_in
    q_m, k_m, v_m, o_m, g_m, q_a, k_a, v_a, q_c, g_br = split_cols(proj, IN_SPLITS)

    qk_m = jax.nn.silu(dwconv_centred(jnp.concatenate([q_m, k_m], axis=-1), mlstm_conv_w, mlstm_conv_b))
    heads_m = lambda a: a.reshape(B, T, MLSTM_HEADS, MLSTM_DH).astype(jnp.float32)
    h_m = bidirectional_mlstm(heads_m(qk_m[..., :MLSTM_W]), heads_m(qk_m[..., MLSTM_W:]), heads_m(v_m),
                              (g_m + mlstm_gate_bias).astype(jnp.float32))
    h_m = head_layer_norm(h_m, mlstm_norm_g).astype(x.dtype) * jax.nn.sigmoid(o_m)

    ang_r, ang_c = axial_rope_angles(T)
    qa = apply_axial_rope(rms_norm(q_a.reshape(B, T, ATTN_HEADS, ATTN_DH), attn_q_norm_g), ang_r, ang_c)
    ka = apply_axial_rope(rms_norm(k_a.reshape(B, T, KV_HEADS, ATTN_DH), attn_k_norm_g), ang_r, ang_c)
    h_a = blocked_attention(qa.reshape(B, T, KV_HEADS, ATTN_GROUP, ATTN_DH), ka,
                            v_a.reshape(B, T, KV_HEADS, ATTN_DH))

    M = mem.shape[1]
    kv_c = mem @ w_mem_kv
    k_c = kv_c[..., :MEM_W].reshape(B, M, MEM_HEADS, MEM_DH)
    v_c = kv_c[..., MEM_W:].reshape(B, M, MEM_HEADS, MEM_DH)
    s_c = jnp.einsum('bthd,bmhd->bhtm', q_c.reshape(B, T, MEM_HEADS, MEM_DH), k_c).astype(jnp.float32) * (MEM_DH ** -0.5)
    p_c = jax.nn.softmax(s_c, axis=-1).astype(v_c.dtype)
    h_c = jnp.einsum('bhtm,bmhd->bthd', p_c, v_c).reshape(B, T, MEM_W)

    gates = jax.nn.sigmoid(g_br.reshape(B, T, N_BRANCH, D_MODEL))
    merged = (gates[:, :, 0] * (h_m @ w_branch_mlstm)
              + gates[:, :, 1] * (h_a @ w_branch_attn)
              + gates[:, :, 2] * (h_c @ w_branch_mem))
    return merged @ w_out


def conv_ffn(x, w_ffn_up, ffn_conv_w, ffn_conv_b, w_ffn_down):
    u = dwconv_centred(x @ w_ffn_up, ffn_conv_w, ffn_conv_b)
    return (jax.nn.gelu(u[..., :D_FF]) * u[..., D_FF:]) @ w_ffn_down


def encoder_trunk(x, mem, ln_in_g, ln_in_b, w_in, mlstm_gate_bias, mlstm_conv_w, mlstm_conv_b,
                  mlstm_norm_g, attn_q_norm_g, attn_k_norm_g, w_mem_kv, w_branch_mlstm,
                  w_branch_attn, w_branch_mem, w_out, ln1_g, ln1_b, w_ffn_up, ffn_conv_w,
                  ffn_conv_b, w_ffn_down, ln2_g, ln2_b):
    x = layer_norm(x, ln_in_g, ln_in_b)
    for l in range(DEPTH):
        mixed = token_mixing(x, mem, w_in[l], mlstm_gate_bias[l], mlstm_conv_w[l], mlstm_conv_b[l],
                             mlstm_norm_g[l], attn_q_norm_g[l], attn_k_norm_g[l], w_mem_kv[l],
                             w_branch_mlstm[l], w_branch_attn[l], w_branch_mem[l], w_out[l])
        x = layer_norm(DEEPNORM_ALPHA * x + mixed, ln1_g[l], ln1_b[l])
        ff = conv_ffn(x, w_ffn_up[l], ffn_conv_w[l], ffn_conv_b[l], w_ffn_down[l])
        x = layer_norm(DEEPNORM_ALPHA * x + ff, ln2_g[l], ln2_b[l])
    return x


def setup_inputs(seed: int = 0) -> dict:
    key = jax.random.key(seed)
    ks = jax.random.split(key, 32)
    f32 = jnp.float32
    nrm = lambda i, shape, scale: jax.random.normal(ks[i], shape, f32) * scale
    fb = jnp.linspace(3.0, 6.0, MLSTM_HEADS, dtype=f32)
    zh = jnp.zeros((MLSTM_HEADS,), f32)
    gate_base = jnp.concatenate([zh, fb, zh, fb])
    return {
        "x_prompt": nrm(0, (BATCH, SEQ, D_MODEL), 1.0),
        "x_sample": nrm(1, (DEC_BATCH, DEC_SEQ, D_MODEL), 1.0),
        "mem_prompt": nrm(2, (BATCH, N_MEM, D_MODEL), 1.0),
        "mem_sample": nrm(3, (DEC_BATCH, N_MEM, D_MODEL), 1.0),
        "ln_in_g": 1.0 + nrm(4, (D_MODEL,), 0.02),
        "ln_in_b": nrm(5, (D_MODEL,), 0.02),
        "w_in": nrm(6, (DEPTH, D_MODEL, IN_W), D_MODEL ** -0.5),
        "mlstm_gate_bias": gate_base + nrm(7, (DEPTH, 4 * MLSTM_HEADS), 0.1),
        "mlstm_conv_w": nrm(8, (DEPTH, CONV_W, 2 * MLSTM_W), CONV_W ** -0.5),
        "mlstm_conv_b": nrm(9, (DEPTH, 2 * MLSTM_W), 0.02),
        "mlstm_norm_g": 1.0 + nrm(10, (DEPTH, MLSTM_W), 0.02),
        "attn_q_norm_g": 1.0 + nrm(11, (DEPTH, ATTN_DH), 0.02),
        "attn_k_norm_g": 1.0 + nrm(12, (DEPTH, ATTN_DH), 0.02),
        "w_mem_kv": nrm(13, (DEPTH, D_MODEL, 2 * MEM_W), D_MODEL ** -0.5),
        "w_branch_mlstm": nrm(14, (DEPTH, MLSTM_W, D_MODEL), MLSTM_W ** -0.5 * DEEPNORM_BETA),
        "w_branch_attn": nrm(15, (DEPTH, ATTN_W, D_MODEL), ATTN_W ** -0.5 * DEEPNORM_BETA),
        "w_branch_mem": nrm(16, (DEPTH, MEM_W, D_MODEL), MEM_W ** -0.5 * DEEPNORM_BETA),
        "w_out": nrm(17, (DEPTH, D_MODEL, D_MODEL), D_MODEL ** -0.5 * DEEPNORM_BETA),
        "ln1_g": 1.0 + nrm(18, (DEPTH, D_MODEL), 0.02),
        "ln1_b": nrm(19, (DEPTH, D_MODEL), 0.02),
        "w_ffn_up": nrm(20, (DEPTH, D_MODEL, 2 * D_FF), D_MODEL ** -0.5 * DEEPNORM_BETA),
        "ffn_conv_w": nrm(21, (DEPTH, CONV_W, 2 * D_FF), CONV_W ** -0.5),
        "ffn_conv_b": nrm(22, (DEPTH, 2 * D_FF), 0.02),
        "w_ffn_down": nrm(23, (DEPTH, D_FF, D_MODEL), D_FF ** -0.5 * DEEPNORM_BETA),
        "ln2_g": 1.0 + nrm(24, (DEPTH, D_MODEL), 0.02),
        "ln2_b": nrm(25, (DEPTH, D_MODEL), 0.02),
    }


def reference(x_prompt, x_sample, mem_prompt, mem_sample, ln_in_g, ln_in_b, w_in, mlstm_gate_bias,
              mlstm_conv_w, mlstm_conv_b, mlstm_norm_g, attn_q_norm_g, attn_k_norm_g, w_mem_kv,
              w_branch_mlstm, w_branch_attn, w_branch_mem, w_out, ln1_g, ln1_b, w_ffn_up,
              ffn_conv_w, ffn_conv_b, w_ffn_down, ln2_g, ln2_b):
    weights = (ln_in_g, ln_in_b, w_in, mlstm_gate_bias, mlstm_conv_w, mlstm_conv_b, mlstm_norm_g,
               attn_q_norm_g, attn_k_norm_g, w_mem_kv, w_branch_mlstm, w_branch_attn, w_branch_mem,
               w_out, ln1_g, ln1_b, w_ffn_up, ffn_conv_w, ffn_conv_b, w_ffn_down, ln2_g, ln2_b)
    y_prompt = encoder_trunk(x_prompt, mem_prompt, *weights)
    y_sample = encoder_trunk(x_sample, mem_sample, *weights)
    return (y_prompt, y_sample)
```

```python
import functools

import jax
import jax.numpy as jnp
from jax import lax
from jax.experimental import pallas as pl
from jax.experimental.pallas import tpu as pltpu

F32 = jnp.float32
BF16 = jnp.bfloat16

D_MODEL = 1024
DEPTH = 1
GRID_W = 64
MLSTM_HEADS = 4
MLSTM_DH = 128
MLSTM_W = MLSTM_HEADS * MLSTM_DH
MLSTM_CHUNK = 128
N_GATES = 4 * MLSTM_HEADS
ATTN_DH = 128
ATTN_HEADS = 8
KV_HEADS = 2
ATTN_GROUP = ATTN_HEADS // KV_HEADS
ATTN_W = ATTN_HEADS * ATTN_DH
KV_W = KV_HEADS * ATTN_DH
ROPE_AXIS_DIM = ATTN_DH // 2
ROPE_THETA = 10000.0
MEM_HEADS = 4
MEM_DH = 128
MEM_W = MEM_HEADS * MEM_DH
N_BRANCH = 3
D_FF = ((8 * D_MODEL // 3 + 127) // 128) * 128
CONV_W = 3
DEEPNORM_ALPHA = (2.0 * DEPTH) ** 0.25
LN_EPS = 1e-5

LANES = 128
SUBLANES = 8
VMEM_LIMIT_BYTES = 56 * 1024 * 1024

SEG_QKM = (0, 2 * MLSTM_W)
SEG_VOM = (SEG_QKM[1], SEG_QKM[1] + 2 * MLSTM_W)
SEG_QA = (SEG_VOM[1], SEG_VOM[1] + ATTN_W)
SEG_KA = (SEG_QA[1], SEG_QA[1] + KV_W)
SEG_VA = (SEG_KA[1], SEG_KA[1] + KV_W)
SEG_QC = (SEG_VA[1], SEG_VA[1] + MEM_W)
SEG_GBR = (SEG_QC[1], SEG_QC[1] + N_BRANCH * D_MODEL)
SEG_GCOL = (SEG_GBR[1], SEG_GBR[1] + LANES)
W_CAT = SEG_GCOL[1]

FFN_CHUNK = 256
N_FFN_CHUNKS = D_FF // FFN_CHUNK


def _params(n_axes):
    return pltpu.CompilerParams(dimension_semantics=("arbitrary",) * n_axes,
                                vmem_limit_bytes=VMEM_LIMIT_BYTES)


def _const_spec(shape):
    nd = len(shape)
    return pl.BlockSpec(shape, lambda *_: (0,) * nd, pipeline_mode=pl.Buffered(1))


def _layer_norm(x, g, b):
    mu = jnp.mean(x, axis=-1, keepdims=True)
    xc = x - mu
    var = jnp.mean(xc * xc, axis=-1, keepdims=True)
    return xc * lax.rsqrt(var + LN_EPS) * g + b


def _sigmoid(x):
    return 1.0 / (1.0 + jnp.exp(-x))


def _log_sigmoid(x):
    return jnp.minimum(x, 0.0) - jnp.log(1.0 + jnp.exp(-jnp.abs(x)))


def _dot(a, b):
    return jnp.dot(a, b, preferred_element_type=F32)


def _dot_nt(a, b):
    return lax.dot_general(a, b, (((1,), (1,)), ((), ())), preferred_element_type=F32)


def _dot_tn(a, b):
    return lax.dot_general(a, b, (((0,), (0,)), ((), ())), preferred_element_type=F32)


def _split3(x):
    x1 = x.astype(BF16)
    r1 = x - x1.astype(F32)
    x2 = r1.astype(BF16)
    r2 = r1 - x2.astype(F32)
    return x1, x2, r2.astype(BF16)


def _rms_rope(xh, g, cos, sin, lo_mask, scale):
    ms = jnp.mean(xh * xh, axis=-1, keepdims=True)
    xh = xh * lax.rsqrt(ms + LN_EPS) * g
    half = ROPE_AXIS_DIM // 2
    rot = jnp.where(lo_mask, pltpu.roll(xh, LANES - half, 1), pltpu.roll(xh, half, 1))
    out = xh * cos + rot * sin
    if scale != 1.0:
        out = out * scale
    return out


def _proj_kernel(x_ref, lng_ref, lnb_ref, w_ref, wgt_ref, gbc_ref, gbr_row_ref, qg_ref, kg_ref,
                 cos_ref, sin_ref,
                 qkm_ref, vom_ref, qa_ref, ka_ref, va_ref, qc_ref, gbr_ref, gcol_ref, grow_ref):
    tm = x_ref.shape[0]
    xn = _layer_norm(x_ref[...], lng_ref[...], lnb_ref[...])
    xb = xn.astype(BF16)

    def proj(lo, hi):
        return _dot(xb, w_ref[:, lo:hi])

    def store_cols(out_ref, seg, piece=512):
        lo, hi = seg
        for c in range(lo, hi, piece):
            out_ref[:, c - lo:c - lo + piece] = proj(c, c + piece).astype(out_ref.dtype)

    store_cols(qkm_ref, SEG_QKM)
    store_cols(vom_ref, SEG_VOM)
    store_cols(va_ref, SEG_VA, piece=KV_W)
    store_cols(qc_ref, SEG_QC)
    store_cols(gbr_ref, SEG_GBR)

    cos = cos_ref[...]
    sin = sin_ref[...]
    lane = lax.broadcasted_iota(jnp.int32, (tm, LANES), 1)
    lo_mask = (lane % ROPE_AXIS_DIM) < (ROPE_AXIS_DIM // 2)
    qg = qg_ref[...]
    kg = kg_ref[...]
    for h in range(ATTN_HEADS):
        c = SEG_QA[0] + h * ATTN_DH
        qh = _rms_rope(proj(c, c + ATTN_DH), qg, cos, sin, lo_mask, ATTN_DH ** -0.5)
        qa_ref[:, h * ATTN_DH:(h + 1) * ATTN_DH] = qh.astype(BF16)
    for h in range(KV_HEADS):
        c = SEG_KA[0] + h * ATTN_DH
        kh = _rms_rope(proj(c, c + ATTN_DH), kg, cos, sin, lo_mask, 1.0)
        ka_ref[:, h * ATTN_DH:(h + 1) * ATTN_DH] = kh.astype(BF16)

    gcol_ref[...] = proj(*SEG_GCOL) + gbc_ref[...]
    grow = _dot_nt(wgt_ref[...], xb)
    for j in range(tm // LANES):
        grow_ref[j] = grow[:, j * LANES:(j + 1) * LANES] + gbr_row_ref[...]


def _in_proj(x2, T, w, tm):
    n = x2.shape[0]
    grid = (n // tm,)
    tiles_per_seq = T // tm
    row = lambda i: (i, 0)
    out_shapes = (
        jax.ShapeDtypeStruct((n, 2 * MLSTM_W), BF16),
        jax.ShapeDtypeStruct((n, 2 * MLSTM_W), BF16),
        jax.ShapeDtypeStruct((n, ATTN_W), BF16),
        jax.ShapeDtypeStruct((n, KV_W), BF16),
        jax.ShapeDtypeStruct((n, KV_W), BF16),
        jax.ShapeDtypeStruct((n, MEM_W), BF16),
        jax.ShapeDtypeStruct((n, N_BRANCH * D_MODEL), BF16),
        jax.ShapeDtypeStruct((n, LANES), F32),
        jax.ShapeDtypeStruct((n // LANES, N_GATES, LANES), F32),
    )
    out_specs = (
        pl.BlockSpec((tm, 2 * MLSTM_W), row),
        pl.BlockSpec((tm, 2 * MLSTM_W), row),
        pl.BlockSpec((tm, ATTN_W), row),
        pl.BlockSpec((tm, KV_W), row),
        pl.BlockSpec((tm, KV_W), row),
        pl.BlockSpec((tm, MEM_W), row),
        pl.BlockSpec((tm, N_BRANCH * D_MODEL), row),
        pl.BlockSpec((tm, LANES), row),
        pl.BlockSpec((tm // LANES, N_GATES, LANES), lambda i: (i, 0, 0)),
    )
    in_specs = [
        pl.BlockSpec((tm, D_MODEL), row),
        _const_spec((1, D_MODEL)), _const_spec((1, D_MODEL)),
        _const_spec((D_MODEL, W_CAT)), _const_spec((N_GATES, D_MODEL)),
        _const_spec((1, LANES)), _const_spec((N_GATES, LANES)),
        _const_spec((1, ATTN_DH)), _const_spec((1, ATTN_DH)),
        pl.BlockSpec((tm, LANES), lambda i: (i % tiles_per_seq, 0)),
        pl.BlockSpec((tm, LANES), lambda i: (i % tiles_per_seq, 0)),
    ]
    return pl.pallas_call(
        _proj_kernel, grid=grid, in_specs=in_specs, out_specs=out_specs, out_shape=out_shapes,
        compiler_params=_params(1), name="in_proj",
    )(x2, w["ln_in_g"], w["ln_in_b"], w["w_cat"], w["w_gt"], w["gb_col"], w["gb_row"],
      w["q_norm_g"], w["k_norm_g"], w["rope_cos"], w["rope_sin"])


def _mlstm_kernel(qkm_ref, vom_ref, gcol_ref, grow_ref, cw_ref, cb_ref, ng_ref, hm_ref,
                  qk_s, gc_s, gr_s, hf_s, st_s, m_s):
    T = qkm_ref.shape[0]
    L = MLSTM_CHUNK
    NC = T // L
    DH = MLSTM_DH

    ri = lax.broadcasted_iota(jnp.int32, (L, L), 0)
    ci = lax.broadcasted_iota(jnp.int32, (L, L), 1)
    causal = ci <= ri
    anti = ci >= ri
    tril = jnp.where(causal, 1.0, 0.0).astype(BF16)
    triu = jnp.where(anti, 1.0, 0.0).astype(BF16)

    lane = lax.broadcasted_iota(jnp.int32, (L, LANES), 1)
    col_is_f = ((lane % 8) >= MLSTM_HEADS) & (lane < N_GATES)
    col_is_fwd = lane < 2 * MLSTM_HEADS
    rowi = lax.broadcasted_iota(jnp.int32, (N_GATES, L), 0)
    row_is_f = (rowi % 8) >= MLSTM_HEADS
    row_is_fwd = rowi < 2 * MLSTM_HEADS
    sub = lax.broadcasted_iota(jnp.int32, (L, 2 * MLSTM_W), 0)

    cw0 = cw_ref[0:1, :]
    cw1 = cw_ref[1:2, :]
    cw2 = cw_ref[2:3, :]
    cb = cb_ref[...]
    lane_q = lax.broadcasted_iota(jnp.int32, (1, 2 * MLSTM_W), 1)
    kscale = jnp.where(lane_q >= MLSTM_W, MLSTM_DH ** -0.5, 1.0).astype(F32)

    def prologue(c, carry):
        r0 = pl.multiple_of(c * L, L)
        rows = pl.ds(r0, L)
        g = gcol_ref[rows, :]
        x = jnp.where(col_is_f, _log_sigmoid(g), 0.0)
        x1, x2, x3 = _split3(x)
        pre = _dot(tril, x1) + _dot(tril, x2) + _dot(tril, x3)
        suf = _dot(triu, x1) + _dot(triu, x2) + _dot(triu, x3)
        gc_s[rows, :] = jnp.where(col_is_f, jnp.where(col_is_fwd, pre, suf), g)
        gr = grow_ref[c]
        xr = jnp.where(row_is_f, _log_sigmoid(gr), 0.0)
        y1, y2, y3 = _split3(xr)
        pre_r = _dot(y1, triu) + _dot(y2, triu) + _dot(y3, triu)
        suf_r = _dot(y1, tril) + _dot(y2, tril) + _dot(y3, tril)
        gr_s[c] = jnp.where(row_is_f, jnp.where(row_is_fwd, pre_r, suf_r), gr)
        xc = qkm_ref[rows, :].astype(F32)
        pb = pl.multiple_of(jnp.maximum(r0 - 16, 0), 16)
        nb = pl.multiple_of(jnp.minimum(r0 + L, T - 16), 16)
        prev_row = qkm_ref[pl.ds(pb, 16), :].astype(F32)[15:16, :]
        next_row = qkm_ref[pl.ds(nb, 16), :].astype(F32)[0:1, :]
        prev_row = prev_row * jnp.where(c > 0, 1.0, 0.0).astype(F32)
        next_row = next_row * jnp.where(c < NC - 1, 1.0, 0.0).astype(F32)
        x_prev = jnp.where(sub == 0, prev_row, pltpu.roll(xc, 1, 0))
        x_next = jnp.where(sub == L - 1, next_row, pltpu.roll(xc, L - 1, 0))
        y = x_prev * cw0 + cb
        y = y + xc * cw1
        y = y + x_next * cw2
        y = y * _sigmoid(y) * kscale
        qk_s[rows, :] = y.astype(BF16)
        return carry

    lax.fori_loop(0, NC, prologue, 0)

    ones_col = jnp.where(lax.broadcasted_iota(jnp.int32, (L, DH), 1) == 0, 1.0, 0.0).astype(BF16)
    ng = ng_ref[...]

    def run_direction(d):
        st_s[...] = jnp.zeros_like(st_s)
        m_s[...] = jnp.zeros_like(m_s)
        mask = causal if d == 0 else anti

        def body(i, carry):
            c = i if d == 0 else NC - 1 - i
            r0 = pl.multiple_of(c * L, L)
            rows = pl.ds(r0, L)
            gc = gc_s[rows, :]
            gr = gr_s[c]
            for h in range(MLSTM_HEADS):
                ji = 2 * MLSTM_HEADS * d + h
                jf = ji + MLSTM_HEADS
                li_c = gc[:, ji:ji + 1]
                bl_c = gc[:, jf:jf + 1]
                li_r = gr[ji:ji + 1, :]
                bl_r = gr[jf:jf + 1, :]
                g_tot = bl_c[L - 1:L, :] if d == 0 else bl_c[0:1, :]
                hs = slice(h * DH, (h + 1) * DH)
                q = qk_s[rows, hs]
                k = qk_s[rows, MLSTM_W + h * DH:MLSTM_W + (h + 1) * DH]
                v = vom_ref[rows, hs]
                vaug = jnp.concatenate([v, ones_col], axis=1)
                c_prev = st_s[h]
                m_prev = m_s[h][:, 0:1]

                dm = jnp.where(mask, bl_c + (li_r - bl_r), -jnp.inf)
                inter = bl_c + m_prev
                m_t = jnp.maximum(inter, jnp.max(dm, axis=-1, keepdims=True))
                s = _dot_nt(q, k) * jnp.exp(dm - m_t)
                w_inter = jnp.exp(inter - m_t)
                tot = _dot(s.astype(BF16), vaug) + w_inter * _dot(q, c_prev.astype(BF16))
                den = tot[:, DH:DH + 1]
                h_out = tot[:, :DH] / jnp.maximum(jnp.abs(den), jnp.exp(-m_t))

                a = g_tot - bl_c + li_c
                ma = jnp.max(a, axis=0, keepdims=True)
                wa = jnp.exp(a - ma)
                m_new = jnp.maximum(g_tot + m_prev, ma)
                sp = jnp.exp(g_tot + m_prev - m_new)
                sc = jnp.exp(ma - m_new)
                kw = (k.astype(F32) * wa).astype(BF16)
                st_s[h] = sp * c_prev + sc * _dot_tn(kw, vaug)
                m_s[h] = jnp.broadcast_to(m_new, (1, LANES))

                if d == 0:
                    hf_s[rows, hs] = h_out
                else:
                    hh = hf_s[rows, hs] + h_out
                    mu = jnp.mean(hh, axis=-1, keepdims=True)
                    hc = hh - mu
                    var = jnp.mean(hc * hc, axis=-1, keepdims=True)
                    hn = hc * lax.rsqrt(var + LN_EPS) * ng[:, hs]
                    o = vom_ref[rows, MLSTM_W + h * DH:MLSTM_W + (h + 1) * DH].astype(F32)
                    hm_ref[rows, hs] = (hn * _sigmoid(o)).astype(BF16)
            return carry

        lax.fori_loop(0, NC, body, 0)

    run_direction(0)
    run_direction(1)


def _mlstm(qkm, vom, gcol, grow, T, w):
    n = qkm.shape[0]
    B = n // T
    NC = T // MLSTM_CHUNK
    seq = lambda b: (b, 0)
    return pl.pallas_call(
        _mlstm_kernel, grid=(B,),
        in_specs=[
            pl.BlockSpec((T, 2 * MLSTM_W), seq),
            pl.BlockSpec((T, 2 * MLSTM_W), seq),
            pl.BlockSpec((T, LANES), seq),
            pl.BlockSpec((NC, N_GATES, LANES), lambda b: (b, 0, 0)),
            _const_spec((CONV_W, 2 * MLSTM_W)), _const_spec((1, 2 * MLSTM_W)),
            _const_spec((1, MLSTM_W)),
        ],
        out_specs=pl.BlockSpec((T, MLSTM_W), seq),
        out_shape=jax.ShapeDtypeStruct((n, MLSTM_W), BF16),
        scratch_shapes=[
            pltpu.VMEM((T, 2 * MLSTM_W), BF16),
            pltpu.VMEM((T, LANES), F32),
            pltpu.VMEM((NC, N_GATES, LANES), F32),
            pltpu.VMEM((T, MLSTM_W), F32),
            pltpu.VMEM((MLSTM_HEADS, MLSTM_DH, 2 * MLSTM_DH), F32),
            pltpu.VMEM((MLSTM_HEADS, 1, LANES), F32),
        ],
        compiler_params=_params(1), name="mlstm",
    )(qkm, vom, gcol, grow, w["mlstm_conv_w"], w["mlstm_conv_b"], w["mlstm_norm_g"])


def _attn_kernel(q_ref, k_ref, v_ref, o_ref):
    k = k_ref[...]
    v = v_ref[...]
    for g in range(ATTN_GROUP):
        hs = slice(g * ATTN_DH, (g + 1) * ATTN_DH)
        s = _dot_nt(q_ref[:, hs], k)
        m = jnp.max(s, axis=-1, keepdims=True)
        p = jnp.exp(s - m)
        l = jnp.sum(p, axis=-1, keepdims=True)
        o = _dot(p.astype(BF16), v)
        o_ref[:, hs] = (o / l).astype(o_ref.dtype)


def _attention(qa, ka, va, T, tq):
    n = qa.shape[0]
    B = n // T
    nq = T // tq
    gw = ATTN_GROUP * ATTN_DH
    return pl.pallas_call(
        _attn_kernel, grid=(B, KV_HEADS, nq),
        in_specs=[
            pl.BlockSpec((tq, gw), lambda b, h, i: (b * nq + i, h)),
            pl.BlockSpec((T, ATTN_DH), lambda b, h, i: (b, h)),
            pl.BlockSpec((T, ATTN_DH), lambda b, h, i: (b, h)),
        ],
        out_specs=pl.BlockSpec((tq, gw), lambda b, h, i: (b * nq + i, h)),
        out_shape=jax.ShapeDtypeStruct((n, ATTN_W), BF16),
        compiler_params=_params(3), name="gqa_attn",
    )(qa, ka, va)


def _mem_kv_kernel(m_ref, w_ref, o_ref):
    o_ref[...] = _dot(m_ref[...].astype(BF16), w_ref[...]).astype(o_ref.dtype)


def _mem_kv(mem2, M, w):
    n = mem2.shape[0]
    return pl.pallas_call(
        _mem_kv_kernel, grid=(n // M,),
        in_specs=[pl.BlockSpec((M, D_MODEL), lambda b: (b, 0)),
                  _const_spec((D_MODEL, 2 * MEM_W))],
        out_specs=pl.BlockSpec((M, 2 * MEM_W), lambda b: (b, 0)),
        out_shape=jax.ShapeDtypeStruct((n, 2 * MEM_W), BF16),
        compiler_params=_params(1), name="mem_kv",
    )(mem2, w["w_mem_kv"])


def _merge_kernel(x_ref, hm_ref, ha_ref, qc_ref, kv_ref, gbr_ref, lng_ref, lnb_ref,
                  wbm_ref, wba_ref, wbc_ref, wo_ref, l1g_ref, l1b_ref, x1_ref):
    hc_parts = []
    for h in range(MEM_HEADS):
        hs = slice(h * MEM_DH, (h + 1) * MEM_DH)
        kc = kv_ref[:, hs]
        vc = kv_ref[:, MEM_W + h * MEM_DH:MEM_W + (h + 1) * MEM_DH]
        s = _dot_nt(qc_ref[:, hs], kc) * (MEM_DH ** -0.5)
        m = jnp.max(s, axis=-1, keepdims=True)
        p = jnp.exp(s - m)
        l = jnp.sum(p, axis=-1, keepdims=True)
        hc_parts.append((_dot(p.astype(BF16), vc) / l).astype(BF16))
    hc = jnp.concatenate(hc_parts, axis=1)

    def gate(j):
        return _sigmoid(gbr_ref[:, j * D_MODEL:(j + 1) * D_MODEL].astype(F32))

    merged = gate(0) * _dot(hm_ref[...], wbm_ref[...])
    merged = merged + gate(1) * _dot(ha_ref[...], wba_ref[...])
    merged = merged + gate(2) * _dot(hc, wbc_ref[...])
    mixed = _dot(merged.astype(BF16), wo_ref[...])
    xn = _layer_norm(x_ref[...], lng_ref[...], lnb_ref[...])
    x1_ref[...] = _layer_norm(DEEPNORM_ALPHA * xn + mixed, l1g_ref[...], l1b_ref[...])


def _merge(x2, hm, ha, qc, kvc, gbr, T, M, w, tm):
    n = x2.shape[0]
    tiles_per_seq = T // tm
    row = lambda i: (i, 0)
    return pl.pallas_call(
        _merge_kernel, grid=(n // tm,),
        in_specs=[
            pl.BlockSpec((tm, D_MODEL), row),
            pl.BlockSpec((tm, MLSTM_W), row),
            pl.BlockSpec((tm, ATTN_W), row),
            pl.BlockSpec((tm, MEM_W), row),
            pl.BlockSpec((M, 2 * MEM_W), lambda i: (i // tiles_per_seq, 0)),
            pl.BlockSpec((tm, N_BRANCH * D_MODEL), row),
            _const_spec((1, D_MODEL)), _const_spec((1, D_MODEL)),
            _const_spec((MLSTM_W, D_MODEL)), _const_spec((ATTN_W, D_MODEL)),
            _const_spec((MEM_W, D_MODEL)), _const_spec((D_MODEL, D_MODEL)),
            _const_spec((1, D_MODEL)), _const_spec((1, D_MODEL)),
        ],
        out_specs=pl.BlockSpec((tm, D_MODEL), row),
        out_shape=jax.ShapeDtypeStruct((n, D_MODEL), F32),
        compiler_params=_params(1), name="merge_out",
    )(x2, hm, ha, qc, kvc, gbr, w["ln_in_g"], w["ln_in_b"], w["w_branch_mlstm"],
      w["w_branch_attn"], w["w_branch_mem"], w["w_out"], w["ln1_g"], w["ln1_b"])


def _gelu_tanh(x):
    return 0.5 * x * (1.0 + jnp.tanh(0.7978845608028654 * (x + 0.044715 * (x * x * x))))


def _ffn_kernel(xp_ref, x_ref, xn_ref, wup_ref, cw_ref, cb_ref, wdn_ref, l2g_ref, l2b_ref,
                y_ref, acc_s, *, tiles_per_seq):
    tm = x_ref.shape[0]
    i = pl.program_id(0)
    j = i % tiles_per_seq
    x = x_ref[...]
    prev = xp_ref[...] * jnp.where(j > 0, 1.0, 0.0).astype(F32)
    nxt = xn_ref[...] * jnp.where(j < tiles_per_seq - 1, 1.0, 0.0).astype(F32)
    xe = jnp.concatenate([prev, x, nxt], axis=0).astype(BF16)
    te = tm + 2 * SUBLANES
    acc_s[...] = jnp.zeros_like(acc_s)

    def conv(u, cw, cb):
        up = pltpu.roll(u, 1, 0)[SUBLANES:SUBLANES + tm, :]
        uc = u[SUBLANES:SUBLANES + tm, :]
        un = pltpu.roll(u, te - 1, 0)[SUBLANES:SUBLANES + tm, :]
        y = up * cw[0:1, :] + cb
        y = y + uc * cw[1:2, :]
        return y + un * cw[2:3, :]

    def body(c, carry):
        ug = conv(_dot(xe, wup_ref[c]), cw_ref[c], cb_ref[c])
        uv = conv(_dot(xe, wup_ref[N_FFN_CHUNKS + c]), cw_ref[N_FFN_CHUNKS + c],
                  cb_ref[N_FFN_CHUNKS + c])
        act = (_gelu_tanh(ug) * uv).astype(BF16)
        acc_s[...] += _dot(act, wdn_ref[c])
        return carry

    lax.fori_loop(0, N_FFN_CHUNKS, body, 0)
    y_ref[...] = _layer_norm(DEEPNORM_ALPHA * x + acc_s[...], l2g_ref[...], l2b_ref[...])


def _ffn(x1, T, w, tm):
    n = x1.shape[0]
    tiles_per_seq = T // tm
    r8 = tm // SUBLANES
    nblk8 = n // SUBLANES
    row = lambda i: (i, 0)
    return pl.pallas_call(
        functools.partial(_ffn_kernel, tiles_per_seq=tiles_per_seq), grid=(n // tm,),
        in_specs=[
            pl.BlockSpec((SUBLANES, D_MODEL), lambda i: (jnp.maximum(i * r8 - 1, 0), 0)),
            pl.BlockSpec((tm, D_MODEL), row),
            pl.BlockSpec((SUBLANES, D_MODEL), lambda i: (jnp.minimum((i + 1) * r8, nblk8 - 1), 0)),
            _const_spec((2 * N_FFN_CHUNKS, D_MODEL, FFN_CHUNK)),
            _const_spec((2 * N_FFN_CHUNKS, CONV_W, FFN_CHUNK)),
            _const_spec((2 * N_FFN_CHUNKS, 1, FFN_CHUNK)),
            _const_spec((N_FFN_CHUNKS, FFN_CHUNK, D_MODEL)),
            _const_spec((1, D_MODEL)), _const_spec((1, D_MODEL)),
        ],
        out_specs=pl.BlockSpec((tm, D_MODEL), row),
        out_shape=jax.ShapeDtypeStruct((n, D_MODEL), F32),
        scratch_shapes=[pltpu.VMEM((tm, D_MODEL), F32)],
        compiler_params=_params(1), name="conv_ffn",
    )(x1, x1, x1, w["w_ffn_up"], w["ffn_conv_w"], w["ffn_conv_b"], w["w_ffn_down"],
      w["ln2_g"], w["ln2_b"])


def _rope_tables(T):
    rows = T // GRID_W
    row = jnp.repeat(jnp.arange(rows, dtype=F32), GRID_W)
    col = jnp.tile(jnp.arange(GRID_W, dtype=F32), rows)
    inv_freq = ROPE_THETA ** (-jnp.arange(0, ROPE_AXIS_DIM, 2, dtype=F32) / ROPE_AXIS_DIM)
    ang_r = row[:, None] * inv_freq
    ang_c = col[:, None] * inv_freq
    cos = jnp.concatenate([jnp.cos(ang_r)] * 2 + [jnp.cos(ang_c)] * 2, axis=-1)
    sin = jnp.concatenate([-jnp.sin(ang_r), jnp.sin(ang_r), -jnp.sin(ang_c), jnp.sin(ang_c)], axis=-1)
    return cos, sin


def _prep_weights(l, T, ln_in_g, ln_in_b, w_in, mlstm_gate_bias, mlstm_conv_w, mlstm_conv_b,
                  mlstm_norm_g, attn_q_norm_g, attn_k_norm_g, w_mem_kv, w_branch_mlstm,
                  w_branch_attn, w_branch_mem, w_out, ln1_g, ln1_b, w_ffn_up, ffn_conv_w,
                  ffn_conv_b, w_ffn_down, ln2_g, ln2_b):
    wi = w_in[l]
    g0 = 4 * MLSTM_W
    g1 = g0 + N_GATES
    w_gate = wi[:, g0:g1]
    w_cat = jnp.concatenate(
        [wi[:, :g0], wi[:, g1:], w_gate, jnp.zeros((D_MODEL, LANES - N_GATES), wi.dtype)],
        axis=1).astype(BF16)
    gb = mlstm_gate_bias[l].astype(F32)
    cos, sin = _rope_tables(T)
    r2 = lambda a: a.reshape(1, -1).astype(F32)
    up = w_ffn_up[l].astype(BF16).reshape(D_MODEL, 2 * N_FFN_CHUNKS, FFN_CHUNK)
    return {
        "ln_in_g": r2(ln_in_g), "ln_in_b": r2(ln_in_b),
        "w_cat": w_cat, "w_gt": w_gate.T.astype(BF16),
        "gb_col": jnp.pad(gb, (0, LANES - N_GATES)).reshape(1, LANES),
        "gb_row": jnp.broadcast_to(gb[:, None], (N_GATES, LANES)),
        "q_norm_g": r2(attn_q_norm_g[l]), "k_norm_g": r2(attn_k_norm_g[l]),
        "rope_cos": cos, "rope_sin": sin,
        "mlstm_conv_w": mlstm_conv_w[l].astype(F32), "mlstm_conv_b": r2(mlstm_conv_b[l]),
        "mlstm_norm_g": r2(mlstm_norm_g[l]),
        "w_mem_kv": w_mem_kv[l].astype(BF16),
        "w_branch_mlstm": w_branch_mlstm[l].astype(BF16),
        "w_branch_attn": w_branch_attn[l].astype(BF16),
        "w_branch_mem": w_branch_mem[l].astype(BF16),
        "w_out": w_out[l].astype(BF16),
        "ln1_g": r2(ln1_g[l]), "ln1_b": r2(ln1_b[l]),
        "w_ffn_up": jnp.transpose(up, (1, 0, 2)),
        "ffn_conv_w": jnp.transpose(
            ffn_conv_w[l].astype(F32).reshape(CONV_W, 2 * N_FFN_CHUNKS, FFN_CHUNK), (1, 0, 2)),
        "ffn_conv_b": ffn_conv_b[l].astype(F32).reshape(2 * N_FFN_CHUNKS, 1, FFN_CHUNK),
        "w_ffn_down": w_ffn_down[l].astype(BF16).reshape(N_FFN_CHUNKS, FFN_CHUNK, D_MODEL),
        "ln2_g": r2(ln2_g[l]), "ln2_b": r2(ln2_b[l]),
    }


def _tile(T, want):
    t = min(want, T)
    assert T % t == 0
    return t


def _trunk(x, mem, w):
    B, T, _ = x.shape
    M = mem.shape[1]
    assert T % MLSTM_CHUNK == 0 and T % GRID_W == 0
    x2 = x.reshape(B * T, D_MODEL)
    mem2 = mem.reshape(B * M, D_MODEL)
    qkm, vom, qa, ka, va, qc, gbr, gcol, grow = _in_proj(x2, T, w, _tile(T, 512))
    hm = _mlstm(qkm, vom, gcol, grow, T, w)
    ha = _attention(qa, ka, va, T, _tile(T, 256))
    kvc = _mem_kv(mem2, M, w)
    x1 = _merge(x2, hm, ha, qc, kvc, gbr, T, M, w, _tile(T, 256))
    y = _ffn(x1, T, w, _tile(T, 256))
    return y.reshape(B, T, D_MODEL)


def kernel(x_prompt, x_sample, mem_prompt, mem_sample, ln_in_g, ln_in_b, w_in, mlstm_gate_bias, mlstm_conv_w, mlstm_conv_b, mlstm_norm_g, attn_q_norm_g, attn_k_norm_g, w_mem_kv, w_branch_mlstm, w_branch_attn, w_branch_mem, w_out, ln1_g, ln1_b, w_ffn_up, ffn_conv_w, ffn_conv_b, w_ffn_down, ln2_g, ln2_b):
    assert DEPTH == 1 and x_prompt.shape[1] == x_sample.shape[1]
    w = _prep_weights(0, x_prompt.shape[1], ln_in_g, ln_in_b, w_in, mlstm_gate_bias, mlstm_conv_w,
                      mlstm_conv_b, mlstm_norm_g, attn_q_norm_g, attn_k_norm_g, w_mem_kv,
                      w_branch_mlstm, w_branch_attn, w_branch_mem, w_out, ln1_g, ln1_b, w_ffn_up,
                      ffn_conv_w, ffn_conv_b, w_ffn_down, ln2_g, ln2_b)
    return (_trunk(x_prompt, mem_prompt, w), _trunk(x_sample, mem_sample, w))
```

```python
import functools

import jax
import jax.numpy as jnp
from jax import lax
from jax.experimental import pallas as pl
from jax.experimental.pallas import tpu as pltpu

F32 = jnp.float32
BF16 = jnp.bfloat16

D_MODEL = 1024
DEPTH = 1
GRID_W = 64
MLSTM_HEADS = 4
MLSTM_DH = 128
MLSTM_W = MLSTM_HEADS * MLSTM_DH
MLSTM_CHUNK = 128
N_GATES = 4 * MLSTM_HEADS
ATTN_DH = 128
ATTN_HEADS = 8
KV_HEADS = 2
ATTN_GROUP = ATTN_HEADS // KV_HEADS
ATTN_W = ATTN_HEADS * ATTN_DH
KV_W = KV_HEADS * ATTN_DH
ROPE_AXIS_DIM = ATTN_DH // 2
ROPE_THETA = 10000.0
MEM_HEADS = 4
MEM_DH = 128
MEM_W = MEM_HEADS * MEM_DH
N_BRANCH = 3
D_FF = ((8 * D_MODEL // 3 + 127) // 128) * 128
CONV_W = 3
DEEPNORM_ALPHA = (2.0 * DEPTH) ** 0.25
LN_EPS = 1e-5
LOG2_E = 1.4426950408889634

LANES = 128
SUBLANES = 8
VMEM_LIMIT_BYTES = 56 * 1024 * 1024

SEG_QKM = (0, 2 * MLSTM_W)
SEG_VOM = (SEG_QKM[1], SEG_QKM[1] + 2 * MLSTM_W)
SEG_QA = (SEG_VOM[1], SEG_VOM[1] + ATTN_W)
SEG_KA = (SEG_QA[1], SEG_QA[1] + KV_W)
SEG_VA = (SEG_KA[1], SEG_KA[1] + KV_W)
SEG_QC = (SEG_VA[1], SEG_VA[1] + MEM_W)
SEG_GBR = (SEG_QC[1], SEG_QC[1] + N_BRANCH * D_MODEL)
SEG_GCOL = (SEG_GBR[1], SEG_GBR[1] + LANES)
W_CAT = SEG_GCOL[1]

FFN_CHUNK = 256
N_FFN_CHUNKS = D_FF // FFN_CHUNK


def _params(n_axes):
    return pltpu.CompilerParams(dimension_semantics=("arbitrary",) * n_axes,
                                vmem_limit_bytes=VMEM_LIMIT_BYTES)


def _const_spec(shape):
    nd = len(shape)
    return pl.BlockSpec(shape, lambda *_: (0,) * nd, pipeline_mode=pl.Buffered(1))


def _layer_norm(x, g, b):
    mu = jnp.mean(x, axis=-1, keepdims=True)
    xc = x - mu
    var = jnp.mean(xc * xc, axis=-1, keepdims=True)
    return xc * lax.rsqrt(var + LN_EPS) * g + b


def _sigmoid(x):
    return 1.0 / (1.0 + jnp.exp(-x))


def _log_sigmoid(x):
    return jnp.minimum(x, 0.0) - jnp.log(1.0 + jnp.exp(-jnp.abs(x)))


def _dot(a, b):
    return jnp.dot(a, b, preferred_element_type=F32)


def _dot_nt(a, b):
    return lax.dot_general(a, b, (((1,), (1,)), ((), ())), preferred_element_type=F32)


def _dot_tn(a, b):
    return lax.dot_general(a, b, (((0,), (0,)), ((), ())), preferred_element_type=F32)


def _split3(x):
    x1 = x.astype(BF16)
    r1 = x - x1.astype(F32)
    x2 = r1.astype(BF16)
    r2 = r1 - x2.astype(F32)
    return x1, x2, r2.astype(BF16)


def _rms_rope(xh, cos_g, sin_g):
    ms = jnp.mean(xh * xh, axis=-1, keepdims=True)
    xh = xh * lax.rsqrt(ms + LN_EPS)
    return xh * cos_g + pltpu.roll(xh, LANES // 2, 1) * sin_g


def _proj_kernel(x_ref, lng_ref, lnb_ref, w_ref, wgt_ref, gbc_ref, gbr_row_ref,
                 cq_ref, sq_ref, ck_ref, sk_ref,
                 qkm_ref, vom_ref, qa_ref, ka_ref, va_ref, qc_ref, gbr_ref, gcol_ref, grow_ref):
    tm = x_ref.shape[0]
    xn = _layer_norm(x_ref[...], lng_ref[...], lnb_ref[...])
    xb = xn.astype(BF16)

    def proj(lo, hi):
        return _dot(xb, w_ref[:, lo:hi])

    plain = []
    for out_ref, seg, piece in ((qkm_ref, SEG_QKM, 512), (vom_ref, SEG_VOM, 512), (va_ref, SEG_VA, KV_W),
                                (qc_ref, SEG_QC, 512), (gbr_ref, SEG_GBR, 512)):
        plain += [(out_ref, c - seg[0], c, piece) for c in range(seg[0], seg[1], piece)]
    pair = 2 * ATTN_DH
    roped = [(qa_ref, c - SEG_QA[0], c, cq_ref, sq_ref) for c in range(SEG_QA[0], SEG_QA[1], pair)]
    roped += [(ka_ref, c - SEG_KA[0], c, ck_ref, sk_ref) for c in range(SEG_KA[0], SEG_KA[1], pair)]

    def emit_plain(k):
        for out_ref, dst, src, piece in plain[:k]:
            out_ref[:, dst:dst + piece] = proj(src, src + piece).astype(out_ref.dtype)
        del plain[:k]

    per = -(-len(plain) // len(roped))
    for out_ref, dst, src, c_ref, s_ref in roped:
        blk = proj(src, src + pair)
        emit_plain(per)
        for j in range(2):
            xh = _rms_rope(blk[:, j * ATTN_DH:(j + 1) * ATTN_DH], c_ref[...], s_ref[...])
            out_ref[:, dst + j * ATTN_DH:dst + (j + 1) * ATTN_DH] = xh.astype(BF16)
    emit_plain(len(plain))

    gcol_ref[...] = proj(*SEG_GCOL) + gbc_ref[...]
    grow = _dot_nt(wgt_ref[...], xb)
    for j in range(tm // LANES):
        grow_ref[j] = grow[:, j * LANES:(j + 1) * LANES] + gbr_row_ref[...]


def _in_proj(x2, T, w, tm):
    n = x2.shape[0]
    grid = (n // tm,)
    tiles_per_seq = T // tm
    row = lambda i: (i, 0)
    out_shapes = (
        jax.ShapeDtypeStruct((n, 2 * MLSTM_W), BF16),
        jax.ShapeDtypeStruct((n, 2 * MLSTM_W), BF16),
        jax.ShapeDtypeStruct((n, ATTN_W), BF16),
        jax.ShapeDtypeStruct((n, KV_W), BF16),
        jax.ShapeDtypeStruct((n, KV_W), BF16),
        jax.ShapeDtypeStruct((n, MEM_W), BF16),
        jax.ShapeDtypeStruct((n, N_BRANCH * D_MODEL), BF16),
        jax.ShapeDtypeStruct((n, LANES), F32),
        jax.ShapeDtypeStruct((n // LANES, N_GATES, LANES), F32),
    )
    out_specs = (
        pl.BlockSpec((tm, 2 * MLSTM_W), row),
        pl.BlockSpec((tm, 2 * MLSTM_W), row),
        pl.BlockSpec((tm, ATTN_W), row),
        pl.BlockSpec((tm, KV_W), row),
        pl.BlockSpec((tm, KV_W), row),
        pl.BlockSpec((tm, MEM_W), row),
        pl.BlockSpec((tm, N_BRANCH * D_MODEL), row),
        pl.BlockSpec((tm, LANES), row),
        pl.BlockSpec((tm // LANES, N_GATES, LANES), lambda i: (i, 0, 0)),
    )
    in_specs = [
        pl.BlockSpec((tm, D_MODEL), row),
        _const_spec((1, D_MODEL)), _const_spec((1, D_MODEL)),
        _const_spec((D_MODEL, W_CAT)), _const_spec((N_GATES, D_MODEL)),
        _const_spec((1, LANES)), _const_spec((N_GATES, LANES)),
    ] + [pl.BlockSpec((tm, LANES), lambda i: (i % tiles_per_seq, 0))] * 4
    return pl.pallas_call(
        _proj_kernel, grid=grid, in_specs=in_specs, out_specs=out_specs, out_shape=out_shapes,
        compiler_params=_params(1), name="in_proj",
    )(x2, w["ln_in_g"], w["ln_in_b"], w["w_cat"], w["w_gt"], w["gb_col"], w["gb_row"],
      w["rope_cos_q"], w["rope_sin_q"], w["rope_cos_k"], w["rope_sin_k"])


def _mlstm_kernel(qkm_ref, vom_ref, gcol_ref, grow_ref, cw_ref, cb_ref, ng_ref, hm_ref,
                  qk_s, gc_s, gr_s, hf_s, st_s, m_s):
    T = qkm_ref.shape[0]
    L = MLSTM_CHUNK
    NC = T // L
    DH = MLSTM_DH

    ri = lax.broadcasted_iota(jnp.int32, (L, L), 0)
    ci = lax.broadcasted_iota(jnp.int32, (L, L), 1)
    causal = ci <= ri
    anti = ci >= ri
    tril = jnp.where(causal, 1.0, 0.0).astype(BF16)
    triu = jnp.where(anti, 1.0, 0.0).astype(BF16)

    lane = lax.broadcasted_iota(jnp.int32, (L, LANES), 1)
    col_is_f = ((lane % 8) >= MLSTM_HEADS) & (lane < N_GATES)
    col_is_fwd = lane < 2 * MLSTM_HEADS
    rowi = lax.broadcasted_iota(jnp.int32, (N_GATES, L), 0)
    row_is_f = (rowi % 8) >= MLSTM_HEADS
    row_is_fwd = rowi < 2 * MLSTM_HEADS
    sub = lax.broadcasted_iota(jnp.int32, (L, 2 * MLSTM_W), 0)

    cw0 = cw_ref[0:1, :]
    cw1 = cw_ref[1:2, :]
    cw2 = cw_ref[2:3, :]
    cb = cb_ref[...]
    lane_q = lax.broadcasted_iota(jnp.int32, (1, 2 * MLSTM_W), 1)
    kscale = jnp.where(lane_q >= MLSTM_W, MLSTM_DH ** -0.5, 1.0).astype(F32)

    def prologue(c, carry):
        r0 = pl.multiple_of(c * L, L)
        rows = pl.ds(r0, L)
        g = gcol_ref[rows, :]
        x = jnp.where(col_is_f, _log_sigmoid(g), 0.0)
        x1, x2, x3 = _split3(x)
        pre = _dot(tril, x1) + _dot(tril, x2) + _dot(tril, x3)
        suf = _dot(triu, x1) + _dot(triu, x2) + _dot(triu, x3)
        gc_s[rows, :] = jnp.where(col_is_f, jnp.where(col_is_fwd, pre, suf), g)
        gr = grow_ref[c]
        xr = jnp.where(row_is_f, _log_sigmoid(gr), 0.0)
        y1, y2, y3 = _split3(xr)
        pre_r = _dot(y1, triu) + _dot(y2, triu) + _dot(y3, triu)
        suf_r = _dot(y1, tril) + _dot(y2, tril) + _dot(y3, tril)
        gr_s[c] = jnp.where(row_is_f, jnp.where(row_is_fwd, pre_r, suf_r), gr)
        xc = qkm_ref[rows, :].astype(F32)
        pb = pl.multiple_of(jnp.maximum(r0 - 16, 0), 16)
        nb = pl.multiple_of(jnp.minimum(r0 + L, T - 16), 16)
        prev_row = qkm_ref[pl.ds(pb, 16), :].astype(F32)[15:16, :]
        next_row = qkm_ref[pl.ds(nb, 16), :].astype(F32)[0:1, :]
        prev_row = prev_row * jnp.where(c > 0, 1.0, 0.0).astype(F32)
        next_row = next_row * jnp.where(c < NC - 1, 1.0, 0.0).astype(F32)
        x_prev = jnp.where(sub == 0, prev_row, pltpu.roll(xc, 1, 0))
        x_next = jnp.where(sub == L - 1, next_row, pltpu.roll(xc, L - 1, 0))
        y = x_prev * cw0 + cb
        y = y + xc * cw1
        y = y + x_next * cw2
        y = y * _sigmoid(y) * kscale
        qk_s[rows, :] = y.astype(BF16)
        return carry

    lax.fori_loop(0, NC, prologue, 0)

    ones_col = jnp.where(lax.broadcasted_iota(jnp.int32, (L, DH), 1) == 0, 1.0, 0.0).astype(BF16)
    ng = ng_ref[...]

    def run_direction(d):
        st_s[...] = jnp.zeros_like(st_s)
        m_s[...] = jnp.zeros_like(m_s)
        mask = causal if d == 0 else anti

        def body(i, carry):
            c = i if d == 0 else NC - 1 - i
            r0 = pl.multiple_of(c * L, L)
            rows = pl.ds(r0, L)
            gc = gc_s[rows, :]
            gr = gr_s[c]
            for h in range(MLSTM_HEADS):
                ji = 2 * MLSTM_HEADS * d + h
                jf = ji + MLSTM_HEADS
                li_c = gc[:, ji:ji + 1]
                bl_c = gc[:, jf:jf + 1]
                li_r = gr[ji:ji + 1, :]
                bl_r = gr[jf:jf + 1, :]
                g_tot = bl_c[L - 1:L, :] if d == 0 else bl_c[0:1, :]
                hs = slice(h * DH, (h + 1) * DH)
                q = qk_s[rows, hs]
                k = qk_s[rows, MLSTM_W + h * DH:MLSTM_W + (h + 1) * DH]
                v = vom_ref[rows, hs]
                vaug = jnp.concatenate([v, ones_col], axis=1)
                c_prev = st_s[h]
                m_prev = m_s[h][:, 0:1]

                dm = jnp.where(mask, bl_c + (li_r - bl_r), -jnp.inf)
                inter = bl_c + m_prev
                m_t = jnp.maximum(inter, jnp.max(dm, axis=-1, keepdims=True))
                s = _dot_nt(q, k) * jnp.exp(dm - m_t)
                w_inter = jnp.exp(inter - m_t)
                tot = _dot(s.astype(BF16), vaug) + w_inter * _dot(q, c_prev.astype(BF16))
                den = tot[:, DH:DH + 1]
                h_out = tot[:, :DH] / jnp.maximum(jnp.abs(den), jnp.exp(-m_t))

                a = g_tot - bl_c + li_c
                ma = jnp.max(a, axis=0, keepdims=True)
                wa = jnp.exp(a - ma)
                m_new = jnp.maximum(g_tot + m_prev, ma)
                sp = jnp.exp(g_tot + m_prev - m_new)
                sc = jnp.exp(ma - m_new)
                kw = (k.astype(F32) * wa).astype(BF16)
                st_s[h] = sp * c_prev + sc * _dot_tn(kw, vaug)
                m_s[h] = jnp.broadcast_to(m_new, (1, LANES))

                if d == 0:
                    hf_s[rows, hs] = h_out
                else:
                    hh = hf_s[rows, hs] + h_out
                    mu = jnp.mean(hh, axis=-1, keepdims=True)
                    hc = hh - mu
                    var = jnp.mean(hc * hc, axis=-1, keepdims=True)
                    hn = hc * lax.rsqrt(var + LN_EPS) * ng[:, hs]
                    o = vom_ref[rows, MLSTM_W + h * DH:MLSTM_W + (h + 1) * DH].astype(F32)
                    hm_ref[rows, hs] = (hn * _sigmoid(o)).astype(BF16)
            return carry

        lax.fori_loop(0, NC, body, 0)

    run_direction(0)
    run_direction(1)


def _mlstm(qkm, vom, gcol, grow, T, w):
    n = qkm.shape[0]
    B = n // T
    NC = T // MLSTM_CHUNK
    seq = lambda b: (b, 0)
    return pl.pallas_call(
        _mlstm_kernel, grid=(B,),
        in_specs=[
            pl.BlockSpec((T, 2 * MLSTM_W), seq),
            pl.BlockSpec((T, 2 * MLSTM_W), seq),
            pl.BlockSpec((T, LANES), seq),
            pl.BlockSpec((NC, N_GATES, LANES), lambda b: (b, 0, 0)),
            _const_spec((CONV_W, 2 * MLSTM_W)), _const_spec((1, 2 * MLSTM_W)),
            _const_spec((1, MLSTM_W)),
        ],
        out_specs=pl.BlockSpec((T, MLSTM_W), seq),
        out_shape=jax.ShapeDtypeStruct((n, MLSTM_W), BF16),
        scratch_shapes=[
            pltpu.VMEM((T, 2 * MLSTM_W), BF16),
            pltpu.VMEM((T, LANES), F32),
            pltpu.VMEM((NC, N_GATES, LANES), F32),
            pltpu.VMEM((T, MLSTM_W), F32),
            pltpu.VMEM((MLSTM_HEADS, MLSTM_DH, 2 * MLSTM_DH), F32),
            pltpu.VMEM((MLSTM_HEADS, 1, LANES), F32),
        ],
        compiler_params=_params(1), name="mlstm",
    )(qkm, vom, gcol, grow, w["mlstm_conv_w"], w["mlstm_conv_b"], w["mlstm_norm_g"])


def _attn_kernel(q_ref, k_ref, v_ref, o_ref, vaug_s):
    @pl.when(pl.program_id(2) == 0)
    def _():
        vaug_s[:, :ATTN_DH] = v_ref[...]
        vaug_s[:, ATTN_DH:] = jnp.ones((v_ref.shape[0], ATTN_DH), BF16)

    k = k_ref[...]
    for g in range(ATTN_GROUP):
        hs = slice(g * ATTN_DH, (g + 1) * ATTN_DH)
        s = _dot_nt(q_ref[:, hs], k)
        m = jnp.max(s, axis=-1, keepdims=True)
        p = jnp.exp2(s - m)
        oa = _dot(p.astype(BF16), vaug_s[...])
        o_ref[:, hs] = (oa[:, :ATTN_DH] / oa[:, ATTN_DH:ATTN_DH + 1]).astype(o_ref.dtype)


def _attention(qa, ka, va, T, tq):
    n = qa.shape[0]
    B = n // T
    nq = T // tq
    gw = ATTN_GROUP * ATTN_DH
    return pl.pallas_call(
        _attn_kernel, grid=(B, KV_HEADS, nq),
        in_specs=[
            pl.BlockSpec((tq, gw), lambda b, h, i: (b * nq + i, h)),
            pl.BlockSpec((T, ATTN_DH), lambda b, h, i: (b, h)),
            pl.BlockSpec((T, ATTN_DH), lambda b, h, i: (b, h)),
        ],
        out_specs=pl.BlockSpec((tq, gw), lambda b, h, i: (b * nq + i, h)),
        out_shape=jax.ShapeDtypeStruct((n, ATTN_W), BF16),
        scratch_shapes=[pltpu.VMEM((T, 2 * ATTN_DH), BF16)],
        compiler_params=_params(3), name="gqa_attn",
    )(qa, ka, va)


def _mem_kv_kernel(m_ref, w_ref, o_ref):
    o_ref[...] = _dot(m_ref[...].astype(BF16), w_ref[...]).astype(o_ref.dtype)


def _mem_kv(mem2, M, w):
    n = mem2.shape[0]
    return pl.pallas_call(
        _mem_kv_kernel, grid=(n // M,),
        in_specs=[pl.BlockSpec((M, D_MODEL), lambda b: (b, 0)),
                  _const_spec((D_MODEL, 2 * MEM_W))],
        out_specs=pl.BlockSpec((M, 2 * MEM_W), lambda b: (b, 0)),
        out_shape=jax.ShapeDtypeStruct((n, 2 * MEM_W), BF16),
        compiler_params=_params(1), name="mem_kv",
    )(mem2, w["w_mem_kv"])


def _merge_kernel(x_ref, hm_ref, ha_ref, qc_ref, kv_ref, gbr_ref, lng_ref, lnb_ref,
                  wbm_ref, wba_ref, wbc_ref, wo_ref, l1g_ref, l1b_ref, x1_ref):
    hc_parts = []
    for h in range(MEM_HEADS):
        hs = slice(h * MEM_DH, (h + 1) * MEM_DH)
        kc = kv_ref[:, hs]
        vc = kv_ref[:, MEM_W + h * MEM_DH:MEM_W + (h + 1) * MEM_DH]
        s = _dot_nt(qc_ref[:, hs], kc) * (MEM_DH ** -0.5)
        m = jnp.max(s, axis=-1, keepdims=True)
        p = jnp.exp(s - m)
        l = jnp.sum(p, axis=-1, keepdims=True)
        hc_parts.append((_dot(p.astype(BF16), vc) / l).astype(BF16))
    hc = jnp.concatenate(hc_parts, axis=1)

    def gate(j):
        return _sigmoid(gbr_ref[:, j * D_MODEL:(j + 1) * D_MODEL].astype(F32))

    merged = gate(0) * _dot(hm_ref[...], wbm_ref[...])
    merged = merged + gate(1) * _dot(ha_ref[...], wba_ref[...])
    merged = merged + gate(2) * _dot(hc, wbc_ref[...])
    mixed = _dot(merged.astype(BF16), wo_ref[...])
    xn = _layer_norm(x_ref[...], lng_ref[...], lnb_ref[...])
    x1_ref[...] = _layer_norm(DEEPNORM_ALPHA * xn + mixed, l1g_ref[...], l1b_ref[...])


def _merge(x2, hm, ha, qc, kvc, gbr, T, M, w, tm):
    n = x2.shape[0]
    tiles_per_seq = T // tm
    row = lambda i: (i, 0)
    return pl.pallas_call(
        _merge_kernel, grid=(n // tm,),
        in_specs=[
            pl.BlockSpec((tm, D_MODEL), row),
            pl.BlockSpec((tm, MLSTM_W), row),
            pl.BlockSpec((tm, ATTN_W), row),
            pl.BlockSpec((tm, MEM_W), row),
            pl.BlockSpec((M, 2 * MEM_W), lambda i: (i // tiles_per_seq, 0)),
            pl.BlockSpec((tm, N_BRANCH * D_MODEL), row),
            _const_spec((1, D_MODEL)), _const_spec((1, D_MODEL)),
            _const_spec((MLSTM_W, D_MODEL)), _const_spec((ATTN_W, D_MODEL)),
            _const_spec((MEM_W, D_MODEL)), _const_spec((D_MODEL, D_MODEL)),
            _const_spec((1, D_MODEL)), _const_spec((1, D_MODEL)),
        ],
        out_specs=pl.BlockSpec((tm, D_MODEL), row),
        out_shape=jax.ShapeDtypeStruct((n, D_MODEL), F32),
        compiler_params=_params(1), name="merge_out",
    )(x2, hm, ha, qc, kvc, gbr, w["ln_in_g"], w["ln_in_b"], w["w_branch_mlstm"],
      w["w_branch_attn"], w["w_branch_mem"], w["w_out"], w["ln1_g"], w["ln1_b"])


def _gelu_tanh(x):
    return 0.5 * x * (1.0 + jnp.tanh(0.7978845608028654 * (x + 0.044715 * (x * x * x))))


def _ffn_kernel(xp_ref, x_ref, xn_ref, wup_ref, cw_ref, cb_ref, wdn_ref, l2g_ref, l2b_ref,
                y_ref, xe_s, act_s, *, tiles_per_seq):
    tm = x_ref.shape[0]
    i = pl.program_id(0)
    j = i % tiles_per_seq
    prev = xp_ref[...] * jnp.where(j > 0, 1.0, 0.0).astype(F32)
    nxt = xn_ref[...] * jnp.where(j < tiles_per_seq - 1, 1.0, 0.0).astype(F32)
    xe_s[...] = jnp.concatenate([prev, x_ref[...], nxt], axis=0).astype(BF16)
    te = tm + 2 * SUBLANES

    def conv(u, cw, cb):
        up = pltpu.roll(u, 1, 0)[SUBLANES:SUBLANES + tm, :]
        uc = u[SUBLANES:SUBLANES + tm, :]
        un = pltpu.roll(u, te - 1, 0)[SUBLANES:SUBLANES + tm, :]
        y = up * cw[0:1, :] + cb
        y = y + uc * cw[1:2, :]
        return y + un * cw[2:3, :]

    for c in range(N_FFN_CHUNKS):
        cv = N_FFN_CHUNKS + c
        ug = conv(_dot(xe_s[...], wup_ref[c]), cw_ref[c], cb_ref[c])
        uv = conv(_dot(xe_s[...], wup_ref[cv]), cw_ref[cv], cb_ref[cv])
        act_s[:, c * FFN_CHUNK:(c + 1) * FFN_CHUNK] = (_gelu_tanh(ug) * uv).astype(BF16)

    ff = _dot(act_s[...], wdn_ref[...])
    y_ref[...] = _layer_norm(DEEPNORM_ALPHA * x_ref[...] + ff, l2g_ref[...], l2b_ref[...])


def _ffn(x1, T, w, tm):
    n = x1.shape[0]
    tiles_per_seq = T // tm
    r8 = tm // SUBLANES
    nblk8 = n // SUBLANES
    row = lambda i: (i, 0)
    return pl.pallas_call(
        functools.partial(_ffn_kernel, tiles_per_seq=tiles_per_seq), grid=(n // tm,),
        in_specs=[
            pl.BlockSpec((SUBLANES, D_MODEL), lambda i: (jnp.maximum(i * r8 - 1, 0), 0)),
            pl.BlockSpec((tm, D_MODEL), row),
            pl.BlockSpec((SUBLANES, D_MODEL), lambda i: (jnp.minimum((i + 1) * r8, nblk8 - 1), 0)),
            _const_spec((2 * N_FFN_CHUNKS, D_MODEL, FFN_CHUNK)),
            _const_spec((2 * N_FFN_CHUNKS, CONV_W, FFN_CHUNK)),
            _const_spec((2 * N_FFN_CHUNKS, 1, FFN_CHUNK)),
            _const_spec((D_FF, D_MODEL)),
            _const_spec((1, D_MODEL)), _const_spec((1, D_MODEL)),
        ],
        out_specs=pl.BlockSpec((tm, D_MODEL), row),
        out_shape=jax.ShapeDtypeStruct((n, D_MODEL), F32),
        scratch_shapes=[pltpu.VMEM((tm + 2 * SUBLANES, D_MODEL), BF16),
                        pltpu.VMEM((tm, D_FF), BF16)],
        compiler_params=_params(1), name="conv_ffn",
    )(x1, x1, x1, w["w_ffn_up"], w["ffn_conv_w"], w["ffn_conv_b"], w["w_ffn_down"],
      w["ln2_g"], w["ln2_b"])


def _rope_tables(T):
    rows = T // GRID_W
    row = jnp.repeat(jnp.arange(rows, dtype=F32), GRID_W)
    col = jnp.tile(jnp.arange(GRID_W, dtype=F32), rows)
    inv_freq = ROPE_THETA ** (-jnp.arange(0, ROPE_AXIS_DIM, 2, dtype=F32) / ROPE_AXIS_DIM)
    ang_r = row[:, None] * inv_freq
    ang_c = col[:, None] * inv_freq
    cos = jnp.concatenate([jnp.cos(ang_r), jnp.cos(ang_c)] * 2, axis=-1)
    sin = jnp.concatenate([-jnp.sin(ang_r), -jnp.sin(ang_c), jnp.sin(ang_r), jnp.sin(ang_c)], axis=-1)
    return cos, sin


def _rope_head_perm():
    p = jnp.arange(ATTN_DH)
    quarter = ROPE_AXIS_DIM // 2
    half, axis, j = p // ROPE_AXIS_DIM, (p % ROPE_AXIS_DIM) // quarter, p % quarter
    return axis * ROPE_AXIS_DIM + half * quarter + j


def _gained_tables(cos, sin, g, scale):
    g = g.astype(F32)[_rope_head_perm()]
    return cos * (g * scale), sin * (jnp.roll(g, ATTN_DH // 2) * scale)


def _prep_weights(l, T, ln_in_g, ln_in_b, w_in, mlstm_gate_bias, mlstm_conv_w, mlstm_conv_b,
                  mlstm_norm_g, attn_q_norm_g, attn_k_norm_g, w_mem_kv, w_branch_mlstm,
                  w_branch_attn, w_branch_mem, w_out, ln1_g, ln1_b, w_ffn_up, ffn_conv_w,
                  ffn_conv_b, w_ffn_down, ln2_g, ln2_b):
    wi = w_in[l]
    g0 = 4 * MLSTM_W
    g1 = g0 + N_GATES
    w_gate = wi[:, g0:g1]
    perm = _rope_head_perm()
    qk0 = g1
    qk1 = qk0 + ATTN_W + KV_W
    n_qk_heads = ATTN_HEADS + KV_HEADS
    qk_cols = qk0 + (jnp.arange(n_qk_heads)[:, None] * ATTN_DH + perm[None, :]).reshape(-1)
    w_cat = jnp.concatenate(
        [wi[:, :g0], wi[:, qk_cols], wi[:, qk1:], w_gate,
         jnp.zeros((D_MODEL, LANES - N_GATES), wi.dtype)], axis=1).astype(BF16)
    gb = mlstm_gate_bias[l].astype(F32)
    cos, sin = _rope_tables(T)
    cos_q, sin_q = _gained_tables(cos, sin, attn_q_norm_g[l], ATTN_DH ** -0.5 * LOG2_E)
    cos_k, sin_k = _gained_tables(cos, sin, attn_k_norm_g[l], 1.0)
    r2 = lambda a: a.reshape(1, -1).astype(F32)
    up = w_ffn_up[l].astype(BF16).reshape(D_MODEL, 2 * N_FFN_CHUNKS, FFN_CHUNK)
    return {
        "ln_in_g": r2(ln_in_g), "ln_in_b": r2(ln_in_b),
        "w_cat": w_cat, "w_gt": w_gate.T.astype(BF16),
        "gb_col": jnp.pad(gb, (0, LANES - N_GATES)).reshape(1, LANES),
        "gb_row": jnp.broadcast_to(gb[:, None], (N_GATES, LANES)),
        "rope_cos_q": cos_q, "rope_sin_q": sin_q, "rope_cos_k": cos_k, "rope_sin_k": sin_k,
        "mlstm_conv_w": mlstm_conv_w[l].astype(F32), "mlstm_conv_b": r2(mlstm_conv_b[l]),
        "mlstm_norm_g": r2(mlstm_norm_g[l]),
        "w_mem_kv": w_mem_kv[l].astype(BF16),
        "w_branch_mlstm": w_branch_mlstm[l].astype(BF16),
        "w_branch_attn": w_branch_attn[l].astype(BF16),
        "w_branch_mem": w_branch_mem[l].astype(BF16),
        "w_out": w_out[l].astype(BF16),
        "ln1_g": r2(ln1_g[l]), "ln1_b": r2(ln1_b[l]),
        "w_ffn_up": jnp.transpose(up, (1, 0, 2)),
        "ffn_conv_w": jnp.transpose(
            ffn_conv_w[l].astype(F32).reshape(CONV_W, 2 * N_FFN_CHUNKS, FFN_CHUNK), (1, 0, 2)),
        "ffn_conv_b": ffn_conv_b[l].astype(F32).reshape(2 * N_FFN_CHUNKS, 1, FFN_CHUNK),
        "w_ffn_down": w_ffn_down[l].astype(BF16),
        "ln2_g": r2(ln2_g[l]), "ln2_b": r2(ln2_b[l]),
    }


def _tile(T, want):
    t = min(want, T)
    assert T % t == 0
    return t


def _trunk(x, mem, w):
    B, T, _ = x.shape
    M = mem.shape[1]
    assert T % MLSTM_CHUNK == 0 and T % GRID_W == 0
    x2 = x.reshape(B * T, D_MODEL)
    mem2 = mem.reshape(B * M, D_MODEL)
    qkm, vom, qa, ka, va, qc, gbr, gcol, grow = _in_proj(x2, T, w, _tile(T, 512))
    hm = _mlstm(qkm, vom, gcol, grow, T, w)
    ha = _attention(qa, ka, va, T, _tile(T, 512))
    kvc = _mem_kv(mem2, M, w)
    x1 = _merge(x2, hm, ha, qc, kvc, gbr, T, M, w, _tile(T, 512))
    y = _ffn(x1, T, w, _tile(T, 256))
    return y.reshape(B, T, D_MODEL)


def kernel(x_prompt, x_sample, mem_prompt, mem_sample, ln_in_g, ln_in_b, w_in, mlstm_gate_bias, mlstm_conv_w, mlstm_conv_b, mlstm_norm_g, attn_q_norm_g, attn_k_norm_g, w_mem_kv, w_branch_mlstm, w_branch_attn, w_branch_mem, w_out, ln1_g, ln1_b, w_ffn_up, ffn_conv_w, ffn_conv_b, w_ffn_down, ln2_g, ln2_b):
    assert DEPTH == 1 and x_prompt.shape[1] == x_sample.shape[1]
    w = _prep_weights(0, x_prompt.shape[1], ln_in_g, ln_in_b, w_in, mlstm_gate_bias, mlstm_conv_w,
                      mlstm_conv_b, mlstm_norm_g, attn_q_norm_g, attn_k_norm_g, w_mem_kv,
                      w_branch_mlstm, w_branch_attn, w_branch_mem, w_out, ln1_g, ln1_b, w_ffn_up,
                      ffn_conv_w, ffn_conv_b, w_ffn_down, ln2_g, ln2_b)
    return (_trunk(x_prompt, mem_prompt, w), _trunk(x_sample, mem_sample, w))
```

```python
import functools

import jax
import jax.numpy as jnp
from jax import lax
from jax.experimental import pallas as pl
from jax.experimental.pallas import tpu as pltpu

F32 = jnp.float32
BF16 = jnp.bfloat16

D_MODEL = 1024
DEPTH = 1
GRID_W = 64
MLSTM_HEADS = 4
MLSTM_DH = 128
MLSTM_W = MLSTM_HEADS * MLSTM_DH
MLSTM_CHUNK = 128
N_GATES = 4 * MLSTM_HEADS
ATTN_DH = 128
ATTN_HEADS = 8
KV_HEADS = 2
ATTN_GROUP = ATTN_HEADS // KV_HEADS
ATTN_W = ATTN_HEADS * ATTN_DH
KV_W = KV_HEADS * ATTN_DH
ROPE_AXIS_DIM = ATTN_DH // 2
ROPE_THETA = 10000.0
MEM_HEADS = 4
MEM_DH = 128
MEM_W = MEM_HEADS * MEM_DH
N_BRANCH = 3
D_FF = ((8 * D_MODEL // 3 + 127) // 128) * 128
CONV_W = 3
DEEPNORM_ALPHA = (2.0 * DEPTH) ** 0.25
LN_EPS = 1e-5
LOG2_E = 1.4426950408889634

LANES = 128
SUBLANES = 8
VMEM_LIMIT_BYTES = 56 * 1024 * 1024

SEG_QKM = (0, 2 * MLSTM_W)
SEG_VOM = (SEG_QKM[1], SEG_QKM[1] + 2 * MLSTM_W)
SEG_QA = (SEG_VOM[1], SEG_VOM[1] + ATTN_W)
SEG_KA = (SEG_QA[1], SEG_QA[1] + KV_W)
SEG_VA = (SEG_KA[1], SEG_KA[1] + KV_W)
SEG_QC = (SEG_VA[1], SEG_VA[1] + MEM_W)
SEG_GBR = (SEG_QC[1], SEG_QC[1] + N_BRANCH * D_MODEL)
W_CAT = SEG_GBR[1]
GATE_ROW_ORDER = tuple(g * MLSTM_HEADS + h for g in (0, 2, 1, 3) for h in range(MLSTM_HEADS))

FFN_CHUNK = 256
N_FFN_CHUNKS = D_FF // FFN_CHUNK


def _params(n_axes, flags=None):
    return pltpu.CompilerParams(dimension_semantics=("arbitrary",) * n_axes,
                                vmem_limit_bytes=VMEM_LIMIT_BYTES, flags=flags)


def _const_spec(shape):
    nd = len(shape)
    return pl.BlockSpec(shape, lambda *_: (0,) * nd, pipeline_mode=pl.Buffered(1))


def _layer_norm(x, g, b):
    mu = jnp.mean(x, axis=-1, keepdims=True)
    xc = x - mu
    var = jnp.mean(xc * xc, axis=-1, keepdims=True)
    return xc * lax.rsqrt(var + LN_EPS) * g + b


def _sigmoid(x):
    return 1.0 / (1.0 + jnp.exp(-x))


def _log_sigmoid(x):
    return jnp.minimum(x, 0.0) - jnp.log(1.0 + jnp.exp(-jnp.abs(x)))


def _dot(a, b):
    return jnp.dot(a, b, preferred_element_type=F32)


def _dot_nt(a, b):
    return lax.dot_general(a, b, (((1,), (1,)), ((), ())), preferred_element_type=F32)


def _split3(x):
    x1 = x.astype(BF16)
    r1 = x - x1.astype(F32)
    x2 = r1.astype(BF16)
    r2 = r1 - x2.astype(F32)
    return x1, x2, r2.astype(BF16)


def _rms_rope(xh, cos_g, sin_g):
    ms = jnp.mean(xh * xh, axis=-1, keepdims=True)
    xh = xh * lax.rsqrt(ms + LN_EPS)
    return xh * cos_g + pltpu.roll(xh, LANES // 2, 1) * sin_g


def _proj_kernel(x_ref, lng_ref, lnb_ref, w_ref, wgt_ref, gbr_row_ref,
                 cq_ref, sq_ref, ck_ref, sk_ref,
                 qkm_ref, vom_ref, qa_ref, ka_ref, va_ref, qc_ref, gbr_ref, gi_ref, gf_ref):
    tm = x_ref.shape[0]
    xn = _layer_norm(x_ref[...], lng_ref[...], lnb_ref[...])
    xb = xn.astype(BF16)

    def proj(lo, hi):
        return _dot(xb, w_ref[:, lo:hi])

    plain = []
    for out_ref, seg, piece in ((qkm_ref, SEG_QKM, 512), (vom_ref, SEG_VOM, 512), (va_ref, SEG_VA, KV_W),
                                (qc_ref, SEG_QC, 512), (gbr_ref, SEG_GBR, 512)):
        plain += [(out_ref, c - seg[0], c, piece) for c in range(seg[0], seg[1], piece)]
    pair = 2 * ATTN_DH
    roped = [(qa_ref, c - SEG_QA[0], c, cq_ref, sq_ref) for c in range(SEG_QA[0], SEG_QA[1], pair)]
    roped += [(ka_ref, c - SEG_KA[0], c, ck_ref, sk_ref) for c in range(SEG_KA[0], SEG_KA[1], pair)]

    def emit_plain(k):
        for out_ref, dst, src, piece in plain[:k]:
            out_ref[:, dst:dst + piece] = proj(src, src + piece).astype(out_ref.dtype)
        del plain[:k]

    per = -(-len(plain) // len(roped))
    for out_ref, dst, src, c_ref, s_ref in roped:
        blk = proj(src, src + pair)
        emit_plain(per)
        for j in range(2):
            xh = _rms_rope(blk[:, j * ATTN_DH:(j + 1) * ATTN_DH], c_ref[...], s_ref[...])
            out_ref[:, dst + j * ATTN_DH:dst + (j + 1) * ATTN_DH] = xh.astype(BF16)
    emit_plain(len(plain))

    grow = _dot_nt(wgt_ref[...], xb)
    nd = N_GATES // 2
    for j in range(tm // LANES):
        blk = grow[:, j * LANES:(j + 1) * LANES] + gbr_row_ref[...]
        gi_ref[j * nd:(j + 1) * nd, :] = blk[:nd]
        gf_ref[j * nd:(j + 1) * nd, :] = blk[nd:]


def _in_proj(x2, T, w, tm):
    n = x2.shape[0]
    grid = (n // tm,)
    tiles_per_seq = T // tm
    row = lambda i: (i, 0)
    out_shapes = (
        jax.ShapeDtypeStruct((n, 2 * MLSTM_W), BF16),
        jax.ShapeDtypeStruct((n, 2 * MLSTM_W), BF16),
        jax.ShapeDtypeStruct((n, ATTN_W), BF16),
        jax.ShapeDtypeStruct((n, KV_W), BF16),
        jax.ShapeDtypeStruct((n, KV_W), BF16),
        jax.ShapeDtypeStruct((n, MEM_W), BF16),
        jax.ShapeDtypeStruct((n, N_BRANCH * D_MODEL), BF16),
        jax.ShapeDtypeStruct((n // LANES * (N_GATES // 2), LANES), F32),
        jax.ShapeDtypeStruct((n // LANES * (N_GATES // 2), LANES), F32),
    )
    out_specs = (
        pl.BlockSpec((tm, 2 * MLSTM_W), row),
        pl.BlockSpec((tm, 2 * MLSTM_W), row),
        pl.BlockSpec((tm, ATTN_W), row),
        pl.BlockSpec((tm, KV_W), row),
        pl.BlockSpec((tm, KV_W), row),
        pl.BlockSpec((tm, MEM_W), row),
        pl.BlockSpec((tm, N_BRANCH * D_MODEL), row),
        pl.BlockSpec((tm // LANES * (N_GATES // 2), LANES), row),
        pl.BlockSpec((tm // LANES * (N_GATES // 2), LANES), row),
    )
    in_specs = [
        pl.BlockSpec((tm, D_MODEL), row),
        _const_spec((1, D_MODEL)), _const_spec((1, D_MODEL)),
        _const_spec((D_MODEL, W_CAT)), _const_spec((N_GATES, D_MODEL)),
        _const_spec((N_GATES, LANES)),
    ] + [pl.BlockSpec((tm, LANES), lambda i: (i % tiles_per_seq, 0))] * 4
    return pl.pallas_call(
        _proj_kernel, grid=grid, in_specs=in_specs, out_specs=out_specs, out_shape=out_shapes,
        compiler_params=_params(1), name="in_proj",
    )(x2, w["ln_in_g"], w["ln_in_b"], w["w_cat"], w["w_gt"], w["gb_row"],
      w["rope_cos_q"], w["rope_sin_q"], w["rope_cos_k"], w["rope_sin_k"])


N_DIRHEADS = 2 * MLSTM_HEADS
STAT_GROUPS = 7
STAT_BLOCK = (0, 0, 0, 1, 2, 2, 2)


def _mlstm_select():
    lane = jnp.arange(LANES)
    grp, j = lane // N_DIRHEADS, lane % N_DIRHEADS
    blk = jnp.array(STAT_BLOCK + (-1,) * (LANES // N_DIRHEADS - STAT_GROUPS))[grp]
    out_blk = jnp.arange(3 * LANES) // LANES
    sel = (j[None, :, None] == jnp.arange(N_DIRHEADS)[:, None, None]) & (blk[None, :, None] == out_blk[None, None, :])
    return sel.astype(BF16)


def _mlstm_kernel(qkm_ref, vom_ref, gi_ref, gf_ref, sel_ref, cw_ref, cb_ref, ng_ref, hm_ref,
                  q_s, kt_s, ktw_s, a_s, stat_s, gt_s, ma_s, mpf_s, mpb_s, r_s, wa_s, sp_s,
                  hf_s, st_s):
    T = qkm_ref.shape[0]
    L = MLSTM_CHUNK
    NC = T // L
    DH = MLSTM_DH
    H = MLSTM_HEADS

    ri = lax.broadcasted_iota(jnp.int32, (L, L), 0)
    ci = lax.broadcasted_iota(jnp.int32, (L, L), 1)
    causal = ci <= ri
    anti = ci >= ri
    tril = jnp.where(causal, 1.0, 0.0).astype(BF16)
    triu = jnp.where(anti, 1.0, 0.0).astype(BF16)
    ones_ll = jnp.ones((L, L), BF16)

    R = NC * N_DIRHEADS
    row_r = lax.broadcasted_iota(jnp.int32, (R, L), 0)
    lane_r = lax.broadcasted_iota(jnp.int32, (R, L), 1)
    is_fwd = (row_r % N_DIRHEADS) < H
    is_fwd8 = lax.broadcasted_iota(jnp.int32, (N_DIRHEADS, L), 0) < H
    sub = lax.broadcasted_iota(jnp.int32, (L, 2 * MLSTM_W), 0)

    cw0 = cw_ref[0:1, :]
    cw1 = cw_ref[1:2, :]
    cw2 = cw_ref[2:3, :]
    cb = cb_ref[...]
    lane_q = lax.broadcasted_iota(jnp.int32, (1, 2 * MLSTM_W), 1)
    kscale = jnp.where(lane_q >= MLSTM_W, MLSTM_DH ** -0.5, 1.0).astype(F32)

    li = gi_ref[...]
    y1, y2, y3 = _split3(_log_sigmoid(gf_ref[...]))

    def lane_sums(m):
        return _dot(y1, m) + _dot(y2, m) + _dot(y3, m)

    bl = jnp.where(is_fwd, lane_sums(triu), lane_sums(tril))
    gt = lane_sums(ones_ll)
    r = li - bl
    pm = r
    sm = r
    for sh in (1, 2, 4, 8, 16, 32, 64):
        pm = jnp.maximum(pm, jnp.where(lane_r >= sh, pltpu.roll(pm, sh, 1), -jnp.inf))
        sm = jnp.maximum(sm, jnp.where(lane_r < L - sh, pltpu.roll(sm, L - sh, 1), -jnp.inf))
    cm = jnp.where(is_fwd, pm, sm)
    a = gt - bl + li
    ma = jnp.broadcast_to(jnp.max(a, axis=-1, keepdims=True), (R, L))
    gt_s[...] = gt
    ma_s[...] = ma

    def chunk_rows(c):
        return pl.ds(pl.multiple_of(c * N_DIRHEADS, N_DIRHEADS), N_DIRHEADS)

    def stabiliser_scan(i, m):
        rf = chunk_rows(i)
        rb = chunk_rows(NC - 1 - i)
        mpf_s[rf, :] = m
        mpb_s[rb, :] = m
        gt8 = jnp.where(is_fwd8, gt_s[rf, :], gt_s[rb, :])
        ma8 = jnp.where(is_fwd8, ma_s[rf, :], ma_s[rb, :])
        return jnp.maximum(gt8 + m, ma8)

    lax.fori_loop(0, NC, stabiliser_scan, jnp.zeros((N_DIRHEADS, L), F32))

    m_prev = jnp.where(is_fwd, mpf_s[...], mpb_s[...])
    m_new = jnp.maximum(gt + m_prev, ma)
    sp_s[...] = jnp.exp(gt + m_prev - m_new)
    wa_s[...] = jnp.exp(a - m_new)
    r_s[...] = r
    u = -jnp.maximum(m_prev, cm)
    stat_vals = list(_split3(u)) + [jnp.exp(m_prev + u)] + list(_split3(jnp.exp(u - bl)))
    for k, t in enumerate(stat_vals):
        stat_s[k] = t.astype(F32)
    stat_pad = jnp.zeros((L - STAT_GROUPS * N_DIRHEADS, L), F32)

    def prologue(c, carry):
        r0 = pl.multiple_of(c * L, L)
        rows = pl.ds(r0, L)
        rows8 = chunk_rows(c)
        stats = [stat_s[k, rows8, :] for k in range(STAT_GROUPS)] + [stat_pad]
        a_s[rows, :] = jnp.concatenate(stats, axis=0).T.astype(BF16)
        wa = wa_s[rows8, :]
        xc = qkm_ref[rows, :].astype(F32)
        pb = pl.multiple_of(jnp.maximum(r0 - 16, 0), 16)
        nb = pl.multiple_of(jnp.minimum(r0 + L, T - 16), 16)
        prev_row = qkm_ref[pl.ds(pb, 16), :].astype(F32)[15:16, :]
        next_row = qkm_ref[pl.ds(nb, 16), :].astype(F32)[0:1, :]
        prev_row = prev_row * jnp.where(c > 0, 1.0, 0.0).astype(F32)
        next_row = next_row * jnp.where(c < NC - 1, 1.0, 0.0).astype(F32)
        x_prev = jnp.where(sub == 0, prev_row, pltpu.roll(xc, 1, 0))
        x_next = jnp.where(sub == L - 1, next_row, pltpu.roll(xc, L - 1, 0))
        y = x_prev * cw0 + cb
        y = y + xc * cw1
        y = y + x_next * cw2
        y = y * _sigmoid(y) * kscale
        q_s[rows, :] = y[:, :MLSTM_W].astype(BF16)
        for h in range(H):
            kt = y[:, MLSTM_W + h * DH:MLSTM_W + (h + 1) * DH].T
            hr = slice(h * DH, (h + 1) * DH)
            kt_s[c, hr, :] = kt.astype(BF16)
            ktw_s[0, c, hr, :] = (kt * wa[h:h + 1, :]).astype(BF16)
            ktw_s[1, c, hr, :] = (kt * wa[H + h:H + h + 1, :]).astype(BF16)
        return carry

    lax.fori_loop(0, NC, prologue, 0)

    ones_blk = jnp.ones((L, DH), BF16)
    ng = ng_ref[...]

    def run_direction(d):
        st_s[...] = jnp.zeros_like(st_s)
        mask = causal if d == 0 else anti

        def body(i, carry):
            c = i if d == 0 else NC - 1 - i
            r0 = pl.multiple_of(c * L, L)
            rows = pl.ds(r0, L)
            stats = a_s[rows, :]
            for h in range(H):
                j = H * d + h
                hs = slice(h * DH, (h + 1) * DH)
                bc = _dot(stats, sel_ref[j])
                row_j = pl.ds(c * N_DIRHEADS + j, 1)
                p = jnp.where(mask, jnp.exp(bc[:, :L] + r_s[row_j, :]), 0.0)
                q = q_s[rows, hs]
                s = _dot(q, kt_s[c, hs, :])
                lhs = jnp.concatenate([(s * p).astype(BF16), bc[:, L:2 * L].astype(BF16) * q], axis=1)
                vaug = jnp.concatenate([vom_ref[rows, hs], ones_blk], axis=1)
                c_prev = st_s[h]
                tot = _dot(lhs, jnp.concatenate([vaug, c_prev.astype(BF16)], axis=0))
                h_out = tot[:, :DH] / jnp.maximum(jnp.abs(tot[:, DH:]), bc[:, 2 * L:])
                sp = sp_s[row_j, :]
                st_s[h] = jnp.concatenate([sp, sp], axis=1) * c_prev + _dot(ktw_s[d, c, hs, :], vaug)

                if d == 0:
                    hf_s[rows, hs] = h_out
                else:
                    hh = hf_s[rows, hs] + h_out
                    mu = jnp.mean(hh, axis=-1, keepdims=True)
                    hc = hh - mu
                    var = jnp.mean(hc * hc, axis=-1, keepdims=True)
                    hn = hc * lax.rsqrt(var + LN_EPS) * ng[:, hs]
                    o = vom_ref[rows, MLSTM_W + h * DH:MLSTM_W + (h + 1) * DH].astype(F32)
                    hm_ref[rows, hs] = (hn * _sigmoid(o)).astype(BF16)
            return carry

        lax.fori_loop(0, NC, body, 0, unroll=4)

    run_direction(0)
    run_direction(1)


def _mlstm(qkm, vom, gi, gf, T, w):
    n = qkm.shape[0]
    B = n // T
    NC = T // MLSTM_CHUNK
    seq = lambda b: (b, 0)
    gate_major = pltpu.VMEM((NC * N_DIRHEADS, MLSTM_CHUNK), F32)
    return pl.pallas_call(
        _mlstm_kernel, grid=(B,),
        in_specs=[
            pl.BlockSpec((T, 2 * MLSTM_W), seq),
            pl.BlockSpec((T, 2 * MLSTM_W), seq),
            pl.BlockSpec((NC * N_DIRHEADS, MLSTM_CHUNK), seq),
            pl.BlockSpec((NC * N_DIRHEADS, MLSTM_CHUNK), seq),
            _const_spec((N_DIRHEADS, LANES, 3 * MLSTM_CHUNK)),
            _const_spec((CONV_W, 2 * MLSTM_W)), _const_spec((1, 2 * MLSTM_W)),
            _const_spec((1, MLSTM_W)),
        ],
        out_specs=pl.BlockSpec((T, MLSTM_W), seq),
        out_shape=jax.ShapeDtypeStruct((n, MLSTM_W), BF16),
        scratch_shapes=[
            pltpu.VMEM((T, MLSTM_W), BF16),
            pltpu.VMEM((NC, MLSTM_W, MLSTM_CHUNK), BF16),
            pltpu.VMEM((2, NC, MLSTM_W, MLSTM_CHUNK), BF16),
            pltpu.VMEM((T, LANES), BF16),
            pltpu.VMEM((STAT_GROUPS, NC * N_DIRHEADS, MLSTM_CHUNK), F32),
            gate_major, gate_major, gate_major, gate_major,
            gate_major, gate_major, gate_major,
            pltpu.VMEM((T, MLSTM_W), F32),
            pltpu.VMEM((MLSTM_HEADS, MLSTM_DH, 2 * MLSTM_DH), F32),
        ],
        compiler_params=_params(1), name="mlstm",
    )(qkm, vom, gi, gf, w["mlstm_sel"], w["mlstm_conv_w"], w["mlstm_conv_b"], w["mlstm_norm_g"])


def _attn_kernel(q_ref, k_ref, v_ref, o_ref, vaug_s):
    @pl.when(pl.program_id(2) == 0)
    def _():
        vaug_s[:, :ATTN_DH] = v_ref[...]
        vaug_s[:, ATTN_DH:] = jnp.ones((v_ref.shape[0], ATTN_DH), BF16)

    k = k_ref[...]
    for g in range(ATTN_GROUP):
        hs = slice(g * ATTN_DH, (g + 1) * ATTN_DH)
        s = _dot_nt(q_ref[:, hs], k)
        m = jnp.max(s, axis=-1, keepdims=True)
        p = jnp.exp2(s - m)
        oa = _dot(p.astype(BF16), vaug_s[...])
        o_ref[:, hs] = (oa[:, :ATTN_DH] / oa[:, ATTN_DH:ATTN_DH + 1]).astype(o_ref.dtype)


def _attention(qa, ka, va, T, tq):
    n = qa.shape[0]
    B = n // T
    nq = T // tq
    gw = ATTN_GROUP * ATTN_DH
    return pl.pallas_call(
        _attn_kernel, grid=(B, KV_HEADS, nq),
        in_specs=[
            pl.BlockSpec((tq, gw), lambda b, h, i: (b * nq + i, h)),
            pl.BlockSpec((T, ATTN_DH), lambda b, h, i: (b, h)),
            pl.BlockSpec((T, ATTN_DH), lambda b, h, i: (b, h)),
        ],
        out_specs=pl.BlockSpec((tq, gw), lambda b, h, i: (b * nq + i, h)),
        out_shape=jax.ShapeDtypeStruct((n, ATTN_W), BF16),
        scratch_shapes=[pltpu.VMEM((T, 2 * ATTN_DH), BF16)],
        compiler_params=_params(3), name="gqa_attn",
    )(qa, ka, va)


def _mem_kv_kernel(m_ref, w_ref, o_ref):
    o_ref[...] = _dot(m_ref[...].astype(BF16), w_ref[...]).astype(o_ref.dtype)


def _mem_kv(mem2, M, w):
    n = mem2.shape[0]
    return pl.pallas_call(
        _mem_kv_kernel, grid=(n // M,),
        in_specs=[pl.BlockSpec((M, D_MODEL), lambda b: (b, 0)),
                  _const_spec((D_MODEL, 2 * MEM_W))],
        out_specs=pl.BlockSpec((M, 2 * MEM_W), lambda b: (b, 0)),
        out_shape=jax.ShapeDtypeStruct((n, 2 * MEM_W), BF16),
        compiler_params=_params(1), name="mem_kv",
    )(mem2, w["w_mem_kv"])


def _merge_kernel(x_ref, hm_ref, ha_ref, qc_ref, kv_ref, gbr_ref, lng_ref, lnb_ref,
                  wbm_ref, wba_ref, wbc_ref, wo_ref, l1g_ref, l1b_ref, x1_ref):
    hc_parts = []
    for h in range(MEM_HEADS):
        hs = slice(h * MEM_DH, (h + 1) * MEM_DH)
        kc = kv_ref[:, hs]
        vc = kv_ref[:, MEM_W + h * MEM_DH:MEM_W + (h + 1) * MEM_DH]
        s = _dot_nt(qc_ref[:, hs], kc) * (MEM_DH ** -0.5)
        m = jnp.max(s, axis=-1, keepdims=True)
        p = jnp.exp(s - m)
        l = jnp.sum(p, axis=-1, keepdims=True)
        hc_parts.append((_dot(p.astype(BF16), vc) / l).astype(BF16))
    hc = jnp.concatenate(hc_parts, axis=1)

    def gate(j):
        return _sigmoid(gbr_ref[:, j * D_MODEL:(j + 1) * D_MODEL].astype(F32))

    merged = gate(0) * _dot(hm_ref[...], wbm_ref[...])
    merged = merged + gate(1) * _dot(ha_ref[...], wba_ref[...])
    merged = merged + gate(2) * _dot(hc, wbc_ref[...])
    mixed = _dot(merged.astype(BF16), wo_ref[...])
    xn = _layer_norm(x_ref[...], lng_ref[...], lnb_ref[...])
    x1_ref[...] = _layer_norm(DEEPNORM_ALPHA * xn + mixed, l1g_ref[...], l1b_ref[...])


def _merge(x2, hm, ha, qc, kvc, gbr, T, M, w, tm):
    n = x2.shape[0]
    tiles_per_seq = T // tm
    row = lambda i: (i, 0)
    return pl.pallas_call(
        _merge_kernel, grid=(n // tm,),
        in_specs=[
            pl.BlockSpec((tm, D_MODEL), row),
            pl.BlockSpec((tm, MLSTM_W), row),
            pl.BlockSpec((tm, ATTN_W), row),
            pl.BlockSpec((tm, MEM_W), row),
            pl.BlockSpec((M, 2 * MEM_W), lambda i: (i // tiles_per_seq, 0)),
            pl.BlockSpec((tm, N_BRANCH * D_MODEL), row),
            _const_spec((1, D_MODEL)), _const_spec((1, D_MODEL)),
            _const_spec((MLSTM_W, D_MODEL)), _const_spec((ATTN_W, D_MODEL)),
            _const_spec((MEM_W, D_MODEL)), _const_spec((D_MODEL, D_MODEL)),
            _const_spec((1, D_MODEL)), _const_spec((1, D_MODEL)),
        ],
        out_specs=pl.BlockSpec((tm, D_MODEL), row),
        out_shape=jax.ShapeDtypeStruct((n, D_MODEL), F32),
        compiler_params=_params(1), name="merge_out",
    )(x2, hm, ha, qc, kvc, gbr, w["ln_in_g"], w["ln_in_b"], w["w_branch_mlstm"],
      w["w_branch_attn"], w["w_branch_mem"], w["w_out"], w["ln1_g"], w["ln1_b"])


def _gelu_tanh(x):
    return 0.5 * x * (1.0 + jnp.tanh(0.7978845608028654 * (x + 0.044715 * (x * x * x))))


def _ffn_kernel(xp_ref, x_ref, xn_ref, wup_ref, cw_ref, cb_ref, wdn_ref, l2g_ref, l2b_ref,
                y_ref, xe_s, act_s, *, tiles_per_seq):
    tm = x_ref.shape[0]
    i = pl.program_id(0)
    j = i % tiles_per_seq
    prev = xp_ref[...] * jnp.where(j > 0, 1.0, 0.0).astype(F32)
    nxt = xn_ref[...] * jnp.where(j < tiles_per_seq - 1, 1.0, 0.0).astype(F32)
    xe_s[...] = jnp.concatenate([prev, x_ref[...], nxt], axis=0).astype(BF16)
    te = tm + 2 * SUBLANES

    def conv(u, cw, cb):
        up = pltpu.roll(u, 1, 0)[SUBLANES:SUBLANES + tm, :]
        uc = u[SUBLANES:SUBLANES + tm, :]
        un = pltpu.roll(u, te - 1, 0)[SUBLANES:SUBLANES + tm, :]
        y = up * cw[0:1, :] + cb
        y = y + uc * cw[1:2, :]
        return y + un * cw[2:3, :]

    for c in range(N_FFN_CHUNKS):
        cv = N_FFN_CHUNKS + c
        ug = conv(_dot(xe_s[...], wup_ref[c]), cw_ref[c], cb_ref[c])
        uv = conv(_dot(xe_s[...], wup_ref[cv]), cw_ref[cv], cb_ref[cv])
        act_s[:, c * FFN_CHUNK:(c + 1) * FFN_CHUNK] = (_gelu_tanh(ug) * uv).astype(BF16)

    ff = _dot(act_s[...], wdn_ref[...])
    y_ref[...] = _layer_norm(DEEPNORM_ALPHA * x_ref[...] + ff, l2g_ref[...], l2b_ref[...])


def _ffn(x1, T, w, tm):
    n = x1.shape[0]
    tiles_per_seq = T // tm
    r8 = tm // SUBLANES
    nblk8 = n // SUBLANES
    row = lambda i: (i, 0)
    return pl.pallas_call(
        functools.partial(_ffn_kernel, tiles_per_seq=tiles_per_seq), grid=(n // tm,),
        in_specs=[
            pl.BlockSpec((SUBLANES, D_MODEL), lambda i: (jnp.maximum(i * r8 - 1, 0), 0)),
            pl.BlockSpec((tm, D_MODEL), row),
            pl.BlockSpec((SUBLANES, D_MODEL), lambda i: (jnp.minimum((i + 1) * r8, nblk8 - 1), 0)),
            _const_spec((2 * N_FFN_CHUNKS, D_MODEL, FFN_CHUNK)),
            _const_spec((2 * N_FFN_CHUNKS, CONV_W, FFN_CHUNK)),
            _const_spec((2 * N_FFN_CHUNKS, 1, FFN_CHUNK)),
            _const_spec((D_FF, D_MODEL)),
            _const_spec((1, D_MODEL)), _const_spec((1, D_MODEL)),
        ],
        out_specs=pl.BlockSpec((tm, D_MODEL), row),
        out_shape=jax.ShapeDtypeStruct((n, D_MODEL), F32),
        scratch_shapes=[pltpu.VMEM((tm + 2 * SUBLANES, D_MODEL), BF16),
                        pltpu.VMEM((tm, D_FF), BF16)],
        compiler_params=_params(1), name="conv_ffn",
    )(x1, x1, x1, w["w_ffn_up"], w["ffn_conv_w"], w["ffn_conv_b"], w["w_ffn_down"],
      w["ln2_g"], w["ln2_b"])


def _rope_tables(T):
    rows = T // GRID_W
    row = jnp.repeat(jnp.arange(rows, dtype=F32), GRID_W)
    col = jnp.tile(jnp.arange(GRID_W, dtype=F32), rows)
    inv_freq = ROPE_THETA ** (-jnp.arange(0, ROPE_AXIS_DIM, 2, dtype=F32) / ROPE_AXIS_DIM)
    ang_r = row[:, None] * inv_freq
    ang_c = col[:, None] * inv_freq
    cos = jnp.concatenate([jnp.cos(ang_r), jnp.cos(ang_c)] * 2, axis=-1)
    sin = jnp.concatenate([-jnp.sin(ang_r), -jnp.sin(ang_c), jnp.sin(ang_r), jnp.sin(ang_c)], axis=-1)
    return cos, sin


def _rope_head_perm():
    p = jnp.arange(ATTN_DH)
    quarter = ROPE_AXIS_DIM // 2
    half, axis, j = p // ROPE_AXIS_DIM, (p % ROPE_AXIS_DIM) // quarter, p % quarter
    return axis * ROPE_AXIS_DIM + half * quarter + j


def _gained_tables(cos, sin, g, scale):
    g = g.astype(F32)[_rope_head_perm()]
    return cos * (g * scale), sin * (jnp.roll(g, ATTN_DH // 2) * scale)


def _prep_weights(l, T, ln_in_g, ln_in_b, w_in, mlstm_gate_bias, mlstm_conv_w, mlstm_conv_b,
                  mlstm_norm_g, attn_q_norm_g, attn_k_norm_g, w_mem_kv, w_branch_mlstm,
                  w_branch_attn, w_branch_mem, w_out, ln1_g, ln1_b, w_ffn_up, ffn_conv_w,
                  ffn_conv_b, w_ffn_down, ln2_g, ln2_b):
    wi = w_in[l]
    g0 = 4 * MLSTM_W
    g1 = g0 + N_GATES
    w_gate = wi[:, g0:g1]
    perm = _rope_head_perm()
    qk0 = g1
    qk1 = qk0 + ATTN_W + KV_W
    n_qk_heads = ATTN_HEADS + KV_HEADS
    qk_cols = qk0 + (jnp.arange(n_qk_heads)[:, None] * ATTN_DH + perm[None, :]).reshape(-1)
    w_cat = jnp.concatenate([wi[:, :g0], wi[:, qk_cols], wi[:, qk1:]], axis=1).astype(BF16)
    gate_rows = jnp.array(GATE_ROW_ORDER)
    w_gate = w_gate[:, gate_rows]
    gb = mlstm_gate_bias[l].astype(F32)[gate_rows]
    cos, sin = _rope_tables(T)
    cos_q, sin_q = _gained_tables(cos, sin, attn_q_norm_g[l], ATTN_DH ** -0.5 * LOG2_E)
    cos_k, sin_k = _gained_tables(cos, sin, attn_k_norm_g[l], 1.0)
    r2 = lambda a: a.reshape(1, -1).astype(F32)
    up = w_ffn_up[l].astype(BF16).reshape(D_MODEL, 2 * N_FFN_CHUNKS, FFN_CHUNK)
    return {
        "ln_in_g": r2(ln_in_g), "ln_in_b": r2(ln_in_b),
        "w_cat": w_cat, "w_gt": w_gate.T.astype(BF16),
        "gb_row": jnp.broadcast_to(gb[:, None], (N_GATES, LANES)),
        "mlstm_sel": _mlstm_select(),
        "rope_cos_q": cos_q, "rope_sin_q": sin_q, "rope_cos_k": cos_k, "rope_sin_k": sin_k,
        "mlstm_conv_w": mlstm_conv_w[l].astype(F32), "mlstm_conv_b": r2(mlstm_conv_b[l]),
        "mlstm_norm_g": r2(mlstm_norm_g[l]),
        "w_mem_kv": w_mem_kv[l].astype(BF16),
        "w_branch_mlstm": w_branch_mlstm[l].astype(BF16),
        "w_branch_attn": w_branch_attn[l].astype(BF16),
        "w_branch_mem": w_branch_mem[l].astype(BF16),
        "w_out": w_out[l].astype(BF16),
        "ln1_g": r2(ln1_g[l]), "ln1_b": r2(ln1_b[l]),
        "w_ffn_up": jnp.transpose(up, (1, 0, 2)),
        "ffn_conv_w": jnp.transpose(
            ffn_conv_w[l].astype(F32).reshape(CONV_W, 2 * N_FFN_CHUNKS, FFN_CHUNK), (1, 0, 2)),
        "ffn_conv_b": ffn_conv_b[l].astype(F32).reshape(2 * N_FFN_CHUNKS, 1, FFN_CHUNK),
        "w_ffn_down": w_ffn_down[l].astype(BF16),
        "ln2_g": r2(ln2_g[l]), "ln2_b": r2(ln2_b[l]),
    }


def _tile(T, want):
    t = min(want, T)
    assert T % t == 0
    return t


def _trunk(x, mem, w):
    B, T, _ = x.shape
    M = mem.shape[1]
    assert T % MLSTM_CHUNK == 0 and T % GRID_W == 0
    x2 = x.reshape(B * T, D_MODEL)
    mem2 = mem.reshape(B * M, D_MODEL)
    qkm, vom, qa, ka, va, qc, gbr, gi, gf = _in_proj(x2, T, w, _tile(T, 512))
    hm = _mlstm(qkm, vom, gi, gf, T, w)
    ha = _attention(qa, ka, va, T, _tile(T, 512))
    kvc = _mem_kv(mem2, M, w)
    x1 = _merge(x2, hm, ha, qc, kvc, gbr, T, M, w, _tile(T, 512))
    y = _ffn(x1, T, w, _tile(T, 256))
    return y.reshape(B, T, D_MODEL)


def kernel(x_prompt, x_sample, mem_prompt, mem_sample, ln_in_g, ln_in_b, w_in, mlstm_gate_bias, mlstm_conv_w, mlstm_conv_b, mlstm_norm_g, attn_q_norm_g, attn_k_norm_g, w_mem_kv, w_branch_mlstm, w_branch_attn, w_branch_mem, w_out, ln1_g, ln1_b, w_ffn_up, ffn_conv_w, ffn_conv_b, w_ffn_down, ln2_g, ln2_b):
    assert DEPTH == 1 and x_prompt.shape[1] == x_sample.shape[1]
    w = _prep_weights(0, x_prompt.shape[1], ln_in_g, ln_in_b, w_in, mlstm_gate_bias, mlstm_conv_w,
                      mlstm_conv_b, mlstm_norm_g, attn_q_norm_g, attn_k_norm_g, w_mem_kv,
                      w_branch_mlstm, w_branch_attn, w_branch_mem, w_out, ln1_g, ln1_b, w_ffn_up,
                      ffn_conv_w, ffn_conv_b, w_ffn_down, ln2_g, ln2_b)
    return (_trunk(x_prompt, mem_prompt, w), _trunk(x_sample, mem_sample, w))
```

```python
import functools

import jax
import jax.numpy as jnp
from jax import lax
from jax.experimental import pallas as pl
from jax.experimental.pallas import tpu as pltpu

F32 = jnp.float32
BF16 = jnp.bfloat16

D_MODEL = 1024
DEPTH = 1
GRID_W = 64
MLSTM_HEADS = 4
MLSTM_DH = 128
MLSTM_W = MLSTM_HEADS * MLSTM_DH
MLSTM_CHUNK = 128
N_GATES = 4 * MLSTM_HEADS
ATTN_DH = 128
ATTN_HEADS = 8
KV_HEADS = 2
ATTN_GROUP = ATTN_HEADS // KV_HEADS
ATTN_W = ATTN_HEADS * ATTN_DH
KV_W = KV_HEADS * ATTN_DH
ROPE_AXIS_DIM = ATTN_DH // 2
ROPE_THETA = 10000.0
MEM_HEADS = 4
MEM_DH = 128
MEM_W = MEM_HEADS * MEM_DH
N_BRANCH = 3
D_FF = ((8 * D_MODEL // 3 + 127) // 128) * 128
CONV_W = 3
DEEPNORM_ALPHA = (2.0 * DEPTH) ** 0.25
LN_EPS = 1e-5
LOG2_E = 1.4426950408889634

LANES = 128
SUBLANES = 8
VMEM_LIMIT_BYTES = 56 * 1024 * 1024

SEG_QKM = (0, 2 * MLSTM_W)
SEG_VOM = (SEG_QKM[1], SEG_QKM[1] + 2 * MLSTM_W)
SEG_QA = (SEG_VOM[1], SEG_VOM[1] + ATTN_W)
SEG_KA = (SEG_QA[1], SEG_QA[1] + KV_W)
SEG_VA = (SEG_KA[1], SEG_KA[1] + KV_W)
SEG_QC = (SEG_VA[1], SEG_VA[1] + MEM_W)
SEG_GBR = (SEG_QC[1], SEG_QC[1] + N_BRANCH * D_MODEL)
W_CAT = SEG_GBR[1]
GATE_ROW_ORDER = tuple(g * MLSTM_HEADS + h for g in (0, 2, 1, 3) for h in range(MLSTM_HEADS))

FFN_CHUNK = 256
ATTN_UNIT_ROWS = 128


def _params(n_axes, flags=None):
    return pltpu.CompilerParams(dimension_semantics=("arbitrary",) * n_axes,
                                vmem_limit_bytes=VMEM_LIMIT_BYTES, flags=flags)


def _const_spec(shape):
    nd = len(shape)
    return pl.BlockSpec(shape, lambda *_: (0,) * nd, pipeline_mode=pl.Buffered(1))


def _layer_norm(x, g, b):
    mu = jnp.mean(x, axis=-1, keepdims=True)
    xc = x - mu
    var = jnp.mean(xc * xc, axis=-1, keepdims=True)
    return xc * lax.rsqrt(var + LN_EPS) * g + b


def _sigmoid(x):
    return 1.0 / (1.0 + jnp.exp(-x))


def _log_sigmoid(x):
    return jnp.minimum(x, 0.0) - jnp.log(1.0 + jnp.exp(-jnp.abs(x)))


def _dot(a, b):
    return jnp.dot(a, b, preferred_element_type=F32)


def _dot_nt(a, b):
    return lax.dot_general(a, b, (((1,), (1,)), ((), ())), preferred_element_type=F32)


def _split3(x):
    x1 = x.astype(BF16)
    r1 = x - x1.astype(F32)
    x2 = r1.astype(BF16)
    r2 = r1 - x2.astype(F32)
    return x1, x2, r2.astype(BF16)


def _rms_rope(xh, cos_g, sin_g):
    ms = jnp.mean(xh * xh, axis=-1, keepdims=True)
    xh = xh * lax.rsqrt(ms + LN_EPS)
    return xh * cos_g + pltpu.roll(xh, LANES // 2, 1) * sin_g


def _proj_kernel(x_ref, lng_ref, lnb_ref, w_ref, wgt_ref, gbr_row_ref,
                 cq_ref, sq_ref, ck_ref, sk_ref,
                 qkm_ref, vom_ref, qa_ref, ka_ref, va_ref, qc_ref, gbr_ref, gi_ref, gf_ref):
    tm = x_ref.shape[0]
    xn = _layer_norm(x_ref[...], lng_ref[...], lnb_ref[...])
    xb = xn.astype(BF16)

    def proj(lo, hi):
        return _dot(xb, w_ref[:, lo:hi])

    plain = []
    for out_ref, seg, piece in ((qkm_ref, SEG_QKM, 512), (vom_ref, SEG_VOM, 512), (va_ref, SEG_VA, KV_W),
                                (qc_ref, SEG_QC, 512), (gbr_ref, SEG_GBR, 512)):
        plain += [(out_ref, c - seg[0], c, piece) for c in range(seg[0], seg[1], piece)]
    pair = 2 * ATTN_DH
    roped = [(qa_ref, c - SEG_QA[0], c, cq_ref, sq_ref) for c in range(SEG_QA[0], SEG_QA[1], pair)]
    roped += [(ka_ref, c - SEG_KA[0], c, ck_ref, sk_ref) for c in range(SEG_KA[0], SEG_KA[1], pair)]

    def emit_plain(k):
        for out_ref, dst, src, piece in plain[:k]:
            out_ref[:, dst:dst + piece] = proj(src, src + piece).astype(out_ref.dtype)
        del plain[:k]

    per = -(-len(plain) // len(roped))
    for out_ref, dst, src, c_ref, s_ref in roped:
        blk = proj(src, src + pair)
        emit_plain(per)
        for j in range(2):
            xh = _rms_rope(blk[:, j * ATTN_DH:(j + 1) * ATTN_DH], c_ref[...], s_ref[...])
            out_ref[:, dst + j * ATTN_DH:dst + (j + 1) * ATTN_DH] = xh.astype(BF16)
    emit_plain(len(plain))

    grow = _dot_nt(wgt_ref[...], xb)
    nd = N_GATES // 2
    for j in range(tm // LANES):
        blk = grow[:, j * LANES:(j + 1) * LANES] + gbr_row_ref[...]
        gi_ref[j * nd:(j + 1) * nd, :] = blk[:nd]
        gf_ref[j * nd:(j + 1) * nd, :] = blk[nd:]


def _in_proj(x2, T, w, tm):
    n = x2.shape[0]
    grid = (n // tm,)
    tiles_per_seq = T // tm
    row = lambda i: (i, 0)
    out_shapes = (
        jax.ShapeDtypeStruct((n, 2 * MLSTM_W), BF16),
        jax.ShapeDtypeStruct((n, 2 * MLSTM_W), BF16),
        jax.ShapeDtypeStruct((n, ATTN_W), BF16),
        jax.ShapeDtypeStruct((n, KV_W), BF16),
        jax.ShapeDtypeStruct((n, KV_W), BF16),
        jax.ShapeDtypeStruct((n, MEM_W), BF16),
        jax.ShapeDtypeStruct((n, N_BRANCH * D_MODEL), BF16),
        jax.ShapeDtypeStruct((n // LANES * (N_GATES // 2), LANES), F32),
        jax.ShapeDtypeStruct((n // LANES * (N_GATES // 2), LANES), F32),
    )
    out_specs = (
        pl.BlockSpec((tm, 2 * MLSTM_W), row),
        pl.BlockSpec((tm, 2 * MLSTM_W), row),
        pl.BlockSpec((tm, ATTN_W), row),
        pl.BlockSpec((tm, KV_W), row),
        pl.BlockSpec((tm, KV_W), row),
        pl.BlockSpec((tm, MEM_W), row),
        pl.BlockSpec((tm, N_BRANCH * D_MODEL), row),
        pl.BlockSpec((tm // LANES * (N_GATES // 2), LANES), row),
        pl.BlockSpec((tm // LANES * (N_GATES // 2), LANES), row),
    )
    in_specs = [
        pl.BlockSpec((tm, D_MODEL), row),
        _const_spec((1, D_MODEL)), _const_spec((1, D_MODEL)),
        _const_spec((D_MODEL, W_CAT)), _const_spec((N_GATES, D_MODEL)),
        _const_spec((N_GATES, LANES)),
    ] + [pl.BlockSpec((tm, LANES), lambda i: (i % tiles_per_seq, 0))] * 4
    return pl.pallas_call(
        _proj_kernel, grid=grid, in_specs=in_specs, out_specs=out_specs, out_shape=out_shapes,
        compiler_params=_params(1), name="in_proj",
    )(x2, w["ln_in_g"], w["ln_in_b"], w["w_cat"], w["w_gt"], w["gb_row"],
      w["rope_cos_q"], w["rope_sin_q"], w["rope_cos_k"], w["rope_sin_k"])


N_DIRHEADS = 2 * MLSTM_HEADS
STAT_GROUPS = 7
STAT_BLOCK = (0, 0, 0, 1, 2, 2, 2)


def _mlstm_select():
    lane = jnp.arange(LANES)
    grp, j = lane // N_DIRHEADS, lane % N_DIRHEADS
    blk = jnp.array(STAT_BLOCK + (-1,) * (LANES // N_DIRHEADS - STAT_GROUPS))[grp]
    out_blk = jnp.arange(3 * LANES) // LANES
    sel = (j[None, :, None] == jnp.arange(N_DIRHEADS)[:, None, None]) & (blk[None, :, None] == out_blk[None, None, :])
    return sel.astype(BF16)


def _mlstm_kernel(qkm_ref, vom_ref, gi_ref, gf_ref, sel_ref, cw_ref, cb_ref, ng_ref, hm_ref,
                  q_s, kt_s, ktw_s, a_s, stat_s, gt_s, ma_s, mpf_s, mpb_s, r_s, wa_s, sp_s,
                  hf_s, st_s):
    T = qkm_ref.shape[0]
    L = MLSTM_CHUNK
    NC = T // L
    DH = MLSTM_DH
    H = MLSTM_HEADS

    ri = lax.broadcasted_iota(jnp.int32, (L, L), 0)
    ci = lax.broadcasted_iota(jnp.int32, (L, L), 1)
    causal = ci <= ri
    anti = ci >= ri
    tril = jnp.where(causal, 1.0, 0.0).astype(BF16)
    triu = jnp.where(anti, 1.0, 0.0).astype(BF16)
    ones_ll = jnp.ones((L, L), BF16)

    R = NC * N_DIRHEADS
    row_r = lax.broadcasted_iota(jnp.int32, (R, L), 0)
    lane_r = lax.broadcasted_iota(jnp.int32, (R, L), 1)
    is_fwd = (row_r % N_DIRHEADS) < H
    is_fwd8 = lax.broadcasted_iota(jnp.int32, (N_DIRHEADS, L), 0) < H
    sub = lax.broadcasted_iota(jnp.int32, (L, 2 * MLSTM_W), 0)

    cw0 = cw_ref[0:1, :]
    cw1 = cw_ref[1:2, :]
    cw2 = cw_ref[2:3, :]
    cb = cb_ref[...]
    lane_q = lax.broadcasted_iota(jnp.int32, (1, 2 * MLSTM_W), 1)
    kscale = jnp.where(lane_q >= MLSTM_W, MLSTM_DH ** -0.5, 1.0).astype(F32)

    li = gi_ref[...]
    y1, y2, y3 = _split3(_log_sigmoid(gf_ref[...]))

    def lane_sums(m):
        return _dot(y1, m) + _dot(y2, m) + _dot(y3, m)

    bl = jnp.where(is_fwd, lane_sums(triu), lane_sums(tril))
    gt = lane_sums(ones_ll)
    r = li - bl
    pm = r
    sm = r
    for sh in (1, 2, 4, 8, 16, 32, 64):
        pm = jnp.maximum(pm, jnp.where(lane_r >= sh, pltpu.roll(pm, sh, 1), -jnp.inf))
        sm = jnp.maximum(sm, jnp.where(lane_r < L - sh, pltpu.roll(sm, L - sh, 1), -jnp.inf))
    cm = jnp.where(is_fwd, pm, sm)
    a = gt - bl + li
    ma = jnp.broadcast_to(jnp.max(a, axis=-1, keepdims=True), (R, L))
    gt_s[...] = gt
    ma_s[...] = ma

    def chunk_rows(c):
        return pl.ds(pl.multiple_of(c * N_DIRHEADS, N_DIRHEADS), N_DIRHEADS)

    def stabiliser_scan(i, m):
        rf = chunk_rows(i)
        rb = chunk_rows(NC - 1 - i)
        mpf_s[rf, :] = m
        mpb_s[rb, :] = m
        gt8 = jnp.where(is_fwd8, gt_s[rf, :], gt_s[rb, :])
        ma8 = jnp.where(is_fwd8, ma_s[rf, :], ma_s[rb, :])
        return jnp.maximum(gt8 + m, ma8)

    lax.fori_loop(0, NC, stabiliser_scan, jnp.zeros((N_DIRHEADS, L), F32))

    m_prev = jnp.where(is_fwd, mpf_s[...], mpb_s[...])
    m_new = jnp.maximum(gt + m_prev, ma)
    sp_s[...] = jnp.exp(gt + m_prev - m_new)
    wa_s[...] = jnp.exp(a - m_new)
    r_s[...] = r
    u = -jnp.maximum(m_prev, cm)
    stat_vals = list(_split3(u)) + [jnp.exp(m_prev + u)] + list(_split3(jnp.exp(u - bl)))
    for k, t in enumerate(stat_vals):
        stat_s[k] = t.astype(F32)
    stat_pad = jnp.zeros((L - STAT_GROUPS * N_DIRHEADS, L), F32)

    def prologue(c, carry):
        r0 = pl.multiple_of(c * L, L)
        rows = pl.ds(r0, L)
        rows8 = chunk_rows(c)
        stats = [stat_s[k, rows8, :] for k in range(STAT_GROUPS)] + [stat_pad]
        a_s[rows, :] = jnp.concatenate(stats, axis=0).T.astype(BF16)
        wa = wa_s[rows8, :]
        xc = qkm_ref[rows, :].astype(F32)
        pb = pl.multiple_of(jnp.maximum(r0 - 16, 0), 16)
        nb = pl.multiple_of(jnp.minimum(r0 + L, T - 16), 16)
        prev_row = qkm_ref[pl.ds(pb, 16), :].astype(F32)[15:16, :]
        next_row = qkm_ref[pl.ds(nb, 16), :].astype(F32)[0:1, :]
        prev_row = prev_row * jnp.where(c > 0, 1.0, 0.0).astype(F32)
        next_row = next_row * jnp.where(c < NC - 1, 1.0, 0.0).astype(F32)
        x_prev = jnp.where(sub == 0, prev_row, pltpu.roll(xc, 1, 0))
        x_next = jnp.where(sub == L - 1, next_row, pltpu.roll(xc, L - 1, 0))
        y = x_prev * cw0 + cb
        y = y + xc * cw1
        y = y + x_next * cw2
        y = y * _sigmoid(y) * kscale
        q_s[rows, :] = y[:, :MLSTM_W].astype(BF16)
        for h in range(H):
            kt = y[:, MLSTM_W + h * DH:MLSTM_W + (h + 1) * DH].T
            hr = slice(h * DH, (h + 1) * DH)
            kt_s[c, hr, :] = kt.astype(BF16)
            ktw_s[0, c, hr, :] = (kt * wa[h:h + 1, :]).astype(BF16)
            ktw_s[1, c, hr, :] = (kt * wa[H + h:H + h + 1, :]).astype(BF16)
        return carry

    lax.fori_loop(0, NC, prologue, 0)

    ones_blk = jnp.ones((L, DH), BF16)
    ng = ng_ref[...]

    def run_direction(d):
        st_s[...] = jnp.zeros_like(st_s)
        mask = causal if d == 0 else anti

        def body(i, carry):
            c = i if d == 0 else NC - 1 - i
            r0 = pl.multiple_of(c * L, L)
            rows = pl.ds(r0, L)
            stats = a_s[rows, :]
            for h in range(H):
                j = H * d + h
                hs = slice(h * DH, (h + 1) * DH)
                bc = _dot(stats, sel_ref[j])
                row_j = pl.ds(c * N_DIRHEADS + j, 1)
                p = jnp.where(mask, jnp.exp(bc[:, :L] + r_s[row_j, :]), 0.0)
                q = q_s[rows, hs]
                s = _dot(q, kt_s[c, hs, :])
                lhs = jnp.concatenate([(s * p).astype(BF16), bc[:, L:2 * L].astype(BF16) * q], axis=1)
                vaug = jnp.concatenate([vom_ref[rows, hs], ones_blk], axis=1)
                c_prev = st_s[h]
                tot = _dot(lhs, jnp.concatenate([vaug, c_prev.astype(BF16)], axis=0))
                h_out = tot[:, :DH] / jnp.maximum(jnp.abs(tot[:, DH:]), bc[:, 2 * L:])
                sp = sp_s[row_j, :]
                st_s[h] = jnp.concatenate([sp, sp], axis=1) * c_prev + _dot(ktw_s[d, c, hs, :], vaug)

                if d == 0:
                    hf_s[rows, hs] = h_out
                else:
                    hh = hf_s[rows, hs] + h_out
                    mu = jnp.mean(hh, axis=-1, keepdims=True)
                    hc = hh - mu
                    var = jnp.mean(hc * hc, axis=-1, keepdims=True)
                    hn = hc * lax.rsqrt(var + LN_EPS) * ng[:, hs]
                    o = vom_ref[rows, MLSTM_W + h * DH:MLSTM_W + (h + 1) * DH].astype(F32)
                    hm_ref[rows, hs] = (hn * _sigmoid(o)).astype(BF16)
            return carry

        lax.fori_loop(0, NC, body, 0, unroll=4)

    run_direction(0)
    run_direction(1)


def _mlstm(qkm, vom, gi, gf, T, w):
    n = qkm.shape[0]
    B = n // T
    NC = T // MLSTM_CHUNK
    seq = lambda b: (b, 0)
    gate_major = pltpu.VMEM((NC * N_DIRHEADS, MLSTM_CHUNK), F32)
    return pl.pallas_call(
        _mlstm_kernel, grid=(B,),
        in_specs=[
            pl.BlockSpec((T, 2 * MLSTM_W), seq),
            pl.BlockSpec((T, 2 * MLSTM_W), seq),
            pl.BlockSpec((NC * N_DIRHEADS, MLSTM_CHUNK), seq),
            pl.BlockSpec((NC * N_DIRHEADS, MLSTM_CHUNK), seq),
            _const_spec((N_DIRHEADS, LANES, 3 * MLSTM_CHUNK)),
            _const_spec((CONV_W, 2 * MLSTM_W)), _const_spec((1, 2 * MLSTM_W)),
            _const_spec((1, MLSTM_W)),
        ],
        out_specs=pl.BlockSpec((T, MLSTM_W), seq),
        out_shape=jax.ShapeDtypeStruct((n, MLSTM_W), BF16),
        scratch_shapes=[
            pltpu.VMEM((T, MLSTM_W), BF16),
            pltpu.VMEM((NC, MLSTM_W, MLSTM_CHUNK), BF16),
            pltpu.VMEM((2, NC, MLSTM_W, MLSTM_CHUNK), BF16),
            pltpu.VMEM((T, LANES), BF16),
            pltpu.VMEM((STAT_GROUPS, NC * N_DIRHEADS, MLSTM_CHUNK), F32),
            gate_major, gate_major, gate_major, gate_major,
            gate_major, gate_major, gate_major,
            pltpu.VMEM((T, MLSTM_W), F32),
            pltpu.VMEM((MLSTM_HEADS, MLSTM_DH, 2 * MLSTM_DH), F32),
        ],
        compiler_params=_params(1), name="mlstm",
    )(qkm, vom, gi, gf, w["mlstm_sel"], w["mlstm_conv_w"], w["mlstm_conv_b"], w["mlstm_norm_g"])


def _attn_kernel(q_ref, k_ref, v_ref, o_ref, vaug_s):
    @pl.when(pl.program_id(2) == 0)
    def _():
        vaug_s[:, :ATTN_DH] = v_ref[...]
        vaug_s[:, ATTN_DH:] = jnp.ones((v_ref.shape[0], ATTN_DH), BF16)

    k = k_ref[...]
    tq = q_ref.shape[0]
    rows_per_unit = min(tq, ATTN_UNIT_ROWS)
    for g in range(ATTN_GROUP):
        hs = slice(g * ATTN_DH, (g + 1) * ATTN_DH)
        for r0 in range(0, tq, rows_per_unit):
            rs = slice(r0, r0 + rows_per_unit)
            s = _dot_nt(q_ref[rs, hs], k)
            m = jnp.max(s, axis=-1, keepdims=True)
            p = jnp.exp2(s - m)
            oa = _dot(p.astype(BF16), vaug_s[...])
            o_ref[rs, hs] = (oa[:, :ATTN_DH] / oa[:, ATTN_DH:ATTN_DH + 1]).astype(o_ref.dtype)


def _attention(qa, ka, va, T, tq):
    n = qa.shape[0]
    B = n // T
    nq = T // tq
    gw = ATTN_GROUP * ATTN_DH
    return pl.pallas_call(
        _attn_kernel, grid=(B, KV_HEADS, nq),
        in_specs=[
            pl.BlockSpec((tq, gw), lambda b, h, i: (b * nq + i, h)),
            pl.BlockSpec((T, ATTN_DH), lambda b, h, i: (b, h)),
            pl.BlockSpec((T, ATTN_DH), lambda b, h, i: (b, h)),
        ],
        out_specs=pl.BlockSpec((tq, gw), lambda b, h, i: (b * nq + i, h)),
        out_shape=jax.ShapeDtypeStruct((n, ATTN_W), BF16),
        scratch_shapes=[pltpu.VMEM((T, 2 * ATTN_DH), BF16)],
        compiler_params=_params(3), name="gqa_attn",
    )(qa, ka, va)


def _mem_kv_kernel(m_ref, w_ref, o_ref):
    o_ref[...] = _dot(m_ref[...].astype(BF16), w_ref[...]).astype(o_ref.dtype)


def _mem_kv(mem2, M, w):
    n = mem2.shape[0]
    return pl.pallas_call(
        _mem_kv_kernel, grid=(n // M,),
        in_specs=[pl.BlockSpec((M, D_MODEL), lambda b: (b, 0)),
                  _const_spec((D_MODEL, 2 * MEM_W))],
        out_specs=pl.BlockSpec((M, 2 * MEM_W), lambda b: (b, 0)),
        out_shape=jax.ShapeDtypeStruct((n, 2 * MEM_W), BF16),
        compiler_params=_params(1), name="mem_kv",
    )(mem2, w["w_mem_kv"])


def _merge_kernel(x_ref, hm_ref, ha_ref, qc_ref, kv_ref, gbr_ref, lng_ref, lnb_ref,
                  wbm_ref, wba_ref, wbc_ref, wo_ref, l1g_ref, l1b_ref, x1_ref):
    hc_parts = []
    for h in range(MEM_HEADS):
        hs = slice(h * MEM_DH, (h + 1) * MEM_DH)
        kc = kv_ref[:, hs]
        vc = kv_ref[:, MEM_W + h * MEM_DH:MEM_W + (h + 1) * MEM_DH]
        s = _dot_nt(qc_ref[:, hs], kc) * (MEM_DH ** -0.5)
        m = jnp.max(s, axis=-1, keepdims=True)
        p = jnp.exp(s - m)
        l = jnp.sum(p, axis=-1, keepdims=True)
        hc_parts.append((_dot(p.astype(BF16), vc) / l).astype(BF16))
    hc = jnp.concatenate(hc_parts, axis=1)

    def gate(j):
        return _sigmoid(gbr_ref[:, j * D_MODEL:(j + 1) * D_MODEL].astype(F32))

    merged = gate(0) * _dot(hm_ref[...], wbm_ref[...])
    merged = merged + gate(1) * _dot(ha_ref[...], wba_ref[...])
    merged = merged + gate(2) * _dot(hc, wbc_ref[...])
    mixed = _dot(merged.astype(BF16), wo_ref[...])
    xn = _layer_norm(x_ref[...], lng_ref[...], lnb_ref[...])
    x1_ref[...] = _layer_norm(DEEPNORM_ALPHA * xn + mixed, l1g_ref[...], l1b_ref[...])


def _merge(x2, hm, ha, qc, kvc, gbr, T, M, w, tm):
    n = x2.shape[0]
    tiles_per_seq = T // tm
    row = lambda i: (i, 0)
    return pl.pallas_call(
        _merge_kernel, grid=(n // tm,),
        in_specs=[
            pl.BlockSpec((tm, D_MODEL), row),
            pl.BlockSpec((tm, MLSTM_W), row),
            pl.BlockSpec((tm, ATTN_W), row),
            pl.BlockSpec((tm, MEM_W), row),
            pl.BlockSpec((M, 2 * MEM_W), lambda i: (i // tiles_per_seq, 0)),
            pl.BlockSpec((tm, N_BRANCH * D_MODEL), row),
            _const_spec((1, D_MODEL)), _const_spec((1, D_MODEL)),
            _const_spec((MLSTM_W, D_MODEL)), _const_spec((ATTN_W, D_MODEL)),
            _const_spec((MEM_W, D_MODEL)), _const_spec((D_MODEL, D_MODEL)),
            _const_spec((1, D_MODEL)), _const_spec((1, D_MODEL)),
        ],
        out_specs=pl.BlockSpec((tm, D_MODEL), row),
        out_shape=jax.ShapeDtypeStruct((n, D_MODEL), F32),
        compiler_params=_params(1), name="merge_out",
    )(x2, hm, ha, qc, kvc, gbr, w["ln_in_g"], w["ln_in_b"], w["w_branch_mlstm"],
      w["w_branch_attn"], w["w_branch_mem"], w["w_out"], w["ln1_g"], w["ln1_b"])


GELU_C = 0.7978845608028654
GELU_C3 = GELU_C * 0.044715


def _gelu_tanh_x2(x):
    return x * (1.0 + jnp.tanh(x * (GELU_C + GELU_C3 * (x * x))))


def _ffn_kernel(xp_ref, x_ref, xn_ref, wup_ref, cw_ref, cb_ref, wdn_ref, l2g_ref, l2b_ref,
                y_ref, xe_s, act_s, *, tiles_per_seq):
    tm = x_ref.shape[0]
    i = pl.program_id(0)
    j = i % tiles_per_seq
    prev = xp_ref[...] * jnp.where(j > 0, 1.0, 0.0).astype(F32)
    nxt = xn_ref[...] * jnp.where(j < tiles_per_seq - 1, 1.0, 0.0).astype(F32)
    xe_s[...] = jnp.concatenate([prev, x_ref[...], nxt], axis=0).astype(BF16)
    te = tm + 2 * SUBLANES

    def conv(u, cw, cb):
        up = pltpu.roll(u, 1, 0)[SUBLANES:SUBLANES + tm, :]
        uc = u[SUBLANES:SUBLANES + tm, :]
        un = pltpu.roll(u, te - 1, 0)[SUBLANES:SUBLANES + tm, :]
        y = up * cw[0:1, :] + cb
        y = y + uc * cw[1:2, :]
        return y + un * cw[2:3, :]

    def up_conv(lo):
        cols = slice(lo, lo + FFN_CHUNK)
        return conv(_dot(xe_s[...], wup_ref[:, cols]), cw_ref[:, cols], cb_ref[:, cols])

    for c in range(0, D_FF, FFN_CHUNK):
        act_s[:, c:c + FFN_CHUNK] = (_gelu_tanh_x2(up_conv(c)) * up_conv(D_FF + c)).astype(BF16)

    ff = _dot(act_s[...], wdn_ref[...])
    y_ref[...] = _layer_norm(DEEPNORM_ALPHA * x_ref[...] + ff, l2g_ref[...], l2b_ref[...])


def _ffn(x1, T, w, tm):
    n = x1.shape[0]
    tiles_per_seq = T // tm
    r8 = tm // SUBLANES
    nblk8 = n // SUBLANES
    row = lambda i: (i, 0)
    return pl.pallas_call(
        functools.partial(_ffn_kernel, tiles_per_seq=tiles_per_seq), grid=(n // tm,),
        in_specs=[
            pl.BlockSpec((SUBLANES, D_MODEL), lambda i: (jnp.maximum(i * r8 - 1, 0), 0)),
            pl.BlockSpec((tm, D_MODEL), row),
            pl.BlockSpec((SUBLANES, D_MODEL), lambda i: (jnp.minimum((i + 1) * r8, nblk8 - 1), 0)),
            _const_spec((D_MODEL, 2 * D_FF)),
            _const_spec((CONV_W, 2 * D_FF)),
            _const_spec((1, 2 * D_FF)),
            _const_spec((D_FF, D_MODEL)),
            _const_spec((1, D_MODEL)), _const_spec((1, D_MODEL)),
        ],
        out_specs=pl.BlockSpec((tm, D_MODEL), row),
        out_shape=jax.ShapeDtypeStruct((n, D_MODEL), F32),
        scratch_shapes=[pltpu.VMEM((tm + 2 * SUBLANES, D_MODEL), BF16),
                        pltpu.VMEM((tm, D_FF), BF16)],
        compiler_params=_params(1), name="conv_ffn",
    )(x1, x1, x1, w["w_ffn_up"], w["ffn_conv_w"], w["ffn_conv_b"], w["w_ffn_down"],
      w["ln2_g"], w["ln2_b"])


def _rope_tables(T):
    rows = T // GRID_W
    row = jnp.repeat(jnp.arange(rows, dtype=F32), GRID_W)
    col = jnp.tile(jnp.arange(GRID_W, dtype=F32), rows)
    inv_freq = ROPE_THETA ** (-jnp.arange(0, ROPE_AXIS_DIM, 2, dtype=F32) / ROPE_AXIS_DIM)
    ang_r = row[:, None] * inv_freq
    ang_c = col[:, None] * inv_freq
    cos = jnp.concatenate([jnp.cos(ang_r), jnp.cos(ang_c)] * 2, axis=-1)
    sin = jnp.concatenate([-jnp.sin(ang_r), -jnp.sin(ang_c), jnp.sin(ang_r), jnp.sin(ang_c)], axis=-1)
    return cos, sin


def _rope_head_perm():
    p = jnp.arange(ATTN_DH)
    quarter = ROPE_AXIS_DIM // 2
    half, axis, j = p // ROPE_AXIS_DIM, (p % ROPE_AXIS_DIM) // quarter, p % quarter
    return axis * ROPE_AXIS_DIM + half * quarter + j


def _gained_tables(cos, sin, g, scale):
    g = g.astype(F32)[_rope_head_perm()]
    return cos * (g * scale), sin * (jnp.roll(g, ATTN_DH // 2) * scale)


def _prep_weights(l, T, ln_in_g, ln_in_b, w_in, mlstm_gate_bias, mlstm_conv_w, mlstm_conv_b,
                  mlstm_norm_g, attn_q_norm_g, attn_k_norm_g, w_mem_kv, w_branch_mlstm,
                  w_branch_attn, w_branch_mem, w_out, ln1_g, ln1_b, w_ffn_up, ffn_conv_w,
                  ffn_conv_b, w_ffn_down, ln2_g, ln2_b):
    wi = w_in[l]
    g0 = 4 * MLSTM_W
    g1 = g0 + N_GATES
    w_gate = wi[:, g0:g1]
    perm = _rope_head_perm()
    qk0 = g1
    qk1 = qk0 + ATTN_W + KV_W
    n_qk_heads = ATTN_HEADS + KV_HEADS
    qk_cols = qk0 + (jnp.arange(n_qk_heads)[:, None] * ATTN_DH + perm[None, :]).reshape(-1)
    w_cat = jnp.concatenate([wi[:, :g0], wi[:, qk_cols], wi[:, qk1:]], axis=1).astype(BF16)
    gate_rows = jnp.array(GATE_ROW_ORDER)
    w_gate = w_gate[:, gate_rows]
    gb = mlstm_gate_bias[l].astype(F32)[gate_rows]
    cos, sin = _rope_tables(T)
    cos_q, sin_q = _gained_tables(cos, sin, attn_q_norm_g[l], ATTN_DH ** -0.5 * LOG2_E)
    cos_k, sin_k = _gained_tables(cos, sin, attn_k_norm_g[l], 1.0)
    r2 = lambda a: a.reshape(1, -1).astype(F32)
    ffn_half = jnp.where(jnp.arange(2 * D_FF) >= D_FF, 0.5, 1.0).astype(F32)[None, :]
    return {
        "ln_in_g": r2(ln_in_g), "ln_in_b": r2(ln_in_b),
        "w_cat": w_cat, "w_gt": w_gate.T.astype(BF16),
        "gb_row": jnp.broadcast_to(gb[:, None], (N_GATES, LANES)),
        "mlstm_sel": _mlstm_select(),
        "rope_cos_q": cos_q, "rope_sin_q": sin_q, "rope_cos_k": cos_k, "rope_sin_k": sin_k,
        "mlstm_conv_w": mlstm_conv_w[l].astype(F32), "mlstm_conv_b": r2(mlstm_conv_b[l]),
        "mlstm_norm_g": r2(mlstm_norm_g[l]),
        "w_mem_kv": w_mem_kv[l].astype(BF16),
        "w_branch_mlstm": w_branch_mlstm[l].astype(BF16),
        "w_branch_attn": w_branch_attn[l].astype(BF16),
        "w_branch_mem": w_branch_mem[l].astype(BF16),
        "w_out": w_out[l].astype(BF16),
        "ln1_g": r2(ln1_g[l]), "ln1_b": r2(ln1_b[l]),
        "w_ffn_up": w_ffn_up[l].astype(BF16),
        "ffn_conv_w": ffn_conv_w[l].astype(F32) * ffn_half,
        "ffn_conv_b": r2(ffn_conv_b[l]) * ffn_half,
        "w_ffn_down": w_ffn_down[l].astype(BF16),
        "ln2_g": r2(ln2_g[l]), "ln2_b": r2(ln2_b[l]),
    }


def _tile(T, want):
    t = min(want, T)
    assert T % t == 0
    return t


def _trunk(x, mem, w):
    B, T, _ = x.shape
    M = mem.shape[1]
    assert T % MLSTM_CHUNK == 0 and T % GRID_W == 0
    x2 = x.reshape(B * T, D_MODEL)
    mem2 = mem.reshape(B * M, D_MODEL)
    qkm, vom, qa, ka, va, qc, gbr, gi, gf = _in_proj(x2, T, w, _tile(T, 512))
    hm = _mlstm(qkm, vom, gi, gf, T, w)
    ha = _attention(qa, ka, va, T, _tile(T, 1024))
    kvc = _mem_kv(mem2, M, w)
    x1 = _merge(x2, hm, ha, qc, kvc, gbr, T, M, w, _tile(T, 512))
    y = _ffn(x1, T, w, _tile(T, 512))
    return y.reshape(B, T, D_MODEL)


def kernel(x_prompt, x_sample, mem_prompt, mem_sample, ln_in_g, ln_in_b, w_in, mlstm_gate_bias, mlstm_conv_w, mlstm_conv_b, mlstm_norm_g, attn_q_norm_g, attn_k_norm_g, w_mem_kv, w_branch_mlstm, w_branch_attn, w_branch_mem, w_out, ln1_g, ln1_b, w_ffn_up, ffn_conv_w, ffn_conv_b, w_ffn_down, ln2_g, ln2_b):
    assert DEPTH == 1 and x_prompt.shape[1] == x_sample.shape[1]
    w = _prep_weights(0, x_prompt.shape[1], ln_in_g, ln_in_b, w_in, mlstm_gate_bias, mlstm_conv_w,
                      mlstm_conv_b, mlstm_norm_g, attn_q_norm_g, attn_k_norm_g, w_mem_kv,
                      w_branch_mlstm, w_branch_attn, w_branch_mem, w_out, ln1_g, ln1_b, w_ffn_up,
                      ffn_conv_w, ffn_conv_b, w_ffn_down, ln2_g, ln2_b)
    return (_trunk(x_prompt, mem_prompt, w), _trunk(x_sample, mem_sample, w))
```

```python
import functools

import jax
import jax.numpy as jnp
from jax import lax
from jax.experimental import pallas as pl
from jax.experimental.pallas import tpu as pltpu

F32 = jnp.float32
BF16 = jnp.bfloat16

D_MODEL = 1024
DEPTH = 1
GRID_W = 64
MLSTM_HEADS = 4
MLSTM_DH = 128
MLSTM_W = MLSTM_HEADS * MLSTM_DH
MLSTM_CHUNK = 128
N_GATES = 4 * MLSTM_HEADS
ATTN_DH = 128
ATTN_HEADS = 8
KV_HEADS = 2
ATTN_GROUP = ATTN_HEADS // KV_HEADS
ATTN_W = ATTN_HEADS * ATTN_DH
KV_W = KV_HEADS * ATTN_DH
ROPE_AXIS_DIM = ATTN_DH // 2
ROPE_THETA = 10000.0
MEM_HEADS = 4
MEM_DH = 128
MEM_W = MEM_HEADS * MEM_DH
N_BRANCH = 3
D_FF = ((8 * D_MODEL // 3 + 127) // 128) * 128
CONV_W = 3
DEEPNORM_ALPHA = (2.0 * DEPTH) ** 0.25
LN_EPS = 1e-5
LOG2_E = 1.4426950408889634

LANES = 128
SUBLANES = 8
VMEM_LIMIT_BYTES = 56 * 1024 * 1024

SEG_QKM = (0, 2 * MLSTM_W)
SEG_VOM = (SEG_QKM[1], SEG_QKM[1] + 2 * MLSTM_W)
SEG_QA = (SEG_VOM[1], SEG_VOM[1] + ATTN_W)
SEG_KA = (SEG_QA[1], SEG_QA[1] + KV_W)
SEG_VA = (SEG_KA[1], SEG_KA[1] + KV_W)
SEG_QC = (SEG_VA[1], SEG_VA[1] + MEM_W)
SEG_GBR = (SEG_QC[1], SEG_QC[1] + N_BRANCH * D_MODEL)
W_CAT = SEG_GBR[1]
GATE_ROW_ORDER = tuple(g * MLSTM_HEADS + h for g in (0, 2, 1, 3) for h in range(MLSTM_HEADS))

FFN_CHUNK = 256
ATTN_UNIT_ROWS = 128


def _params(n_axes, flags=None):
    return pltpu.CompilerParams(dimension_semantics=("arbitrary",) * n_axes,
                                vmem_limit_bytes=VMEM_LIMIT_BYTES, flags=flags)


def _const_spec(shape):
    nd = len(shape)
    return pl.BlockSpec(shape, lambda *_: (0,) * nd, pipeline_mode=pl.Buffered(1))


def _layer_norm(x, g, b):
    mu = jnp.mean(x, axis=-1, keepdims=True)
    xc = x - mu
    var = jnp.mean(xc * xc, axis=-1, keepdims=True)
    return xc * lax.rsqrt(var + LN_EPS) * g + b


def _sigmoid(x):
    return 1.0 / (1.0 + jnp.exp(-x))


def _log_sigmoid(x):
    return jnp.minimum(x, 0.0) - jnp.log(1.0 + jnp.exp(-jnp.abs(x)))


def _dot(a, b):
    return jnp.dot(a, b, preferred_element_type=F32)


def _dot_nt(a, b):
    return lax.dot_general(a, b, (((1,), (1,)), ((), ())), preferred_element_type=F32)


def _split3(x):
    x1 = x.astype(BF16)
    r1 = x - x1.astype(F32)
    x2 = r1.astype(BF16)
    r2 = r1 - x2.astype(F32)
    return x1, x2, r2.astype(BF16)


def _rms_rope(xh, cos_g, sin_g):
    ms = jnp.mean(xh * xh, axis=-1, keepdims=True)
    xh = xh * lax.rsqrt(ms + LN_EPS)
    return xh * cos_g + pltpu.roll(xh, LANES // 2, 1) * sin_g


def _proj_kernel(xp_ref, x_ref, xn_ref, lng_ref, lnb_ref, w_ref, wgt_ref, gbr_row_ref,
                 cq_ref, sq_ref, ck_ref, sk_ref, cw_ref, cb_ref,
                 qm_ref, ktm_ref, vom_ref, qa_ref, ka_ref, va_ref, qc_ref, gbr_ref, gi_ref, gf_ref,
                 *, tiles_per_seq):
    tm = x_ref.shape[0]
    lng = lng_ref[...]
    lnb = lnb_ref[...]
    xb = _layer_norm(x_ref[...], lng, lnb).astype(BF16)

    def proj(lo, hi):
        return _dot(xb, w_ref[:, lo:hi])

    j = pl.program_id(0) % tiles_per_seq
    halo = jnp.concatenate([xp_ref[...], xn_ref[...]], axis=0)
    halo_u = _dot(_layer_norm(halo, lng, lnb).astype(BF16), w_ref[:, SEG_QKM[0]:SEG_QKM[1]])
    hr = xp_ref.shape[0]
    prev_rows = halo_u[hr - 1:hr, :] * jnp.where(j > 0, 1.0, 0.0).astype(F32)
    next_rows = halo_u[hr:hr + 1, :] * jnp.where(j < tiles_per_seq - 1, 1.0, 0.0).astype(F32)
    sub = lax.broadcasted_iota(jnp.int32, (tm, MLSTM_W), 0)

    def conv_silu(lo, scale):
        cols = slice(lo, lo + MLSTM_W)
        u = proj(SEG_QKM[0] + lo, SEG_QKM[0] + lo + MLSTM_W)
        u_prev = jnp.where(sub == 0, prev_rows[:, cols], pltpu.roll(u, 1, 0))
        u_next = jnp.where(sub == tm - 1, next_rows[:, cols], pltpu.roll(u, tm - 1, 0))
        y = u_prev * cw_ref[0:1, cols] + cb_ref[:, cols]
        y = y + u * cw_ref[1:2, cols]
        y = y + u_next * cw_ref[2:3, cols]
        y = y * _sigmoid(y)
        return y if scale == 1.0 else y * scale

    def emit_q():
        qm_ref[...] = conv_silu(0, 1.0).astype(BF16)

    def emit_k():
        yk = conv_silu(MLSTM_W, MLSTM_DH ** -0.5)
        for a in range(tm // MLSTM_CHUNK):
            for h in range(MLSTM_HEADS):
                blk = yk[a * MLSTM_CHUNK:(a + 1) * MLSTM_CHUNK, h * MLSTM_DH:(h + 1) * MLSTM_DH]
                r0 = a * MLSTM_W + h * MLSTM_DH
                ktm_ref[r0:r0 + MLSTM_DH, :] = blk.T.astype(BF16)

    def plain_job(out_ref, dst, src, piece):
        def job():
            out_ref[:, dst:dst + piece] = proj(src, src + piece).astype(out_ref.dtype)
        return job

    def rope_job(out_ref, dst, src, c_ref, s_ref):
        def job():
            blk = proj(src, src + pair)
            for jj in range(2):
                xh = _rms_rope(blk[:, jj * ATTN_DH:(jj + 1) * ATTN_DH], c_ref[...], s_ref[...])
                out_ref[:, dst + jj * ATTN_DH:dst + (jj + 1) * ATTN_DH] = xh.astype(BF16)
        return job

    pair = 2 * ATTN_DH
    vector_jobs = [emit_q, emit_k]
    vector_jobs += [rope_job(qa_ref, c - SEG_QA[0], c, cq_ref, sq_ref) for c in range(SEG_QA[0], SEG_QA[1], pair)]
    vector_jobs += [rope_job(ka_ref, c - SEG_KA[0], c, ck_ref, sk_ref) for c in range(SEG_KA[0], SEG_KA[1], pair)]
    plain_jobs = []
    for out_ref, seg, piece in ((vom_ref, SEG_VOM, 512), (va_ref, SEG_VA, KV_W),
                                (qc_ref, SEG_QC, 512), (gbr_ref, SEG_GBR, 512)):
        plain_jobs += [plain_job(out_ref, c - seg[0], c, piece) for c in range(seg[0], seg[1], piece)]

    while vector_jobs or plain_jobs:
        if plain_jobs:
            plain_jobs.pop(0)()
        if vector_jobs:
            vector_jobs.pop(0)()

    grow = _dot_nt(wgt_ref[...], xb)
    nd = N_GATES // 2
    for j in range(tm // LANES):
        blk = grow[:, j * LANES:(j + 1) * LANES] + gbr_row_ref[...]
        gi_ref[j * nd:(j + 1) * nd, :] = blk[:nd]
        gf_ref[j * nd:(j + 1) * nd, :] = blk[nd:]


def _in_proj(x2, T, w, tm):
    n = x2.shape[0]
    grid = (n // tm,)
    tiles_per_seq = T // tm
    row = lambda i: (i, 0)
    hb = 2 * SUBLANES
    nhb = n // hb
    out_shapes = (
        jax.ShapeDtypeStruct((n, MLSTM_W), BF16),
        jax.ShapeDtypeStruct((n // MLSTM_CHUNK * MLSTM_W, MLSTM_CHUNK), BF16),
        jax.ShapeDtypeStruct((n, 2 * MLSTM_W), BF16),
        jax.ShapeDtypeStruct((n, ATTN_W), BF16),
        jax.ShapeDtypeStruct((n, KV_W), BF16),
        jax.ShapeDtypeStruct((n, KV_W), BF16),
        jax.ShapeDtypeStruct((n, MEM_W), BF16),
        jax.ShapeDtypeStruct((n, N_BRANCH * D_MODEL), BF16),
        jax.ShapeDtypeStruct((n // LANES * (N_GATES // 2), LANES), F32),
        jax.ShapeDtypeStruct((n // LANES * (N_GATES // 2), LANES), F32),
    )
    out_specs = (
        pl.BlockSpec((tm, MLSTM_W), row),
        pl.BlockSpec((tm // MLSTM_CHUNK * MLSTM_W, MLSTM_CHUNK), row),
        pl.BlockSpec((tm, 2 * MLSTM_W), row),
        pl.BlockSpec((tm, ATTN_W), row),
        pl.BlockSpec((tm, KV_W), row),
        pl.BlockSpec((tm, KV_W), row),
        pl.BlockSpec((tm, MEM_W), row),
        pl.BlockSpec((tm, N_BRANCH * D_MODEL), row),
        pl.BlockSpec((tm // LANES * (N_GATES // 2), LANES), row),
        pl.BlockSpec((tm // LANES * (N_GATES // 2), LANES), row),
    )
    in_specs = [
        pl.BlockSpec((hb, D_MODEL), lambda i: (jnp.maximum(i * (tm // hb) - 1, 0), 0)),
        pl.BlockSpec((tm, D_MODEL), row),
        pl.BlockSpec((hb, D_MODEL), lambda i: (jnp.minimum((i + 1) * (tm // hb), nhb - 1), 0)),
        _const_spec((1, D_MODEL)), _const_spec((1, D_MODEL)),
        _const_spec((D_MODEL, W_CAT)), _const_spec((N_GATES, D_MODEL)),
        _const_spec((N_GATES, LANES)),
    ] + [pl.BlockSpec((tm, LANES), lambda i: (i % tiles_per_seq, 0))] * 4 + [
        _const_spec((CONV_W, 2 * MLSTM_W)), _const_spec((1, 2 * MLSTM_W)),
    ]
    return pl.pallas_call(
        functools.partial(_proj_kernel, tiles_per_seq=tiles_per_seq),
        grid=grid, in_specs=in_specs, out_specs=out_specs, out_shape=out_shapes,
        compiler_params=_params(1), name="in_proj",
    )(x2, x2, x2, w["ln_in_g"], w["ln_in_b"], w["w_cat"], w["w_gt"], w["gb_row"],
      w["rope_cos_q"], w["rope_sin_q"], w["rope_cos_k"], w["rope_sin_k"],
      w["mlstm_conv_w"], w["mlstm_conv_b"])


N_DIRHEADS = 2 * MLSTM_HEADS
STAT_GROUPS = 7
STAT_BLOCK = (0, 0, 0, 1, 2, 2, 2)


def _mlstm_select():
    lane = jnp.arange(LANES)
    grp, j = lane // N_DIRHEADS, lane % N_DIRHEADS
    blk = jnp.array(STAT_BLOCK + (-1,) * (LANES // N_DIRHEADS - STAT_GROUPS))[grp]
    out_blk = jnp.arange(3 * LANES) // LANES
    sel = (j[None, :, None] == jnp.arange(N_DIRHEADS)[:, None, None]) & (blk[None, :, None] == out_blk[None, None, :])
    return sel.astype(BF16)


def _mlstm_kernel(q_ref, kt_ref, vom_ref, gi_ref, gf_ref, sel_ref, ng_ref, hm_ref,
                  ktw_s, lhs_s, e_s, stat_s, gt_s, ma_s, mpf_s, mpb_s, r_s, wa_s, sp_s,
                  hf_s, st_s):
    T = q_ref.shape[0]
    L = MLSTM_CHUNK
    NC = T // L
    DH = MLSTM_DH
    H = MLSTM_HEADS

    ri = lax.broadcasted_iota(jnp.int32, (L, L), 0)
    ci = lax.broadcasted_iota(jnp.int32, (L, L), 1)
    causal = ci <= ri
    anti = ci >= ri
    tril = jnp.where(causal, 1.0, 0.0).astype(BF16)
    triu = jnp.where(anti, 1.0, 0.0).astype(BF16)
    ones_ll = jnp.ones((L, L), BF16)

    R = NC * N_DIRHEADS
    row_r = lax.broadcasted_iota(jnp.int32, (R, L), 0)
    lane_r = lax.broadcasted_iota(jnp.int32, (R, L), 1)
    is_fwd = (row_r % N_DIRHEADS) < H
    is_fwd8 = lax.broadcasted_iota(jnp.int32, (N_DIRHEADS, L), 0) < H

    li = gi_ref[...]
    y1, y2, y3 = _split3(_log_sigmoid(gf_ref[...]))

    def lane_sums(m):
        return _dot(y1, m) + _dot(y2, m) + _dot(y3, m)

    bl = jnp.where(is_fwd, lane_sums(triu), lane_sums(tril))
    gt = lane_sums(ones_ll)
    r = li - bl
    pm = r
    sm = r
    for sh in (1, 2, 4, 8, 16, 32, 64):
        pm = jnp.maximum(pm, jnp.where(lane_r >= sh, pltpu.roll(pm, sh, 1), -jnp.inf))
        sm = jnp.maximum(sm, jnp.where(lane_r < L - sh, pltpu.roll(sm, L - sh, 1), -jnp.inf))
    cm = jnp.where(is_fwd, pm, sm)
    a = gt - bl + li
    ma = jnp.broadcast_to(jnp.max(a, axis=-1, keepdims=True), (R, L))
    gt_s[...] = gt
    ma_s[...] = ma

    def chunk_rows(c):
        return pl.ds(pl.multiple_of(c * N_DIRHEADS, N_DIRHEADS), N_DIRHEADS)

    def stabiliser_scan(i, m):
        rf = chunk_rows(i)
        rb = chunk_rows(NC - 1 - i)
        mpf_s[rf, :] = m
        mpb_s[rb, :] = m
        gt8 = jnp.where(is_fwd8, gt_s[rf, :], gt_s[rb, :])
        ma8 = jnp.where(is_fwd8, ma_s[rf, :], ma_s[rb, :])
        return jnp.maximum(gt8 + m, ma8)

    lax.fori_loop(0, NC, stabiliser_scan, jnp.zeros((N_DIRHEADS, L), F32))

    m_prev = jnp.where(is_fwd, mpf_s[...], mpb_s[...])
    m_new = jnp.maximum(gt + m_prev, ma)
    sp_s[...] = jnp.exp(gt + m_prev - m_new)
    wa_s[...] = jnp.exp(a - m_new)
    r_s[...] = r
    u = -jnp.maximum(m_prev, cm)
    stat_vals = list(_split3(u)) + [jnp.exp(m_prev + u)] + list(_split3(jnp.exp(u - bl)))
    for k, t in enumerate(stat_vals):
        stat_s[k] = t.astype(F32)
    stat_pad = jnp.zeros((L - STAT_GROUPS * N_DIRHEADS, L), F32)

    def prologue(c, carry):
        r0 = pl.multiple_of(c * L, L)
        rows = pl.ds(r0, L)
        rows8 = chunk_rows(c)
        stats = [stat_s[k, rows8, :] for k in range(STAT_GROUPS)] + [stat_pad]
        stats_t = jnp.concatenate(stats, axis=0).T.astype(BF16)
        wa = wa_s[rows8, :]
        r8 = r_s[rows8, :]
        kt_rows = pl.multiple_of(c * MLSTM_W, MLSTM_W)
        for h in range(H):
            hs = slice(h * DH, (h + 1) * DH)
            q = q_ref[rows, hs]
            kt_b = kt_ref[pl.ds(kt_rows + h * DH, DH), :]
            s = _dot(q, kt_b)
            kt = kt_b.astype(F32)
            for d in range(2):
                j = H * d + h
                bc = _dot(stats_t, sel_ref[j])
                p = jnp.where(causal if d == 0 else anti, jnp.exp(bc[:, :L] + r8[j:j + 1, :]), 0.0)
                lhs_s[d, rows, 2 * h * DH:(2 * h + 1) * DH] = (s * p).astype(BF16)
                lhs_s[d, rows, (2 * h + 1) * DH:(2 * h + 2) * DH] = bc[:, L:2 * L].astype(BF16) * q
                e_s[j, rows, :] = bc[:, 2 * L:]
                ktw_s[d, c, hs, :] = (kt * wa[j:j + 1, :]).astype(BF16)
        return carry

    lax.fori_loop(0, NC, prologue, 0)

    ones_blk = jnp.ones((L, DH), BF16)
    ng = ng_ref[...]

    def run_direction(d):
        st_s[...] = jnp.zeros_like(st_s)

        def body(i, carry):
            c = i if d == 0 else NC - 1 - i
            r0 = pl.multiple_of(c * L, L)
            rows = pl.ds(r0, L)
            for h in range(H):
                j = H * d + h
                hs = slice(h * DH, (h + 1) * DH)
                lhs = lhs_s[d, rows, 2 * h * DH:(2 * h + 2) * DH]
                vaug = jnp.concatenate([vom_ref[rows, hs], ones_blk], axis=1)
                c_prev = st_s[h]
                tot = _dot(lhs, jnp.concatenate([vaug, c_prev.astype(BF16)], axis=0))
                h_out = tot[:, :DH] / jnp.maximum(jnp.abs(tot[:, DH:]), e_s[j, rows, :])
                sp = sp_s[pl.ds(c * N_DIRHEADS + j, 1), :]
                st_s[h] = jnp.concatenate([sp, sp], axis=1) * c_prev + _dot(ktw_s[d, c, hs, :], vaug)

                if d == 0:
                    hf_s[rows, hs] = h_out
                else:
                    hh = hf_s[rows, hs] + h_out
                    mu = jnp.mean(hh, axis=-1, keepdims=True)
                    hc = hh - mu
                    var = jnp.mean(hc * hc, axis=-1, keepdims=True)
                    hn = hc * lax.rsqrt(var + LN_EPS) * ng[:, hs]
                    o = vom_ref[rows, MLSTM_W + h * DH:MLSTM_W + (h + 1) * DH].astype(F32)
                    hm_ref[rows, hs] = (hn * _sigmoid(o)).astype(BF16)
            return carry

        lax.fori_loop(0, NC, body, 0, unroll=4)

    run_direction(0)
    run_direction(1)


def _mlstm(qm, ktm, vom, gi, gf, T, w):
    n = qm.shape[0]
    B = n // T
    NC = T // MLSTM_CHUNK
    seq = lambda b: (b, 0)
    gate_major = pltpu.VMEM((NC * N_DIRHEADS, MLSTM_CHUNK), F32)
    return pl.pallas_call(
        _mlstm_kernel, grid=(B,),
        in_specs=[
            pl.BlockSpec((T, MLSTM_W), seq),
            pl.BlockSpec((NC * MLSTM_W, MLSTM_CHUNK), seq),
            pl.BlockSpec((T, 2 * MLSTM_W), seq),
            pl.BlockSpec((NC * N_DIRHEADS, MLSTM_CHUNK), seq),
            pl.BlockSpec((NC * N_DIRHEADS, MLSTM_CHUNK), seq),
            _const_spec((N_DIRHEADS, LANES, 3 * MLSTM_CHUNK)),
            _const_spec((1, MLSTM_W)),
        ],
        out_specs=pl.BlockSpec((T, MLSTM_W), seq),
        out_shape=jax.ShapeDtypeStruct((n, MLSTM_W), BF16),
        scratch_shapes=[
            pltpu.VMEM((2, NC, MLSTM_W, MLSTM_CHUNK), BF16),
            pltpu.VMEM((2, T, 2 * MLSTM_W), BF16),
            pltpu.VMEM((N_DIRHEADS, T, LANES), F32),
            pltpu.VMEM((STAT_GROUPS, NC * N_DIRHEADS, MLSTM_CHUNK), F32),
            gate_major, gate_major, gate_major, gate_major,
            gate_major, gate_major, gate_major,
            pltpu.VMEM((T, MLSTM_W), F32),
            pltpu.VMEM((MLSTM_HEADS, MLSTM_DH, 2 * MLSTM_DH), F32),
        ],
        compiler_params=_params(1), name="mlstm",
    )(qm, ktm, vom, gi, gf, w["mlstm_sel"], w["mlstm_norm_g"])


def _attn_kernel(q_ref, k_ref, v_ref, o_ref, vaug_s):
    @pl.when(pl.program_id(2) == 0)
    def _():
        vaug_s[:, :ATTN_DH] = v_ref[...]
        vaug_s[:, ATTN_DH:] = jnp.ones((v_ref.shape[0], ATTN_DH), BF16)

    k = k_ref[...]
    tq = q_ref.shape[0]
    rows_per_unit = min(tq, ATTN_UNIT_ROWS)
    for g in range(ATTN_GROUP):
        hs = slice(g * ATTN_DH, (g + 1) * ATTN_DH)
        for r0 in range(0, tq, rows_per_unit):
            rs = slice(r0, r0 + rows_per_unit)
            s = _dot_nt(q_ref[rs, hs], k)
            m = jnp.max(s, axis=-1, keepdims=True)
            p = jnp.exp2(s - m)
            oa = _dot(p.astype(BF16), vaug_s[...])
            o_ref[rs, hs] = (oa[:, :ATTN_DH] / oa[:, ATTN_DH:ATTN_DH + 1]).astype(o_ref.dtype)


def _attention(qa, ka, va, T, tq):
    n = qa.shape[0]
    B = n // T
    nq = T // tq
    gw = ATTN_GROUP * ATTN_DH
    return pl.pallas_call(
        _attn_kernel, grid=(B, KV_HEADS, nq),
        in_specs=[
            pl.BlockSpec((tq, gw), lambda b, h, i: (b * nq + i, h)),
            pl.BlockSpec((T, ATTN_DH), lambda b, h, i: (b, h)),
            pl.BlockSpec((T, ATTN_DH), lambda b, h, i: (b, h)),
        ],
        out_specs=pl.BlockSpec((tq, gw), lambda b, h, i: (b * nq + i, h)),
        out_shape=jax.ShapeDtypeStruct((n, ATTN_W), BF16),
        scratch_shapes=[pltpu.VMEM((T, 2 * ATTN_DH), BF16)],
        compiler_params=_params(3), name="gqa_attn",
    )(qa, ka, va)


def _mem_kv_kernel(m_ref, w_ref, o_ref):
    o_ref[...] = _dot(m_ref[...].astype(BF16), w_ref[...]).astype(o_ref.dtype)


def _mem_kv(mem2, M, w):
    n = mem2.shape[0]
    return pl.pallas_call(
        _mem_kv_kernel, grid=(n // M,),
        in_specs=[pl.BlockSpec((M, D_MODEL), lambda b: (b, 0)),
                  _const_spec((D_MODEL, 2 * MEM_W))],
        out_specs=pl.BlockSpec((M, 2 * MEM_W), lambda b: (b, 0)),
        out_shape=jax.ShapeDtypeStruct((n, 2 * MEM_W), BF16),
        compiler_params=_params(1), name="mem_kv",
    )(mem2, w["w_mem_kv"])


def _merge_kernel(x_ref, hm_ref, ha_ref, qc_ref, kv_ref, gbr_ref, lng_ref, lnb_ref,
                  wbm_ref, wba_ref, wbc_ref, wo_ref, l1g_ref, l1b_ref, x1_ref):
    hc_parts = []
    for h in range(MEM_HEADS):
        hs = slice(h * MEM_DH, (h + 1) * MEM_DH)
        kc = kv_ref[:, hs]
        vc = kv_ref[:, MEM_W + h * MEM_DH:MEM_W + (h + 1) * MEM_DH]
        s = _dot_nt(qc_ref[:, hs], kc) * (MEM_DH ** -0.5)
        m = jnp.max(s, axis=-1, keepdims=True)
        p = jnp.exp(s - m)
        l = jnp.sum(p, axis=-1, keepdims=True)
        hc_parts.append((_dot(p.astype(BF16), vc) / l).astype(BF16))
    hc = jnp.concatenate(hc_parts, axis=1)

    def gate(j):
        return _sigmoid(gbr_ref[:, j * D_MODEL:(j + 1) * D_MODEL].astype(F32))

    merged = gate(0) * _dot(hm_ref[...], wbm_ref[...])
    merged = merged + gate(1) * _dot(ha_ref[...], wba_ref[...])
    merged = merged + gate(2) * _dot(hc, wbc_ref[...])
    mixed = _dot(merged.astype(BF16), wo_ref[...])
    xn = _layer_norm(x_ref[...], lng_ref[...], lnb_ref[...])
    x1_ref[...] = _layer_norm(DEEPNORM_ALPHA * xn + mixed, l1g_ref[...], l1b_ref[...])


def _merge(x2, hm, ha, qc, kvc, gbr, T, M, w, tm):
    n = x2.shape[0]
    tiles_per_seq = T // tm
    row = lambda i: (i, 0)
    return pl.pallas_call(
        _merge_kernel, grid=(n // tm,),
        in_specs=[
            pl.BlockSpec((tm, D_MODEL), row),
            pl.BlockSpec((tm, MLSTM_W), row),
            pl.BlockSpec((tm, ATTN_W), row),
            pl.BlockSpec((tm, MEM_W), row),
            pl.BlockSpec((M, 2 * MEM_W), lambda i: (i // tiles_per_seq, 0)),
            pl.BlockSpec((tm, N_BRANCH * D_MODEL), row),
            _const_spec((1, D_MODEL)), _const_spec((1, D_MODEL)),
            _const_spec((MLSTM_W, D_MODEL)), _const_spec((ATTN_W, D_MODEL)),
            _const_spec((MEM_W, D_MODEL)), _const_spec((D_MODEL, D_MODEL)),
            _const_spec((1, D_MODEL)), _const_spec((1, D_MODEL)),
        ],
        out_specs=pl.BlockSpec((tm, D_MODEL), row),
        out_shape=jax.ShapeDtypeStruct((n, D_MODEL), F32),
        compiler_params=_params(1), name="merge_out",
    )(x2, hm, ha, qc, kvc, gbr, w["ln_in_g"], w["ln_in_b"], w["w_branch_mlstm"],
      w["w_branch_attn"], w["w_branch_mem"], w["w_out"], w["ln1_g"], w["ln1_b"])


GELU_C = 0.7978845608028654
GELU_C3 = GELU_C * 0.044715


def _gelu_tanh_x2(x):
    return x * (1.0 + jnp.tanh(x * (GELU_C + GELU_C3 * (x * x))))


def _ffn_kernel(xp_ref, x_ref, xn_ref, wup_ref, cw_ref, cb_ref, wdn_ref, l2g_ref, l2b_ref,
                y_ref, xe_s, act_s, *, tiles_per_seq):
    tm = x_ref.shape[0]
    i = pl.program_id(0)
    j = i % tiles_per_seq
    prev = xp_ref[...] * jnp.where(j > 0, 1.0, 0.0).astype(F32)
    nxt = xn_ref[...] * jnp.where(j < tiles_per_seq - 1, 1.0, 0.0).astype(F32)
    xe_s[...] = jnp.concatenate([prev, x_ref[...], nxt], axis=0).astype(BF16)
    te = tm + 2 * SUBLANES

    def conv(u, cw, cb):
        up = pltpu.roll(u, 1, 0)[SUBLANES:SUBLANES + tm, :]
        uc = u[SUBLANES:SUBLANES + tm, :]
        un = pltpu.roll(u, te - 1, 0)[SUBLANES:SUBLANES + tm, :]
        y = up * cw[0:1, :] + cb
        y = y + uc * cw[1:2, :]
        return y + un * cw[2:3, :]

    def up_conv(lo):
        cols = slice(lo, lo + FFN_CHUNK)
        return conv(_dot(xe_s[...], wup_ref[:, cols]), cw_ref[:, cols], cb_ref[:, cols])

    for c in range(0, D_FF, FFN_CHUNK):
        act_s[:, c:c + FFN_CHUNK] = (_gelu_tanh_x2(up_conv(c)) * up_conv(D_FF + c)).astype(BF16)

    ff = _dot(act_s[...], wdn_ref[...])
    y_ref[...] = _layer_norm(DEEPNORM_ALPHA * x_ref[...] + ff, l2g_ref[...], l2b_ref[...])


def _ffn(x1, T, w, tm):
    n = x1.shape[0]
    tiles_per_seq = T // tm
    r8 = tm // SUBLANES
    nblk8 = n // SUBLANES
    row = lambda i: (i, 0)
    return pl.pallas_call(
        functools.partial(_ffn_kernel, tiles_per_seq=tiles_per_seq), grid=(n // tm,),
        in_specs=[
            pl.BlockSpec((SUBLANES, D_MODEL), lambda i: (jnp.maximum(i * r8 - 1, 0), 0)),
            pl.BlockSpec((tm, D_MODEL), row),
            pl.BlockSpec((SUBLANES, D_MODEL), lambda i: (jnp.minimum((i + 1) * r8, nblk8 - 1), 0)),
            _const_spec((D_MODEL, 2 * D_FF)),
            _const_spec((CONV_W, 2 * D_FF)),
            _const_spec((1, 2 * D_FF)),
            _const_spec((D_FF, D_MODEL)),
            _const_spec((1, D_MODEL)), _const_spec((1, D_MODEL)),
        ],
        out_specs=pl.BlockSpec((tm, D_MODEL), row),
        out_shape=jax.ShapeDtypeStruct((n, D_MODEL), F32),
        scratch_shapes=[pltpu.VMEM((tm + 2 * SUBLANES, D_MODEL), BF16),
                        pltpu.VMEM((tm, D_FF), BF16)],
        compiler_params=_params(1), name="conv_ffn",
    )(x1, x1, x1, w["w_ffn_up"], w["ffn_conv_w"], w["ffn_conv_b"], w["w_ffn_down"],
      w["ln2_g"], w["ln2_b"])


def _rope_tables(T):
    rows = T // GRID_W
    row = jnp.repeat(jnp.arange(rows, dtype=F32), GRID_W)
    col = jnp.tile(jnp.arange(GRID_W, dtype=F32), rows)
    inv_freq = ROPE_THETA ** (-jnp.arange(0, ROPE_AXIS_DIM, 2, dtype=F32) / ROPE_AXIS_DIM)
    ang_r = row[:, None] * inv_freq
    ang_c = col[:, None] * inv_freq
    cos = jnp.concatenate([jnp.cos(ang_r), jnp.cos(ang_c)] * 2, axis=-1)
    sin = jnp.concatenate([-jnp.sin(ang_r), -jnp.sin(ang_c), jnp.sin(ang_r), jnp.sin(ang_c)], axis=-1)
    return cos, sin


def _rope_head_perm():
    p = jnp.arange(ATTN_DH)
    quarter = ROPE_AXIS_DIM // 2
    half, axis, j = p // ROPE_AXIS_DIM, (p % ROPE_AXIS_DIM) // quarter, p % quarter
    return axis * ROPE_AXIS_DIM + half * quarter + j


def _gained_tables(cos, sin, g, scale):
    g = g.astype(F32)[_rope_head_perm()]
    return cos * (g * scale), sin * (jnp.roll(g, ATTN_DH // 2) * scale)


def _prep_weights(l, T, ln_in_g, ln_in_b, w_in, mlstm_gate_bias, mlstm_conv_w, mlstm_conv_b,
                  mlstm_norm_g, attn_q_norm_g, attn_k_norm_g, w_mem_kv, w_branch_mlstm,
                  w_branch_attn, w_branch_mem, w_out, ln1_g, ln1_b, w_ffn_up, ffn_conv_w,
                  ffn_conv_b, w_ffn_down, ln2_g, ln2_b):
    wi = w_in[l]
    g0 = 4 * MLSTM_W
    g1 = g0 + N_GATES
    w_gate = wi[:, g0:g1]
    perm = _rope_head_perm()
    qk0 = g1
    qk1 = qk0 + ATTN_W + KV_W
    n_qk_heads = ATTN_HEADS + KV_HEADS
    qk_cols = qk0 + (jnp.arange(n_qk_heads)[:, None] * ATTN_DH + perm[None, :]).reshape(-1)
    w_cat = jnp.concatenate([wi[:, :g0], wi[:, qk_cols], wi[:, qk1:]], axis=1).astype(BF16)
    gate_rows = jnp.array(GATE_ROW_ORDER)
    w_gate = w_gate[:, gate_rows]
    gb = mlstm_gate_bias[l].astype(F32)[gate_rows]
    cos, sin = _rope_tables(T)
    cos_q, sin_q = _gained_tables(cos, sin, attn_q_norm_g[l], ATTN_DH ** -0.5 * LOG2_E)
    cos_k, sin_k = _gained_tables(cos, sin, attn_k_norm_g[l], 1.0)
    r2 = lambda a: a.reshape(1, -1).astype(F32)
    ffn_half = jnp.where(jnp.arange(2 * D_FF) >= D_FF, 0.5, 1.0).astype(F32)[None, :]
    return {
        "ln_in_g": r2(ln_in_g), "ln_in_b": r2(ln_in_b),
        "w_cat": w_cat, "w_gt": w_gate.T.astype(BF16),
        "gb_row": jnp.broadcast_to(gb[:, None], (N_GATES, LANES)),
        "mlstm_sel": _mlstm_select(),
        "rope_cos_q": cos_q, "rope_sin_q": sin_q, "rope_cos_k": cos_k, "rope_sin_k": sin_k,
        "mlstm_conv_w": mlstm_conv_w[l].astype(F32), "mlstm_conv_b": r2(mlstm_conv_b[l]),
        "mlstm_norm_g": r2(mlstm_norm_g[l]),
        "w_mem_kv": w_mem_kv[l].astype(BF16),
        "w_branch_mlstm": w_branch_mlstm[l].astype(BF16),
        "w_branch_attn": w_branch_attn[l].astype(BF16),
        "w_branch_mem": w_branch_mem[l].astype(BF16),
        "w_out": w_out[l].astype(BF16),
        "ln1_g": r2(ln1_g[l]), "ln1_b": r2(ln1_b[l]),
        "w_ffn_up": w_ffn_up[l].astype(BF16),
        "ffn_conv_w": ffn_conv_w[l].astype(F32) * ffn_half,
        "ffn_conv_b": r2(ffn_conv_b[l]) * ffn_half,
        "w_ffn_down": w_ffn_down[l].astype(BF16),
        "ln2_g": r2(ln2_g[l]), "ln2_b": r2(ln2_b[l]),
    }


def _tile(T, want):
    t = min(want, T)
    assert T % t == 0
    return t


def _trunk(x, mem, w):
    B, T, _ = x.shape
    M = mem.shape[1]
    assert T % MLSTM_CHUNK == 0 and T % GRID_W == 0
    x2 = x.reshape(B * T, D_MODEL)
    mem2 = mem.reshape(B * M, D_MODEL)
    qm, ktm, vom, qa, ka, va, qc, gbr, gi, gf = _in_proj(x2, T, w, _tile(T, 512))
    hm = _mlstm(qm, ktm, vom, gi, gf, T, w)
    ha = _attention(qa, ka, va, T, _tile(T, 1024))
    kvc = _mem_kv(mem2, M, w)
    x1 = _merge(x2, hm, ha, qc, kvc, gbr, T, M, w, _tile(T, 512))
    y = _ffn(x1, T, w, _tile(T, 512))
    return y.reshape(B, T, D_MODEL)


def kernel(x_prompt, x_sample, mem_prompt, mem_sample, ln_in_g, ln_in_b, w_in, mlstm_gate_bias, mlstm_conv_w, mlstm_conv_b, mlstm_norm_g, attn_q_norm_g, attn_k_norm_g, w_mem_kv, w_branch_mlstm, w_branch_attn, w_branch_mem, w_out, ln1_g, ln1_b, w_ffn_up, ffn_conv_w, ffn_conv_b, w_ffn_down, ln2_g, ln2_b):
    assert DEPTH == 1 and x_prompt.shape[1] == x_sample.shape[1]
    w = _prep_weights(0, x_prompt.shape[1], ln_in_g, ln_in_b, w_in, mlstm_gate_bias, mlstm_conv_w,
                      mlstm_conv_b, mlstm_norm_g, attn_q_norm_g, attn_k_norm_g, w_mem_kv,
                      w_branch_mlstm, w_branch_attn, w_branch_mem, w_out, ln1_g, ln1_b, w_ffn_up,
                      ffn_conv_w, ffn_conv_b, w_ffn_down, ln2_g, ln2_b)
    return (_trunk(x_prompt, mem_prompt, w), _trunk(x_sample, mem_sample, w))
```

```python
import functools

import jax
import jax.numpy as jnp
from jax import lax
from jax.experimental import pallas as pl
from jax.experimental.pallas import tpu as pltpu

F32 = jnp.float32
BF16 = jnp.bfloat16

D_MODEL = 1024
DEPTH = 1
GRID_W = 64
MLSTM_HEADS = 4
MLSTM_DH = 128
MLSTM_W = MLSTM_HEADS * MLSTM_DH
MLSTM_CHUNK = 128
N_GATES = 4 * MLSTM_HEADS
ATTN_DH = 128
ATTN_HEADS = 8
KV_HEADS = 2
ATTN_GROUP = ATTN_HEADS // KV_HEADS
ATTN_W = ATTN_HEADS * ATTN_DH
KV_W = KV_HEADS * ATTN_DH
ROPE_AXIS_DIM = ATTN_DH // 2
ROPE_THETA = 10000.0
MEM_HEADS = 4
MEM_DH = 128
MEM_W = MEM_HEADS * MEM_DH
N_BRANCH = 3
D_FF = ((8 * D_MODEL // 3 + 127) // 128) * 128
CONV_W = 3
DEEPNORM_ALPHA = (2.0 * DEPTH) ** 0.25
LN_EPS = 1e-5
LOG2_E = 1.4426950408889634

LANES = 128
SUBLANES = 8
VMEM_LIMIT_BYTES = 56 * 1024 * 1024

SEG_QKM = (0, 2 * MLSTM_W)
SEG_VOM = (SEG_QKM[1], SEG_QKM[1] + 2 * MLSTM_W)
SEG_QA = (SEG_VOM[1], SEG_VOM[1] + ATTN_W)
SEG_KA = (SEG_QA[1], SEG_QA[1] + KV_W)
SEG_VA = (SEG_KA[1], SEG_KA[1] + KV_W)
SEG_QC = (SEG_VA[1], SEG_VA[1] + MEM_W)
SEG_GBR = (SEG_QC[1], SEG_QC[1] + N_BRANCH * D_MODEL)
W_CAT = SEG_GBR[1]
GATE_ROW_ORDER = tuple(g * MLSTM_HEADS + h for g in (0, 2, 1, 3) for h in range(MLSTM_HEADS))

FFN_CHUNK = 256
ROW_UNIT = 512
ATTN_UNIT_ROWS = 128


def _params(n_axes):
    return pltpu.CompilerParams(dimension_semantics=("arbitrary",) * n_axes,
                                vmem_limit_bytes=VMEM_LIMIT_BYTES)


def _const_spec(shape):
    nd = len(shape)
    return pl.BlockSpec(shape, lambda *_: (0,) * nd, pipeline_mode=pl.Buffered(1))


def _layer_norm(x, g, b):
    mu = jnp.mean(x, axis=-1, keepdims=True)
    xc = x - mu
    var = jnp.mean(xc * xc, axis=-1, keepdims=True)
    return xc * lax.rsqrt(var + LN_EPS) * g + b


def _sigmoid(x):
    return 1.0 / (1.0 + jnp.exp(-x))


def _log_sigmoid(x):
    return jnp.minimum(x, 0.0) - jnp.log(1.0 + jnp.exp(-jnp.abs(x)))


def _dot(a, b):
    return jnp.dot(a, b, preferred_element_type=F32)


def _dot_nt(a, b):
    return lax.dot_general(a, b, (((1,), (1,)), ((), ())), preferred_element_type=F32)


def _split3(x):
    x1 = x.astype(BF16)
    r1 = x - x1.astype(F32)
    x2 = r1.astype(BF16)
    r2 = r1 - x2.astype(F32)
    return x1, x2, r2.astype(BF16)


def _rms_rope(xh, cos_g, sin_g):
    ms = jnp.mean(xh * xh, axis=-1, keepdims=True)
    xh = xh * lax.rsqrt(ms + LN_EPS)
    return xh * cos_g + pltpu.roll(xh, LANES // 2, 1) * sin_g


def _proj_kernel(xp_ref, x_ref, xn_ref, lng_ref, lnb_ref, w_ref, wgt_ref, gbr_row_ref,
                 cq_ref, sq_ref, ck_ref, sk_ref, cw_ref, cb_ref,
                 xno_ref, qm_ref, ktm_ref, vom_ref, qa_ref, ka_ref, va_ref, qc_ref, gbr_ref, gi_ref, gf_ref,
                 *, tiles_per_seq):
    tm = x_ref.shape[0]
    lng = lng_ref[...]
    lnb = lnb_ref[...]
    xn = _layer_norm(x_ref[...], lng, lnb)
    xno_ref[...] = xn
    xb = xn.astype(BF16)

    def proj(lo, hi):
        return _dot(xb, w_ref[:, lo:hi])

    j = pl.program_id(0) % tiles_per_seq
    halo = jnp.concatenate([xp_ref[...], xn_ref[...]], axis=0)
    halo_u = _dot(_layer_norm(halo, lng, lnb).astype(BF16), w_ref[:, SEG_QKM[0]:SEG_QKM[1]])
    hr = xp_ref.shape[0]
    prev_rows = halo_u[hr - 1:hr, :] * jnp.where(j > 0, 1.0, 0.0).astype(F32)
    next_rows = halo_u[hr:hr + 1, :] * jnp.where(j < tiles_per_seq - 1, 1.0, 0.0).astype(F32)
    cw_cols = MLSTM_W
    sub = lax.broadcasted_iota(jnp.int32, (tm, cw_cols), 0)

    def conv_silu(lo, scale):
        cols = slice(lo, lo + cw_cols)
        u = proj(SEG_QKM[0] + lo, SEG_QKM[0] + lo + cw_cols)
        u_prev = jnp.where(sub == 0, prev_rows[:, cols], pltpu.roll(u, 1, 0))
        u_next = jnp.where(sub == tm - 1, next_rows[:, cols], pltpu.roll(u, tm - 1, 0))
        y = u_prev * cw_ref[0:1, cols] + cb_ref[:, cols]
        y = y + u * cw_ref[1:2, cols]
        y = y + u_next * cw_ref[2:3, cols]
        y = y * _sigmoid(y)
        return y if scale == 1.0 else y * scale

    def q_job(lo):
        def job():
            qm_ref[:, lo:lo + cw_cols] = conv_silu(lo, 1.0).astype(BF16)
        return job

    def k_job(lo):
        def job():
            yk = conv_silu(MLSTM_W + lo, MLSTM_DH ** -0.5)
            for a in range(tm // MLSTM_CHUNK):
                for c0 in range(0, cw_cols, MLSTM_DH):
                    blk = yk[a * MLSTM_CHUNK:(a + 1) * MLSTM_CHUNK, c0:c0 + MLSTM_DH]
                    r0 = a * MLSTM_W + lo + c0
                    ktm_ref[r0:r0 + MLSTM_DH, :] = blk.T.astype(BF16)
        return job

    def plain_job(out_ref, dst, src, piece):
        def job():
            out_ref[:, dst:dst + piece] = proj(src, src + piece).astype(out_ref.dtype)
        return job

    def rope_job(out_ref, dst, src, c_ref, s_ref):
        def job():
            blk = proj(src, src + pair)
            for jj in range(2):
                xh = _rms_rope(blk[:, jj * ATTN_DH:(jj + 1) * ATTN_DH], c_ref[...], s_ref[...])
                out_ref[:, dst + jj * ATTN_DH:dst + (jj + 1) * ATTN_DH] = xh.astype(BF16)
        return job

    pair = 2 * ATTN_DH
    vector_jobs = [mk(lo) for mk in (q_job, k_job) for lo in range(0, MLSTM_W, cw_cols)]
    vector_jobs += [rope_job(qa_ref, c - SEG_QA[0], c, cq_ref, sq_ref) for c in range(SEG_QA[0], SEG_QA[1], pair)]
    vector_jobs += [rope_job(ka_ref, c - SEG_KA[0], c, ck_ref, sk_ref) for c in range(SEG_KA[0], SEG_KA[1], pair)]
    plain_jobs = []
    for out_ref, seg, piece in ((vom_ref, SEG_VOM, 512), (va_ref, SEG_VA, KV_W),
                                (qc_ref, SEG_QC, 512), (gbr_ref, SEG_GBR, 512)):
        plain_jobs += [plain_job(out_ref, c - seg[0], c, piece) for c in range(seg[0], seg[1], piece)]

    while vector_jobs or plain_jobs:
        if plain_jobs:
            plain_jobs.pop(0)()
        if vector_jobs:
            vector_jobs.pop(0)()

    grow = _dot_nt(wgt_ref[...], xb)
    nd = N_GATES // 2
    for j in range(tm // LANES):
        blk = grow[:, j * LANES:(j + 1) * LANES] + gbr_row_ref[...]
        gi_ref[j * nd:(j + 1) * nd, :] = blk[:nd]
        gf_ref[j * nd:(j + 1) * nd, :] = blk[nd:]


def _in_proj(x2, T, w, tm):
    n = x2.shape[0]
    grid = (n // tm,)
    tiles_per_seq = T // tm
    row = lambda i: (i, 0)
    hb = 2 * SUBLANES
    nhb = n // hb
    out_shapes = (
        jax.ShapeDtypeStruct((n, D_MODEL), F32),
        jax.ShapeDtypeStruct((n, MLSTM_W), BF16),
        jax.ShapeDtypeStruct((n // MLSTM_CHUNK * MLSTM_W, MLSTM_CHUNK), BF16),
        jax.ShapeDtypeStruct((n, 2 * MLSTM_W), BF16),
        jax.ShapeDtypeStruct((n, ATTN_W), BF16),
        jax.ShapeDtypeStruct((n, KV_W), BF16),
        jax.ShapeDtypeStruct((n, KV_W), BF16),
        jax.ShapeDtypeStruct((n, MEM_W), BF16),
        jax.ShapeDtypeStruct((n, N_BRANCH * D_MODEL), BF16),
        jax.ShapeDtypeStruct((n // LANES * (N_GATES // 2), LANES), F32),
        jax.ShapeDtypeStruct((n // LANES * (N_GATES // 2), LANES), F32),
    )
    out_specs = (
        pl.BlockSpec((tm, D_MODEL), row),
        pl.BlockSpec((tm, MLSTM_W), row),
        pl.BlockSpec((tm // MLSTM_CHUNK * MLSTM_W, MLSTM_CHUNK), row),
        pl.BlockSpec((tm, 2 * MLSTM_W), row),
        pl.BlockSpec((tm, ATTN_W), row),
        pl.BlockSpec((tm, KV_W), row),
        pl.BlockSpec((tm, KV_W), row),
        pl.BlockSpec((tm, MEM_W), row),
        pl.BlockSpec((tm, N_BRANCH * D_MODEL), row),
        pl.BlockSpec((tm // LANES * (N_GATES // 2), LANES), row),
        pl.BlockSpec((tm // LANES * (N_GATES // 2), LANES), row),
    )
    in_specs = [
        pl.BlockSpec((hb, D_MODEL), lambda i: (jnp.maximum(i * (tm // hb) - 1, 0), 0)),
        pl.BlockSpec((tm, D_MODEL), row),
        pl.BlockSpec((hb, D_MODEL), lambda i: (jnp.minimum((i + 1) * (tm // hb), nhb - 1), 0)),
        _const_spec((1, D_MODEL)), _const_spec((1, D_MODEL)),
        _const_spec((D_MODEL, W_CAT)), _const_spec((N_GATES, D_MODEL)),
        _const_spec((N_GATES, LANES)),
    ] + [pl.BlockSpec((tm, LANES), lambda i: (i % tiles_per_seq, 0))] * 4 + [
        _const_spec((CONV_W, 2 * MLSTM_W)), _const_spec((1, 2 * MLSTM_W)),
    ]
    return pl.pallas_call(
        functools.partial(_proj_kernel, tiles_per_seq=tiles_per_seq),
        grid=grid, in_specs=in_specs, out_specs=out_specs, out_shape=out_shapes,
        compiler_params=_params(1), name="in_proj",
    )(x2, x2, x2, w["ln_in_g"], w["ln_in_b"], w["w_cat"], w["w_gt"], w["gb_row"],
      w["rope_cos_q"], w["rope_sin_q"], w["rope_cos_k"], w["rope_sin_k"],
      w["mlstm_conv_w"], w["mlstm_conv_b"])


N_DIRHEADS = 2 * MLSTM_HEADS
STAT_GROUPS = 6
STAT_BLOCK = (0, 0, 0, 1, 1, 1)
STAT_BLOCKS = 2


def _mlstm_select():
    lane = jnp.arange(LANES)
    grp, j = lane // N_DIRHEADS, lane % N_DIRHEADS
    blk = jnp.array(STAT_BLOCK + (-1,) * (LANES // N_DIRHEADS - STAT_GROUPS))[grp]
    out_blk = jnp.arange(STAT_BLOCKS * LANES) // LANES
    sel = (j[None, :, None] == jnp.arange(N_DIRHEADS)[:, None, None]) & (blk[None, :, None] == out_blk[None, None, :])
    return sel.astype(BF16)


def _mlstm_kernel(q_ref, kt_ref, vom_ref, gi_ref, gf_ref, sel_ref, ng_ref, hm_ref,
                  ktw_s, lhs_s, e_s, stat_s, gt_s, ma_s, mpf_s, mpb_s, r_s, wa_s, sp_s,
                  hf_s, st_s):
    T = q_ref.shape[0]
    L = MLSTM_CHUNK
    NC = T // L
    DH = MLSTM_DH
    H = MLSTM_HEADS

    ri = lax.broadcasted_iota(jnp.int32, (L, L), 0)
    ci = lax.broadcasted_iota(jnp.int32, (L, L), 1)
    causal = ci <= ri
    anti = ci >= ri
    tril = jnp.where(causal, 1.0, 0.0).astype(BF16)
    triu = jnp.where(anti, 1.0, 0.0).astype(BF16)
    ones_ll = jnp.ones((L, L), BF16)

    R = NC * N_DIRHEADS
    row_r = lax.broadcasted_iota(jnp.int32, (R, L), 0)
    lane_r = lax.broadcasted_iota(jnp.int32, (R, L), 1)
    is_fwd = (row_r % N_DIRHEADS) < H
    is_fwd8 = lax.broadcasted_iota(jnp.int32, (N_DIRHEADS, L), 0) < H

    li = gi_ref[...]
    y1, y2, y3 = _split3(_log_sigmoid(gf_ref[...]))

    def lane_sums(m):
        return _dot(y1, m) + _dot(y2, m) + _dot(y3, m)

    bl = jnp.where(is_fwd, lane_sums(triu), lane_sums(tril))
    gt = lane_sums(ones_ll)
    r = li - bl
    pm = r
    sm = r
    for sh in (1, 2, 4, 8, 16, 32, 64):
        pm = jnp.maximum(pm, jnp.where(lane_r >= sh, pltpu.roll(pm, sh, 1), -jnp.inf))
        sm = jnp.maximum(sm, jnp.where(lane_r < L - sh, pltpu.roll(sm, L - sh, 1), -jnp.inf))
    cm = jnp.where(is_fwd, pm, sm)
    a = gt - bl + li
    ma = jnp.broadcast_to(jnp.max(a, axis=-1, keepdims=True), (R, L))
    gt_s[...] = gt
    ma_s[...] = ma

    def chunk_rows(c):
        return pl.ds(pl.multiple_of(c * N_DIRHEADS, N_DIRHEADS), N_DIRHEADS)

    def stabiliser_scan(i, m):
        rf = chunk_rows(i)
        rb = chunk_rows(NC - 1 - i)
        mpf_s[rf, :] = m
        mpb_s[rb, :] = m
        gt8 = jnp.where(is_fwd8, gt_s[rf, :], gt_s[rb, :])
        ma8 = jnp.where(is_fwd8, ma_s[rf, :], ma_s[rb, :])
        return jnp.maximum(gt8 + m, ma8)

    lax.fori_loop(0, NC, stabiliser_scan, jnp.zeros((N_DIRHEADS, L), F32))

    m_prev = jnp.where(is_fwd, mpf_s[...], mpb_s[...])
    m_new = jnp.maximum(gt + m_prev, ma)
    sp_s[...] = jnp.exp(gt + m_prev - m_new)
    wa_s[...] = jnp.exp(a - m_new)
    r_s[...] = r
    ma_s[...] = m_prev
    u = -jnp.maximum(m_prev, cm)
    stat_vals = list(_split3(u)) + list(_split3(jnp.exp(u - bl)))
    for k, t in enumerate(stat_vals):
        stat_s[k] = t.astype(F32)
    stat_pad = jnp.zeros((L - STAT_GROUPS * N_DIRHEADS, L), F32)

    def prologue(c, carry):
        r0 = pl.multiple_of(c * L, L)
        rows = pl.ds(r0, L)
        rows8 = chunk_rows(c)
        stats = [stat_s[k, rows8, :] for k in range(STAT_GROUPS)] + [stat_pad]
        stats_t = jnp.concatenate(stats, axis=0).T.astype(BF16)
        wa = wa_s[rows8, :]
        r8 = r_s[rows8, :]
        mp8 = ma_s[rows8, :]
        kt_rows = pl.multiple_of(c * MLSTM_W, MLSTM_W)
        for h in range(H):
            hs = slice(h * DH, (h + 1) * DH)
            q = q_ref[rows, hs]
            kt_b = kt_ref[pl.ds(kt_rows + h * DH, DH), :]
            s = _dot(q, kt_b)
            kt = kt_b.astype(F32)
            for d in range(2):
                j = H * d + h
                bc = _dot(stats_t, sel_ref[j])
                u_b = bc[:, :L]
                p = jnp.where(causal if d == 0 else anti, jnp.exp(u_b + r8[j:j + 1, :]), 0.0)
                w_inter = jnp.exp(u_b + mp8[j:j + 1, :])
                lhs_s[d, rows, 2 * h * DH:(2 * h + 1) * DH] = (s * p).astype(BF16)
                lhs_s[d, rows, (2 * h + 1) * DH:(2 * h + 2) * DH] = w_inter.astype(BF16) * q
                e_s[j, rows, :] = bc[:, L:]
                ktw_s[d, c, hs, :] = (kt * wa[j:j + 1, :]).astype(BF16)
        return carry

    lax.fori_loop(0, NC, prologue, 0, unroll=4)

    ones_blk = jnp.ones((L, DH), BF16)
    ng = ng_ref[...]

    def run_direction(d):
        st_s[...] = jnp.zeros_like(st_s)

        def body(i, carry):
            c = i if d == 0 else NC - 1 - i
            r0 = pl.multiple_of(c * L, L)
            rows = pl.ds(r0, L)
            for h in range(H):
                j = H * d + h
                hs = slice(h * DH, (h + 1) * DH)
                lhs = lhs_s[d, rows, 2 * h * DH:(2 * h + 2) * DH]
                vaug = jnp.concatenate([vom_ref[rows, hs], ones_blk], axis=1)
                c_prev = st_s[h]
                tot = _dot(lhs, jnp.concatenate([vaug, c_prev.astype(BF16)], axis=0))
                h_out = tot[:, :DH] / jnp.maximum(jnp.abs(tot[:, DH:]), e_s[j, rows, :])
                sp = sp_s[pl.ds(c * N_DIRHEADS + j, 1), :]
                st_s[h] = jnp.concatenate([sp, sp], axis=1) * c_prev + _dot(ktw_s[d, c, hs, :], vaug)

                if d == 0:
                    hf_s[rows, hs] = h_out
                else:
                    hh = hf_s[rows, hs] + h_out
                    mu = jnp.mean(hh, axis=-1, keepdims=True)
                    hc = hh - mu
                    var = jnp.mean(hc * hc, axis=-1, keepdims=True)
                    hn = hc * lax.rsqrt(var + LN_EPS) * ng[:, hs]
                    o = vom_ref[rows, MLSTM_W + h * DH:MLSTM_W + (h + 1) * DH].astype(F32)
                    hm_ref[rows, hs] = (hn * _sigmoid(o)).astype(BF16)
            return carry

        lax.fori_loop(0, NC, body, 0, unroll=4)

    run_direction(0)
    run_direction(1)


def _mlstm(qm, ktm, vom, gi, gf, T, w):
    n = qm.shape[0]
    B = n // T
    NC = T // MLSTM_CHUNK
    seq = lambda b: (b, 0)
    gate_major = pltpu.VMEM((NC * N_DIRHEADS, MLSTM_CHUNK), F32)
    return pl.pallas_call(
        _mlstm_kernel, grid=(B,),
        in_specs=[
            pl.BlockSpec((T, MLSTM_W), seq),
            pl.BlockSpec((NC * MLSTM_W, MLSTM_CHUNK), seq),
            pl.BlockSpec((T, 2 * MLSTM_W), seq),
            pl.BlockSpec((NC * N_DIRHEADS, MLSTM_CHUNK), seq),
            pl.BlockSpec((NC * N_DIRHEADS, MLSTM_CHUNK), seq),
            _const_spec((N_DIRHEADS, LANES, STAT_BLOCKS * MLSTM_CHUNK)),
            _const_spec((1, MLSTM_W)),
        ],
        out_specs=pl.BlockSpec((T, MLSTM_W), seq),
        out_shape=jax.ShapeDtypeStruct((n, MLSTM_W), BF16),
        scratch_shapes=[
            pltpu.VMEM((2, NC, MLSTM_W, MLSTM_CHUNK), BF16),
            pltpu.VMEM((2, T, 2 * MLSTM_W), BF16),
            pltpu.VMEM((N_DIRHEADS, T, LANES), F32),
            pltpu.VMEM((STAT_GROUPS, NC * N_DIRHEADS, MLSTM_CHUNK), F32),
            gate_major, gate_major, gate_major, gate_major,
            gate_major, gate_major, gate_major,
            pltpu.VMEM((T, MLSTM_W), F32),
            pltpu.VMEM((MLSTM_HEADS, MLSTM_DH, 2 * MLSTM_DH), F32),
        ],
        compiler_params=_params(1), name="mlstm",
    )(qm, ktm, vom, gi, gf, w["mlstm_sel"], w["mlstm_norm_g"])


def _attn_kernel(q_ref, k_ref, v_ref, o_ref, vaug_s):
    @pl.when(pl.program_id(2) == 0)
    def _():
        vaug_s[:, :ATTN_DH] = v_ref[...]
        vaug_s[:, ATTN_DH:] = jnp.ones((v_ref.shape[0], ATTN_DH), BF16)

    k = k_ref[...]
    tq = q_ref.shape[0]
    rows_per_unit = min(tq, ATTN_UNIT_ROWS)
    for g in range(ATTN_GROUP):
        hs = slice(g * ATTN_DH, (g + 1) * ATTN_DH)
        for r0 in range(0, tq, rows_per_unit):
            rs = slice(r0, r0 + rows_per_unit)
            s = _dot_nt(q_ref[rs, hs], k)
            m = jnp.max(s, axis=-1, keepdims=True)
            p = jnp.exp2(s - m)
            oa = _dot(p.astype(BF16), vaug_s[...])
            o_ref[rs, hs] = (oa[:, :ATTN_DH] / oa[:, ATTN_DH:ATTN_DH + 1]).astype(o_ref.dtype)


def _attention(qa, ka, va, T, tq):
    n = qa.shape[0]
    B = n // T
    nq = T // tq
    gw = ATTN_GROUP * ATTN_DH
    return pl.pallas_call(
        _attn_kernel, grid=(B, KV_HEADS, nq),
        in_specs=[
            pl.BlockSpec((tq, gw), lambda b, h, i: (b * nq + i, h)),
            pl.BlockSpec((T, ATTN_DH), lambda b, h, i: (b, h)),
            pl.BlockSpec((T, ATTN_DH), lambda b, h, i: (b, h)),
        ],
        out_specs=pl.BlockSpec((tq, gw), lambda b, h, i: (b * nq + i, h)),
        out_shape=jax.ShapeDtypeStruct((n, ATTN_W), BF16),
        scratch_shapes=[pltpu.VMEM((T, 2 * ATTN_DH), BF16)],
        compiler_params=_params(3), name="gqa_attn",
    )(qa, ka, va)


def _mem_kv_kernel(m_ref, w_ref, o_ref):
    o_ref[...] = _dot(m_ref[...].astype(BF16), w_ref[...]).astype(o_ref.dtype)


def _mem_kv(mem2, M, w):
    n = mem2.shape[0]
    return pl.pallas_call(
        _mem_kv_kernel, grid=(n // M,),
        in_specs=[pl.BlockSpec((M, D_MODEL), lambda b: (b, 0)),
                  _const_spec((D_MODEL, 2 * MEM_W))],
        out_specs=pl.BlockSpec((M, 2 * MEM_W), lambda b: (b, 0)),
        out_shape=jax.ShapeDtypeStruct((n, 2 * MEM_W), BF16),
        compiler_params=_params(1), name="mem_kv",
    )(mem2, w["w_mem_kv"])


def _merge_kernel(xn_ref, hm_ref, ha_ref, qc_ref, kv_ref, gbr_ref,
                  wbm_ref, wba_ref, wbc_ref, wo_ref, l1g_ref, l1b_ref, x1_ref):
    ones_blk = jnp.ones((kv_ref.shape[0], MEM_DH), BF16)
    tm = xn_ref.shape[0]
    unit = min(tm, ROW_UNIT)
    for r0 in range(0, tm, unit):
        rs = slice(r0, r0 + unit)
        hc_parts = []
        for h in range(MEM_HEADS):
            hs = slice(h * MEM_DH, (h + 1) * MEM_DH)
            kc = kv_ref[:, hs]
            vaug = jnp.concatenate([kv_ref[:, MEM_W + h * MEM_DH:MEM_W + (h + 1) * MEM_DH], ones_blk], axis=1)
            s = _dot_nt(qc_ref[rs, hs], kc)
            p = jnp.exp2(s - jnp.max(s, axis=-1, keepdims=True))
            oa = _dot(p.astype(BF16), vaug)
            hc_parts.append((oa[:, :MEM_DH] / oa[:, MEM_DH:MEM_DH + 1]).astype(BF16))
        hc = jnp.concatenate(hc_parts, axis=1)

        def gate(j):
            return _sigmoid(gbr_ref[rs, j * D_MODEL:(j + 1) * D_MODEL].astype(F32))

        merged = gate(0) * _dot(hm_ref[rs, :], wbm_ref[...])
        merged = merged + gate(1) * _dot(ha_ref[rs, :], wba_ref[...])
        merged = merged + gate(2) * _dot(hc, wbc_ref[...])
        mixed = _dot(merged.astype(BF16), wo_ref[...])
        x1_ref[rs, :] = _layer_norm(DEEPNORM_ALPHA * xn_ref[rs, :] + mixed, l1g_ref[...], l1b_ref[...])


def _merge(xn, hm, ha, qc, kvc, gbr, T, M, w, tm):
    n = xn.shape[0]
    tiles_per_seq = T // tm
    row = lambda i: (i, 0)
    return pl.pallas_call(
        _merge_kernel, grid=(n // tm,),
        in_specs=[
            pl.BlockSpec((tm, D_MODEL), row),
            pl.BlockSpec((tm, MLSTM_W), row),
            pl.BlockSpec((tm, ATTN_W), row),
            pl.BlockSpec((tm, MEM_W), row),
            pl.BlockSpec((M, 2 * MEM_W), lambda i: (i // tiles_per_seq, 0)),
            pl.BlockSpec((tm, N_BRANCH * D_MODEL), row),
            _const_spec((MLSTM_W, D_MODEL)), _const_spec((ATTN_W, D_MODEL)),
            _const_spec((MEM_W, D_MODEL)), _const_spec((D_MODEL, D_MODEL)),
            _const_spec((1, D_MODEL)), _const_spec((1, D_MODEL)),
        ],
        out_specs=pl.BlockSpec((tm, D_MODEL), row),
        out_shape=jax.ShapeDtypeStruct((n, D_MODEL), F32),
        compiler_params=_params(1), name="merge_out",
    )(xn, hm, ha, qc, kvc, gbr, w["w_branch_mlstm"],
      w["w_branch_attn"], w["w_branch_mem"], w["w_out"], w["ln1_g"], w["ln1_b"])


GELU_C = 0.7978845608028654
GELU_C3 = GELU_C * 0.044715


def _gelu_tanh_x2(x):
    return x * (1.0 + jnp.tanh(x * (GELU_C + GELU_C3 * (x * x))))


def _ffn_kernel(xp_ref, x_ref, xn_ref, wup_ref, cw_ref, cb_ref, wdn_ref, l2g_ref, l2b_ref,
                y_ref, xe_s, act_s, *, tiles_per_seq):
    tm = x_ref.shape[0]
    i = pl.program_id(0)
    j = i % tiles_per_seq
    prev = xp_ref[...] * jnp.where(j > 0, 1.0, 0.0).astype(F32)
    nxt = xn_ref[...] * jnp.where(j < tiles_per_seq - 1, 1.0, 0.0).astype(F32)
    xe_s[...] = jnp.concatenate([prev, x_ref[...], nxt], axis=0).astype(BF16)
    te = tm + 2 * SUBLANES

    def conv(u, cw, cb):
        up = pltpu.roll(u, 1, 0)[SUBLANES:SUBLANES + tm, :]
        uc = u[SUBLANES:SUBLANES + tm, :]
        un = pltpu.roll(u, te - 1, 0)[SUBLANES:SUBLANES + tm, :]
        y = up * cw[0:1, :] + cb
        y = y + uc * cw[1:2, :]
        return y + un * cw[2:3, :]

    def up_conv(lo):
        cols = slice(lo, lo + FFN_CHUNK)
        return conv(_dot(xe_s[...], wup_ref[:, cols]), cw_ref[:, cols], cb_ref[:, cols])

    for c in range(0, D_FF, FFN_CHUNK):
        act_s[:, c:c + FFN_CHUNK] = (_gelu_tanh_x2(up_conv(c)) * up_conv(D_FF + c)).astype(BF16)

    ff = _dot(act_s[...], wdn_ref[...])
    y_ref[...] = _layer_norm(DEEPNORM_ALPHA * x_ref[...] + ff, l2g_ref[...], l2b_ref[...])


def _ffn(x1, T, w, tm):
    n = x1.shape[0]
    tiles_per_seq = T // tm
    r8 = tm // SUBLANES
    nblk8 = n // SUBLANES
    row = lambda i: (i, 0)
    return pl.pallas_call(
        functools.partial(_ffn_kernel, tiles_per_seq=tiles_per_seq), grid=(n // tm,),
        in_specs=[
            pl.BlockSpec((SUBLANES, D_MODEL), lambda i: (jnp.maximum(i * r8 - 1, 0), 0)),
            pl.BlockSpec((tm, D_MODEL), row),
            pl.BlockSpec((SUBLANES, D_MODEL), lambda i: (jnp.minimum((i + 1) * r8, nblk8 - 1), 0)),
            _const_spec((D_MODEL, 2 * D_FF)),
            _const_spec((CONV_W, 2 * D_FF)),
            _const_spec((1, 2 * D_FF)),
            _const_spec((D_FF, D_MODEL)),
            _const_spec((1, D_MODEL)), _const_spec((1, D_MODEL)),
        ],
        out_specs=pl.BlockSpec((tm, D_MODEL), row),
        out_shape=jax.ShapeDtypeStruct((n, D_MODEL), F32),
        scratch_shapes=[pltpu.VMEM((tm + 2 * SUBLANES, D_MODEL), BF16),
                        pltpu.VMEM((tm, D_FF), BF16)],
        compiler_params=_params(1), name="conv_ffn",
    )(x1, x1, x1, w["w_ffn_up"], w["ffn_conv_w"], w["ffn_conv_b"], w["w_ffn_down"],
      w["ln2_g"], w["ln2_b"])


def _rope_tables(T):
    rows = T // GRID_W
    row = jnp.repeat(jnp.arange(rows, dtype=F32), GRID_W)
    col = jnp.tile(jnp.arange(GRID_W, dtype=F32), rows)
    inv_freq = ROPE_THETA ** (-jnp.arange(0, ROPE_AXIS_DIM, 2, dtype=F32) / ROPE_AXIS_DIM)
    ang_r = row[:, None] * inv_freq
    ang_c = col[:, None] * inv_freq
    cos = jnp.concatenate([jnp.cos(ang_r), jnp.cos(ang_c)] * 2, axis=-1)
    sin = jnp.concatenate([-jnp.sin(ang_r), -jnp.sin(ang_c), jnp.sin(ang_r), jnp.sin(ang_c)], axis=-1)
    return cos, sin


def _rope_head_perm():
    p = jnp.arange(ATTN_DH)
    quarter = ROPE_AXIS_DIM // 2
    half, axis, j = p // ROPE_AXIS_DIM, (p % ROPE_AXIS_DIM) // quarter, p % quarter
    return axis * ROPE_AXIS_DIM + half * quarter + j


def _gained_tables(cos, sin, g, scale):
    g = g.astype(F32)[_rope_head_perm()]
    return cos * (g * scale), sin * (jnp.roll(g, ATTN_DH // 2) * scale)


def _prep_weights(l, T, ln_in_g, ln_in_b, w_in, mlstm_gate_bias, mlstm_conv_w, mlstm_conv_b,
                  mlstm_norm_g, attn_q_norm_g, attn_k_norm_g, w_mem_kv, w_branch_mlstm,
                  w_branch_attn, w_branch_mem, w_out, ln1_g, ln1_b, w_ffn_up, ffn_conv_w,
                  ffn_conv_b, w_ffn_down, ln2_g, ln2_b):
    wi = w_in[l]
    g0 = 4 * MLSTM_W
    g1 = g0 + N_GATES
    w_gate = wi[:, g0:g1]
    qk0 = g1
    qk1 = qk0 + ATTN_W + KV_W
    qc0 = qk1 + KV_W
    qc1 = qc0 + MEM_W
    quarter = ROPE_AXIS_DIM // 2
    w_qk = wi[:, qk0:qk1].reshape(D_MODEL, ATTN_HEADS + KV_HEADS, 2, 2, quarter)
    w_qk = jnp.swapaxes(w_qk, 2, 3).reshape(D_MODEL, ATTN_W + KV_W)
    w_qc = wi[:, qc0:qc1] * (MEM_DH ** -0.5 * LOG2_E)
    w_cat = jnp.concatenate([wi[:, :g0], w_qk, wi[:, qk1:qc0], w_qc, wi[:, qc1:]], axis=1).astype(BF16)
    gate_rows = jnp.array(GATE_ROW_ORDER)
    w_gate = w_gate[:, gate_rows]
    gb = mlstm_gate_bias[l].astype(F32)[gate_rows]
    cos, sin = _rope_tables(T)
    cos_q, sin_q = _gained_tables(cos, sin, attn_q_norm_g[l], ATTN_DH ** -0.5 * LOG2_E)
    cos_k, sin_k = _gained_tables(cos, sin, attn_k_norm_g[l], 1.0)
    r2 = lambda a: a.reshape(1, -1).astype(F32)
    ffn_half = jnp.where(jnp.arange(2 * D_FF) >= D_FF, 0.5, 1.0).astype(F32)[None, :]
    return {
        "ln_in_g": r2(ln_in_g), "ln_in_b": r2(ln_in_b),
        "w_cat": w_cat, "w_gt": w_gate.T.astype(BF16),
        "gb_row": jnp.broadcast_to(gb[:, None], (N_GATES, LANES)),
        "mlstm_sel": _mlstm_select(),
        "rope_cos_q": cos_q, "rope_sin_q": sin_q, "rope_cos_k": cos_k, "rope_sin_k": sin_k,
        "mlstm_conv_w": mlstm_conv_w[l].astype(F32), "mlstm_conv_b": r2(mlstm_conv_b[l]),
        "mlstm_norm_g": r2(mlstm_norm_g[l]),
        "w_mem_kv": w_mem_kv[l].astype(BF16),
        "w_branch_mlstm": w_branch_mlstm[l].astype(BF16),
        "w_branch_attn": w_branch_attn[l].astype(BF16),
        "w_branch_mem": w_branch_mem[l].astype(BF16),
        "w_out": w_out[l].astype(BF16),
        "ln1_g": r2(ln1_g[l]), "ln1_b": r2(ln1_b[l]),
        "w_ffn_up": w_ffn_up[l].astype(BF16),
        "ffn_conv_w": ffn_conv_w[l].astype(F32) * ffn_half,
        "ffn_conv_b": r2(ffn_conv_b[l]) * ffn_half,
        "w_ffn_down": w_ffn_down[l].astype(BF16),
        "ln2_g": r2(ln2_g[l]), "ln2_b": r2(ln2_b[l]),
    }


def _tile(T, want):
    t = min(want, T)
    assert T % t == 0
    return t


def _trunk(x, mem, w):
    B, T, _ = x.shape
    M = mem.shape[1]
    assert T % MLSTM_CHUNK == 0 and T % GRID_W == 0
    x2 = x.reshape(B * T, D_MODEL)
    mem2 = mem.reshape(B * M, D_MODEL)
    xn, qm, ktm, vom, qa, ka, va, qc, gbr, gi, gf = _in_proj(x2, T, w, _tile(T, 512))
    hm = _mlstm(qm, ktm, vom, gi, gf, T, w)
    ha = _attention(qa, ka, va, T, _tile(T, 1024))
    kvc = _mem_kv(mem2, M, w)
    x1 = _merge(xn, hm, ha, qc, kvc, gbr, T, M, w, _tile(T, 512))
    y = _ffn(x1, T, w, _tile(T, 512))
    return y.reshape(B, T, D_MODEL)


def kernel(x_prompt, x_sample, mem_prompt, mem_sample, ln_in_g, ln_in_b, w_in, mlstm_gate_bias, mlstm_conv_w, mlstm_conv_b, mlstm_norm_g, attn_q_norm_g, attn_k_norm_g, w_mem_kv, w_branch_mlstm, w_branch_attn, w_branch_mem, w_out, ln1_g, ln1_b, w_ffn_up, ffn_conv_w, ffn_conv_b, w_ffn_down, ln2_g, ln2_b):
    assert DEPTH == 1 and x_prompt.shape[1] == x_sample.shape[1]
    w = _prep_weights(0, x_prompt.shape[1], ln_in_g, ln_in_b, w_in, mlstm_gate_bias, mlstm_conv_w,
                      mlstm_conv_b, mlstm_norm_g, attn_q_norm_g, attn_k_norm_g, w_mem_kv,
                      w_branch_mlstm, w_branch_attn, w_branch_mem, w_out, ln1_g, ln1_b, w_ffn_up,
                      ffn_conv_w, ffn_conv_b, w_ffn_down, ln2_g, ln2_b)
    return (_trunk(x_prompt, mem_prompt, w), _trunk(x_sample, mem_sample, w))
```

```python
import functools

import jax
import jax.numpy as jnp
from jax import lax
from jax.experimental import pallas as pl
from jax.experimental.pallas import tpu as pltpu

F32 = jnp.float32
BF16 = jnp.bfloat16

D_MODEL = 1024
DEPTH = 1
GRID_W = 64
MLSTM_HEADS = 4
MLSTM_DH = 128
MLSTM_W = MLSTM_HEADS * MLSTM_DH
MLSTM_CHUNK = 128
N_GATES = 4 * MLSTM_HEADS
ATTN_DH = 128
ATTN_HEADS = 8
KV_HEADS = 2
ATTN_GROUP = ATTN_HEADS // KV_HEADS
ATTN_W = ATTN_HEADS * ATTN_DH
KV_W = KV_HEADS * ATTN_DH
ROPE_AXIS_DIM = ATTN_DH // 2
ROPE_THETA = 10000.0
MEM_HEADS = 4
MEM_DH = 128
MEM_W = MEM_HEADS * MEM_DH
N_BRANCH = 3
D_FF = ((8 * D_MODEL // 3 + 127) // 128) * 128
CONV_W = 3
DEEPNORM_ALPHA = (2.0 * DEPTH) ** 0.25
LN_EPS = 1e-5
LOG2_E = 1.4426950408889634

LANES = 128
SUBLANES = 8
VMEM_LIMIT_BYTES = 56 * 1024 * 1024

SEG_QKM = (0, 2 * MLSTM_W)
SEG_VOM = (SEG_QKM[1], SEG_QKM[1] + 2 * MLSTM_W)
SEG_QA = (SEG_VOM[1], SEG_VOM[1] + ATTN_W)
SEG_KA = (SEG_QA[1], SEG_QA[1] + KV_W)
SEG_VA = (SEG_KA[1], SEG_KA[1] + KV_W)
SEG_QC = (SEG_VA[1], SEG_VA[1] + MEM_W)
SEG_GBR = (SEG_QC[1], SEG_QC[1] + N_BRANCH * D_MODEL)
W_CAT = SEG_GBR[1]
GATE_ROW_ORDER = tuple(g * MLSTM_HEADS + h for g in (0, 2, 1, 3) for h in range(MLSTM_HEADS))

FFN_CHUNK = 256
ROW_UNIT = 512
ATTN_UNIT_ROWS = 128


def _params(n_axes):
    return pltpu.CompilerParams(dimension_semantics=("arbitrary",) * n_axes,
                                vmem_limit_bytes=VMEM_LIMIT_BYTES)


def _const_spec(shape):
    nd = len(shape)
    return pl.BlockSpec(shape, lambda *_: (0,) * nd, pipeline_mode=pl.Buffered(1))


def _layer_norm(x, g, b):
    mu = jnp.mean(x, axis=-1, keepdims=True)
    xc = x - mu
    var = jnp.mean(xc * xc, axis=-1, keepdims=True)
    return xc * lax.rsqrt(var + LN_EPS) * g + b


def _sigmoid(x):
    return 1.0 / (1.0 + jnp.exp(-x))


def _log_sigmoid(x):
    return jnp.minimum(x, 0.0) - jnp.log(1.0 + jnp.exp(-jnp.abs(x)))


def _dot(a, b):
    return jnp.dot(a, b, preferred_element_type=F32)


def _dot_nt(a, b):
    return lax.dot_general(a, b, (((1,), (1,)), ((), ())), preferred_element_type=F32)


def _split3(x):
    x1 = x.astype(BF16)
    r1 = x - x1.astype(F32)
    x2 = r1.astype(BF16)
    r2 = r1 - x2.astype(F32)
    return x1, x2, r2.astype(BF16)


def _rms_rope(xh, cos_g, sin_g):
    ms = jnp.mean(xh * xh, axis=-1, keepdims=True)
    xh = xh * lax.rsqrt(ms + LN_EPS)
    return xh * cos_g + pltpu.roll(xh, LANES // 2, 1) * sin_g


def _proj_kernel(xp_ref, x_ref, xn_ref, lng_ref, lnb_ref, w_ref, wgt_ref, gbr_row_ref,
                 cq_ref, sq_ref, ck_ref, sk_ref, cw_ref, cb_ref,
                 xno_ref, qm_ref, ktm_ref, vom_ref, qa_ref, ka_ref, va_ref, qc_ref, gbr_ref, gi_ref, gf_ref,
                 *, tiles_per_seq):
    tm = x_ref.shape[0]
    lng = lng_ref[...]
    lnb = lnb_ref[...]
    xn = _layer_norm(x_ref[...], lng, lnb)
    xno_ref[...] = xn
    xb = xn.astype(BF16)

    def proj(lo, hi):
        return _dot(xb, w_ref[:, lo:hi])

    j = pl.program_id(0) % tiles_per_seq
    halo = jnp.concatenate([xp_ref[...], xn_ref[...]], axis=0)
    halo_u = _dot(_layer_norm(halo, lng, lnb).astype(BF16), w_ref[:, SEG_QKM[0]:SEG_QKM[1]])
    hr = xp_ref.shape[0]
    prev_rows = halo_u[hr - 1:hr, :] * jnp.where(j > 0, 1.0, 0.0).astype(F32)
    next_rows = halo_u[hr:hr + 1, :] * jnp.where(j < tiles_per_seq - 1, 1.0, 0.0).astype(F32)
    sub = lax.broadcasted_iota(jnp.int32, (tm, MLSTM_W), 0)

    def conv_silu(lo, scale):
        cols = slice(lo, lo + MLSTM_W)
        u = proj(SEG_QKM[0] + lo, SEG_QKM[0] + lo + MLSTM_W)
        u_prev = jnp.where(sub == 0, prev_rows[:, cols], pltpu.roll(u, 1, 0))
        u_next = jnp.where(sub == tm - 1, next_rows[:, cols], pltpu.roll(u, tm - 1, 0))
        y = u_prev * cw_ref[0:1, cols] + cb_ref[:, cols]
        y = y + u * cw_ref[1:2, cols]
        y = y + u_next * cw_ref[2:3, cols]
        y = y * _sigmoid(y)
        return y if scale == 1.0 else y * scale

    def q_job():
        qm_ref[...] = conv_silu(0, 1.0).astype(BF16)

    def k_job():
        yk = conv_silu(MLSTM_W, MLSTM_DH ** -0.5)
        for a in range(tm // MLSTM_CHUNK):
            for h in range(MLSTM_HEADS):
                blk = yk[a * MLSTM_CHUNK:(a + 1) * MLSTM_CHUNK, h * MLSTM_DH:(h + 1) * MLSTM_DH]
                r0 = a * MLSTM_W + h * MLSTM_DH
                ktm_ref[r0:r0 + MLSTM_DH, :] = blk.T.astype(BF16)

    def plain_job(out_ref, dst, src, piece):
        def job():
            out_ref[:, dst:dst + piece] = proj(src, src + piece).astype(out_ref.dtype)
        return job

    def rope_job(out_ref, dst, src, c_ref, s_ref):
        def job():
            blk = proj(src, src + pair)
            for jj in range(2):
                xh = _rms_rope(blk[:, jj * ATTN_DH:(jj + 1) * ATTN_DH], c_ref[...], s_ref[...])
                out_ref[:, dst + jj * ATTN_DH:dst + (jj + 1) * ATTN_DH] = xh.astype(BF16)
        return job

    pair = 2 * ATTN_DH
    vector_jobs = [q_job, k_job]
    vector_jobs += [rope_job(qa_ref, c - SEG_QA[0], c, cq_ref, sq_ref) for c in range(SEG_QA[0], SEG_QA[1], pair)]
    vector_jobs += [rope_job(ka_ref, c - SEG_KA[0], c, ck_ref, sk_ref) for c in range(SEG_KA[0], SEG_KA[1], pair)]
    plain_jobs = []
    for out_ref, seg, piece in ((vom_ref, SEG_VOM, 512), (va_ref, SEG_VA, KV_W),
                                (qc_ref, SEG_QC, 512), (gbr_ref, SEG_GBR, 512)):
        plain_jobs += [plain_job(out_ref, c - seg[0], c, piece) for c in range(seg[0], seg[1], piece)]

    while vector_jobs or plain_jobs:
        if plain_jobs:
            plain_jobs.pop(0)()
        if vector_jobs:
            vector_jobs.pop(0)()

    grow = _dot_nt(wgt_ref[...], xb)
    nd = N_GATES // 2
    for j in range(tm // LANES):
        blk = grow[:, j * LANES:(j + 1) * LANES] + gbr_row_ref[...]
        gi_ref[j * nd:(j + 1) * nd, :] = blk[:nd]
        gf_ref[j * nd:(j + 1) * nd, :] = blk[nd:]


def _in_proj(x2, T, w, tm):
    n = x2.shape[0]
    grid = (n // tm,)
    tiles_per_seq = T // tm
    row = lambda i: (i, 0)
    hb = 2 * SUBLANES
    nhb = n // hb
    out_shapes = (
        jax.ShapeDtypeStruct((n, D_MODEL), F32),
        jax.ShapeDtypeStruct((n, MLSTM_W), BF16),
        jax.ShapeDtypeStruct((n // MLSTM_CHUNK * MLSTM_W, MLSTM_CHUNK), BF16),
        jax.ShapeDtypeStruct((n, 2 * MLSTM_W), BF16),
        jax.ShapeDtypeStruct((n, ATTN_W), BF16),
        jax.ShapeDtypeStruct((n, KV_W), BF16),
        jax.ShapeDtypeStruct((n, KV_W), BF16),
        jax.ShapeDtypeStruct((n, MEM_W), BF16),
        jax.ShapeDtypeStruct((n, N_BRANCH * D_MODEL), BF16),
        jax.ShapeDtypeStruct((n // LANES * (N_GATES // 2), LANES), F32),
        jax.ShapeDtypeStruct((n // LANES * (N_GATES // 2), LANES), F32),
    )
    out_specs = (
        pl.BlockSpec((tm, D_MODEL), row),
        pl.BlockSpec((tm, MLSTM_W), row),
        pl.BlockSpec((tm // MLSTM_CHUNK * MLSTM_W, MLSTM_CHUNK), row),
        pl.BlockSpec((tm, 2 * MLSTM_W), row),
        pl.BlockSpec((tm, ATTN_W), row),
        pl.BlockSpec((tm, KV_W), row),
        pl.BlockSpec((tm, KV_W), row),
        pl.BlockSpec((tm, MEM_W), row),
        pl.BlockSpec((tm, N_BRANCH * D_MODEL), row),
        pl.BlockSpec((tm // LANES * (N_GATES // 2), LANES), row),
        pl.BlockSpec((tm // LANES * (N_GATES // 2), LANES), row),
    )
    in_specs = [
        pl.BlockSpec((hb, D_MODEL), lambda i: (jnp.maximum(i * (tm // hb) - 1, 0), 0)),
        pl.BlockSpec((tm, D_MODEL), row),
        pl.BlockSpec((hb, D_MODEL), lambda i: (jnp.minimum((i + 1) * (tm // hb), nhb - 1), 0)),
        _const_spec((1, D_MODEL)), _const_spec((1, D_MODEL)),
        _const_spec((D_MODEL, W_CAT)), _const_spec((N_GATES, D_MODEL)),
        _const_spec((N_GATES, LANES)),
    ] + [pl.BlockSpec((tm, LANES), lambda i: (i % tiles_per_seq, 0))] * 4 + [
        _const_spec((CONV_W, 2 * MLSTM_W)), _const_spec((1, 2 * MLSTM_W)),
    ]
    return pl.pallas_call(
        functools.partial(_proj_kernel, tiles_per_seq=tiles_per_seq),
        grid=grid, in_specs=in_specs, out_specs=out_specs, out_shape=out_shapes,
        compiler_params=_params(1), name="in_proj",
    )(x2, x2, x2, w["ln_in_g"], w["ln_in_b"], w["w_cat"], w["w_gt"], w["gb_row"],
      w["rope_cos_q"], w["rope_sin_q"], w["rope_cos_k"], w["rope_sin_k"],
      w["mlstm_conv_w"], w["mlstm_conv_b"])


N_DIRHEADS = 2 * MLSTM_HEADS
STAT_GROUPS = 6
STAT_BLOCK = (0, 0, 0, 1, 1, 1)
STAT_BLOCKS = 2


def _mlstm_select():
    lane = jnp.arange(LANES)
    grp, j = lane // N_DIRHEADS, lane % N_DIRHEADS
    blk = jnp.array(STAT_BLOCK + (-1,) * (LANES // N_DIRHEADS - STAT_GROUPS))[grp]
    out_blk = jnp.arange(STAT_BLOCKS * LANES) // LANES
    sel = (j[None, :, None] == jnp.arange(N_DIRHEADS)[:, None, None]) & (blk[None, :, None] == out_blk[None, None, :])
    return sel.astype(BF16)


def _mlstm_kernel(q_ref, kt_ref, vom_ref, gi_ref, gf_ref, sel_ref, ng_ref, hm_ref,
                  ktw_s, lhs_s, e_s, stat_s, gt_s, ma_s, mpf_s, mpb_s, r_s, wa_s, sp_s,
                  hf_s, st_s):
    T = q_ref.shape[0]
    L = MLSTM_CHUNK
    NC = T // L
    DH = MLSTM_DH
    H = MLSTM_HEADS

    ri = lax.broadcasted_iota(jnp.int32, (L, L), 0)
    ci = lax.broadcasted_iota(jnp.int32, (L, L), 1)
    causal = ci <= ri
    anti = ci >= ri
    tril = jnp.where(causal, 1.0, 0.0).astype(BF16)
    triu = jnp.where(anti, 1.0, 0.0).astype(BF16)
    ones_ll = jnp.ones((L, L), BF16)

    R = NC * N_DIRHEADS
    row_r = lax.broadcasted_iota(jnp.int32, (R, L), 0)
    lane_r = lax.broadcasted_iota(jnp.int32, (R, L), 1)
    is_fwd = (row_r % N_DIRHEADS) < H
    is_fwd8 = lax.broadcasted_iota(jnp.int32, (N_DIRHEADS, L), 0) < H

    li = gi_ref[...]
    y1, y2, y3 = _split3(_log_sigmoid(gf_ref[...]))

    def lane_sums(m):
        return _dot(y1, m) + _dot(y2, m) + _dot(y3, m)

    bl = jnp.where(is_fwd, lane_sums(triu), lane_sums(tril))
    gt = lane_sums(ones_ll)
    r = li - bl
    pm = r
    sm = r
    for sh in (1, 2, 4, 8, 16, 32, 64):
        pm = jnp.maximum(pm, jnp.where(lane_r >= sh, pltpu.roll(pm, sh, 1), -jnp.inf))
        sm = jnp.maximum(sm, jnp.where(lane_r < L - sh, pltpu.roll(sm, L - sh, 1), -jnp.inf))
    cm = jnp.where(is_fwd, pm, sm)
    a = gt - bl + li
    ma = jnp.broadcast_to(jnp.max(a, axis=-1, keepdims=True), (R, L))
    gt_s[...] = gt
    ma_s[...] = ma

    def chunk_rows(c):
        return pl.ds(pl.multiple_of(c * N_DIRHEADS, N_DIRHEADS), N_DIRHEADS)

    def stabiliser_scan(i, m):
        rf = chunk_rows(i)
        rb = chunk_rows(NC - 1 - i)
        mpf_s[rf, :] = m
        mpb_s[rb, :] = m
        gt8 = jnp.where(is_fwd8, gt_s[rf, :], gt_s[rb, :])
        ma8 = jnp.where(is_fwd8, ma_s[rf, :], ma_s[rb, :])
        return jnp.maximum(gt8 + m, ma8)

    lax.fori_loop(0, NC, stabiliser_scan, jnp.zeros((N_DIRHEADS, L), F32))

    m_prev = jnp.where(is_fwd, mpf_s[...], mpb_s[...])
    m_new = jnp.maximum(gt + m_prev, ma)
    sp_s[...] = jnp.exp(gt + m_prev - m_new)
    wa_s[...] = jnp.exp(a - m_new)
    r_s[...] = r
    ma_s[...] = m_prev
    u = -jnp.maximum(m_prev, cm)
    stat_vals = list(_split3(u)) + list(_split3(jnp.exp(u - bl)))
    for k, t in enumerate(stat_vals):
        stat_s[k] = t.astype(F32)
    stat_pad = jnp.zeros((L - STAT_GROUPS * N_DIRHEADS, L), F32)

    def prologue(c, carry):
        r0 = pl.multiple_of(c * L, L)
        rows = pl.ds(r0, L)
        rows8 = chunk_rows(c)
        stats = [stat_s[k, rows8, :] for k in range(STAT_GROUPS)] + [stat_pad]
        stats_t = jnp.concatenate(stats, axis=0).T.astype(BF16)
        wa = wa_s[rows8, :]
        r8 = r_s[rows8, :]
        mp8 = ma_s[rows8, :]
        kt_rows = pl.multiple_of(c * MLSTM_W, MLSTM_W)
        for h in range(H):
            hs = slice(h * DH, (h + 1) * DH)
            q = q_ref[rows, hs]
            kt_b = kt_ref[pl.ds(kt_rows + h * DH, DH), :]
            s = _dot(q, kt_b)
            kt = kt_b.astype(F32)
            for d in range(2):
                j = H * d + h
                bc = _dot(stats_t, sel_ref[j])
                u_b = bc[:, :L]
                p = jnp.where(causal if d == 0 else anti, jnp.exp(u_b + r8[j:j + 1, :]), 0.0)
                w_inter = jnp.exp(u_b + mp8[j:j + 1, :])
                lhs_s[d, rows, 2 * h * DH:(2 * h + 1) * DH] = (s * p).astype(BF16)
                lhs_s[d, rows, (2 * h + 1) * DH:(2 * h + 2) * DH] = w_inter.astype(BF16) * q
                e_s[j, rows, :] = bc[:, L:]
                ktw_s[d, c, hs, :] = (kt * wa[j:j + 1, :]).astype(BF16)
        return carry

    lax.fori_loop(0, NC, prologue, 0, unroll=4)

    ones_blk = jnp.ones((L, DH), BF16)
    ng = ng_ref[...]

    def run_direction(d):
        st_s[...] = jnp.zeros_like(st_s)

        def body(i, carry):
            c = i if d == 0 else NC - 1 - i
            r0 = pl.multiple_of(c * L, L)
            rows = pl.ds(r0, L)
            for h in range(H):
                j = H * d + h
                hs = slice(h * DH, (h + 1) * DH)
                lhs = lhs_s[d, rows, 2 * h * DH:(2 * h + 2) * DH]
                vaug = jnp.concatenate([vom_ref[rows, hs], ones_blk], axis=1)
                c_prev = st_s[h]
                tot = _dot(lhs, jnp.concatenate([vaug, c_prev.astype(BF16)], axis=0))
                h_out = tot[:, :DH] / jnp.maximum(jnp.abs(tot[:, DH:]), e_s[j, rows, :])
                sp = sp_s[pl.ds(c * N_DIRHEADS + j, 1), :]
                st_s[h] = jnp.concatenate([sp, sp], axis=1) * c_prev + _dot(ktw_s[d, c, hs, :], vaug)

                if d == 0:
                    hf_s[rows, hs] = h_out
                else:
                    hh = hf_s[rows, hs] + h_out
                    mu = jnp.mean(hh, axis=-1, keepdims=True)
                    hc = hh - mu
                    var = jnp.mean(hc * hc, axis=-1, keepdims=True)
                    hn = hc * lax.rsqrt(var + LN_EPS) * ng[:, hs]
                    o = vom_ref[rows, MLSTM_W + h * DH:MLSTM_W + (h + 1) * DH].astype(F32)
                    hm_ref[rows, hs] = (hn * _sigmoid(o)).astype(BF16)
            return carry

        lax.fori_loop(0, NC, body, 0, unroll=4)

    run_direction(0)
    run_direction(1)


def _mlstm(qm, ktm, vom, gi, gf, T, w):
    n = qm.shape[0]
    B = n // T
    NC = T // MLSTM_CHUNK
    seq = lambda b: (b, 0)
    gate_major = pltpu.VMEM((NC * N_DIRHEADS, MLSTM_CHUNK), F32)
    return pl.pallas_call(
        _mlstm_kernel, grid=(B,),
        in_specs=[
            pl.BlockSpec((T, MLSTM_W), seq),
            pl.BlockSpec((NC * MLSTM_W, MLSTM_CHUNK), seq),
            pl.BlockSpec((T, 2 * MLSTM_W), seq),
            pl.BlockSpec((NC * N_DIRHEADS, MLSTM_CHUNK), seq),
            pl.BlockSpec((NC * N_DIRHEADS, MLSTM_CHUNK), seq),
            _const_spec((N_DIRHEADS, LANES, STAT_BLOCKS * MLSTM_CHUNK)),
            _const_spec((1, MLSTM_W)),
        ],
        out_specs=pl.BlockSpec((T, MLSTM_W), seq),
        out_shape=jax.ShapeDtypeStruct((n, MLSTM_W), BF16),
        scratch_shapes=[
            pltpu.VMEM((2, NC, MLSTM_W, MLSTM_CHUNK), BF16),
            pltpu.VMEM((2, T, 2 * MLSTM_W), BF16),
            pltpu.VMEM((N_DIRHEADS, T, LANES), F32),
            pltpu.VMEM((STAT_GROUPS, NC * N_DIRHEADS, MLSTM_CHUNK), F32),
            gate_major, gate_major, gate_major, gate_major,
            gate_major, gate_major, gate_major,
            pltpu.VMEM((T, MLSTM_W), F32),
            pltpu.VMEM((MLSTM_HEADS, MLSTM_DH, 2 * MLSTM_DH), F32),
        ],
        compiler_params=_params(1), name="mlstm",
    )(qm, ktm, vom, gi, gf, w["mlstm_sel"], w["mlstm_norm_g"])


def _attn_kernel(q_ref, k_ref, v_ref, o_ref, vaug_s):
    @pl.when(pl.program_id(2) == 0)
    def _():
        vaug_s[:, :ATTN_DH] = v_ref[...]
        vaug_s[:, ATTN_DH:] = jnp.ones((v_ref.shape[0], ATTN_DH), BF16)

    k = k_ref[...]
    tq = q_ref.shape[0]
    rows_per_unit = min(tq, ATTN_UNIT_ROWS)
    for g in range(ATTN_GROUP):
        hs = slice(g * ATTN_DH, (g + 1) * ATTN_DH)
        for r0 in range(0, tq, rows_per_unit):
            rs = slice(r0, r0 + rows_per_unit)
            s = _dot_nt(q_ref[rs, hs], k)
            m = jnp.max(s, axis=-1, keepdims=True)
            p = jnp.exp2(s - m)
            oa = _dot(p.astype(BF16), vaug_s[...])
            o_ref[rs, hs] = (oa[:, :ATTN_DH] / oa[:, ATTN_DH:ATTN_DH + 1]).astype(o_ref.dtype)


def _attention(qa, ka, va, T, tq):
    n = qa.shape[0]
    B = n // T
    nq = T // tq
    gw = ATTN_GROUP * ATTN_DH
    return pl.pallas_call(
        _attn_kernel, grid=(B, KV_HEADS, nq),
        in_specs=[
            pl.BlockSpec((tq, gw), lambda b, h, i: (b * nq + i, h)),
            pl.BlockSpec((T, ATTN_DH), lambda b, h, i: (b, h)),
            pl.BlockSpec((T, ATTN_DH), lambda b, h, i: (b, h)),
        ],
        out_specs=pl.BlockSpec((tq, gw), lambda b, h, i: (b * nq + i, h)),
        out_shape=jax.ShapeDtypeStruct((n, ATTN_W), BF16),
        scratch_shapes=[pltpu.VMEM((T, 2 * ATTN_DH), BF16)],
        compiler_params=_params(3), name="gqa_attn",
    )(qa, ka, va)


def _mem_kv_kernel(m_ref, w_ref, o_ref):
    o_ref[...] = _dot(m_ref[...].astype(BF16), w_ref[...]).astype(o_ref.dtype)


def _mem_kv(mem2, w):
    n = mem2.shape[0]
    tm = _tile(n, 1024)
    return pl.pallas_call(
        _mem_kv_kernel, grid=(n // tm,),
        in_specs=[pl.BlockSpec((tm, D_MODEL), lambda b: (b, 0)),
                  _const_spec((D_MODEL, 2 * MEM_W))],
        out_specs=pl.BlockSpec((tm, 2 * MEM_W), lambda b: (b, 0)),
        out_shape=jax.ShapeDtypeStruct((n, 2 * MEM_W), BF16),
        compiler_params=_params(1), name="mem_kv",
    )(mem2, w["w_mem_kv"])


def _merge_kernel(xn_ref, hm_ref, ha_ref, qc_ref, kv_ref, gbr_ref,
                  wbm_ref, wba_ref, wbc_ref, wo_ref, l1g_ref, l1b_ref, x1_ref):
    ones_blk = jnp.ones((kv_ref.shape[0], MEM_DH), BF16)
    tm = xn_ref.shape[0]
    unit = min(tm, ROW_UNIT)
    for r0 in range(0, tm, unit):
        rs = slice(r0, r0 + unit)
        hc_parts = []
        for h in range(MEM_HEADS):
            hs = slice(h * MEM_DH, (h + 1) * MEM_DH)
            kc = kv_ref[:, hs]
            vaug = jnp.concatenate([kv_ref[:, MEM_W + h * MEM_DH:MEM_W + (h + 1) * MEM_DH], ones_blk], axis=1)
            s = _dot_nt(qc_ref[rs, hs], kc)
            p = jnp.exp2(s - jnp.max(s, axis=-1, keepdims=True))
            oa = _dot(p.astype(BF16), vaug)
            hc_parts.append((oa[:, :MEM_DH] / oa[:, MEM_DH:MEM_DH + 1]).astype(BF16))
        hc = jnp.concatenate(hc_parts, axis=1)

        def gate(j):
            return _sigmoid(gbr_ref[rs, j * D_MODEL:(j + 1) * D_MODEL].astype(F32))

        merged = gate(0) * _dot(hm_ref[rs, :], wbm_ref[...])
        merged = merged + gate(1) * _dot(ha_ref[rs, :], wba_ref[...])
        merged = merged + gate(2) * _dot(hc, wbc_ref[...])
        mixed = _dot(merged.astype(BF16), wo_ref[...])
        x1_ref[rs, :] = _layer_norm(DEEPNORM_ALPHA * xn_ref[rs, :] + mixed, l1g_ref[...], l1b_ref[...])


def _merge(xn, hm, ha, qc, kvc, gbr, T, M, w, tm):
    n = xn.shape[0]
    tiles_per_seq = T // tm
    row = lambda i: (i, 0)
    return pl.pallas_call(
        _merge_kernel, grid=(n // tm,),
        in_specs=[
            pl.BlockSpec((tm, D_MODEL), row),
            pl.BlockSpec((tm, MLSTM_W), row),
            pl.BlockSpec((tm, ATTN_W), row),
            pl.BlockSpec((tm, MEM_W), row),
            pl.BlockSpec((M, 2 * MEM_W), lambda i: (i // tiles_per_seq, 0)),
            pl.BlockSpec((tm, N_BRANCH * D_MODEL), row),
            _const_spec((MLSTM_W, D_MODEL)), _const_spec((ATTN_W, D_MODEL)),
            _const_spec((MEM_W, D_MODEL)), _const_spec((D_MODEL, D_MODEL)),
            _const_spec((1, D_MODEL)), _const_spec((1, D_MODEL)),
        ],
        out_specs=pl.BlockSpec((tm, D_MODEL), row),
        out_shape=jax.ShapeDtypeStruct((n, D_MODEL), F32),
        compiler_params=_params(1), name="merge_out",
    )(xn, hm, ha, qc, kvc, gbr, w["w_branch_mlstm"],
      w["w_branch_attn"], w["w_branch_mem"], w["w_out"], w["ln1_g"], w["ln1_b"])


GELU_C = 0.7978845608028654
GELU_C3 = GELU_C * 0.044715


def _gelu_tanh_x2(x):
    return x * (1.0 + jnp.tanh(x * (GELU_C + GELU_C3 * (x * x))))


def _ffn_kernel(xp_ref, x_ref, xn_ref, wup_ref, cw_ref, cb_ref, wdn_ref, l2g_ref, l2b_ref,
                y_ref, xe_s, act_s, *, tiles_per_seq):
    tm = x_ref.shape[0]
    i = pl.program_id(0)
    j = i % tiles_per_seq
    prev = xp_ref[...] * jnp.where(j > 0, 1.0, 0.0).astype(F32)
    nxt = xn_ref[...] * jnp.where(j < tiles_per_seq - 1, 1.0, 0.0).astype(F32)
    xe_s[...] = jnp.concatenate([prev, x_ref[...], nxt], axis=0).astype(BF16)
    te = tm + 2 * SUBLANES

    def conv(u, cw, cb):
        up = pltpu.roll(u, 1, 0)[SUBLANES:SUBLANES + tm, :]
        uc = u[SUBLANES:SUBLANES + tm, :]
        un = pltpu.roll(u, te - 1, 0)[SUBLANES:SUBLANES + tm, :]
        y = up * cw[0:1, :] + cb
        y = y + uc * cw[1:2, :]
        return y + un * cw[2:3, :]

    def up_conv(lo):
        cols = slice(lo, lo + FFN_CHUNK)
        return conv(_dot(xe_s[...], wup_ref[:, cols]), cw_ref[:, cols], cb_ref[:, cols])

    for c in range(0, D_FF, FFN_CHUNK):
        act_s[:, c:c + FFN_CHUNK] = (_gelu_tanh_x2(up_conv(c)) * up_conv(D_FF + c)).astype(BF16)

    ff = _dot(act_s[...], wdn_ref[...])
    y_ref[...] = _layer_norm(DEEPNORM_ALPHA * x_ref[...] + ff, l2g_ref[...], l2b_ref[...])


def _ffn(x1, T, w, tm):
    n = x1.shape[0]
    tiles_per_seq = T // tm
    r8 = tm // SUBLANES
    nblk8 = n // SUBLANES
    row = lambda i: (i, 0)
    return pl.pallas_call(
        functools.partial(_ffn_kernel, tiles_per_seq=tiles_per_seq), grid=(n // tm,),
        in_specs=[
            pl.BlockSpec((SUBLANES, D_MODEL), lambda i: (jnp.maximum(i * r8 - 1, 0), 0)),
            pl.BlockSpec((tm, D_MODEL), row),
            pl.BlockSpec((SUBLANES, D_MODEL), lambda i: (jnp.minimum((i + 1) * r8, nblk8 - 1), 0)),
            _const_spec((D_MODEL, 2 * D_FF)),
            _const_spec((CONV_W, 2 * D_FF)),
            _const_spec((1, 2 * D_FF)),
            _const_spec((D_FF, D_MODEL)),
            _const_spec((1, D_MODEL)), _const_spec((1, D_MODEL)),
        ],
        out_specs=pl.BlockSpec((tm, D_MODEL), row),
        out_shape=jax.ShapeDtypeStruct((n, D_MODEL), F32),
        scratch_shapes=[pltpu.VMEM((tm + 2 * SUBLANES, D_MODEL), BF16),
                        pltpu.VMEM((tm, D_FF), BF16)],
        compiler_params=_params(1), name="conv_ffn",
    )(x1, x1, x1, w["w_ffn_up"], w["ffn_conv_w"], w["ffn_conv_b"], w["w_ffn_down"],
      w["ln2_g"], w["ln2_b"])


def _rope_tables(T):
    rows = T // GRID_W
    row = jnp.repeat(jnp.arange(rows, dtype=F32), GRID_W)
    col = jnp.tile(jnp.arange(GRID_W, dtype=F32), rows)
    inv_freq = ROPE_THETA ** (-jnp.arange(0, ROPE_AXIS_DIM, 2, dtype=F32) / ROPE_AXIS_DIM)
    ang_r = row[:, None] * inv_freq
    ang_c = col[:, None] * inv_freq
    cos = jnp.concatenate([jnp.cos(ang_r), jnp.cos(ang_c)] * 2, axis=-1)
    sin = jnp.concatenate([-jnp.sin(ang_r), -jnp.sin(ang_c), jnp.sin(ang_r), jnp.sin(ang_c)], axis=-1)
    return cos, sin


def _rope_head_perm():
    p = jnp.arange(ATTN_DH)
    quarter = ROPE_AXIS_DIM // 2
    half, axis, j = p // ROPE_AXIS_DIM, (p % ROPE_AXIS_DIM) // quarter, p % quarter
    return axis * ROPE_AXIS_DIM + half * quarter + j


def _gained_tables(cos, sin, g, scale):
    g = g.astype(F32)[_rope_head_perm()]
    return cos * (g * scale), sin * (jnp.roll(g, ATTN_DH // 2) * scale)


def _prep_weights(l, T, ln_in_g, ln_in_b, w_in, mlstm_gate_bias, mlstm_conv_w, mlstm_conv_b,
                  mlstm_norm_g, attn_q_norm_g, attn_k_norm_g, w_mem_kv, w_branch_mlstm,
                  w_branch_attn, w_branch_mem, w_out, ln1_g, ln1_b, w_ffn_up, ffn_conv_w,
                  ffn_conv_b, w_ffn_down, ln2_g, ln2_b):
    wi = w_in[l]
    g0 = 4 * MLSTM_W
    g1 = g0 + N_GATES
    w_gate = wi[:, g0:g1]
    qk0 = g1
    qk1 = qk0 + ATTN_W + KV_W
    qc0 = qk1 + KV_W
    qc1 = qc0 + MEM_W
    quarter = ROPE_AXIS_DIM // 2
    w_qk = wi[:, qk0:qk1].reshape(D_MODEL, ATTN_HEADS + KV_HEADS, 2, 2, quarter)
    w_qk = jnp.swapaxes(w_qk, 2, 3).reshape(D_MODEL, ATTN_W + KV_W)
    w_qc = wi[:, qc0:qc1] * (MEM_DH ** -0.5 * LOG2_E)
    w_cat = jnp.concatenate([wi[:, :g0], w_qk, wi[:, qk1:qc0], w_qc, wi[:, qc1:]], axis=1).astype(BF16)
    gate_rows = jnp.array(GATE_ROW_ORDER)
    w_gate = w_gate[:, gate_rows]
    gb = mlstm_gate_bias[l].astype(F32)[gate_rows]
    cos, sin = _rope_tables(T)
    cos_q, sin_q = _gained_tables(cos, sin, attn_q_norm_g[l], ATTN_DH ** -0.5 * LOG2_E)
    cos_k, sin_k = _gained_tables(cos, sin, attn_k_norm_g[l], 1.0)
    r2 = lambda a: a.reshape(1, -1).astype(F32)
    ffn_half = jnp.where(jnp.arange(2 * D_FF) >= D_FF, 0.5, 1.0).astype(F32)[None, :]
    return {
        "ln_in_g": r2(ln_in_g), "ln_in_b": r2(ln_in_b),
        "w_cat": w_cat, "w_gt": w_gate.T.astype(BF16),
        "gb_row": jnp.broadcast_to(gb[:, None], (N_GATES, LANES)),
        "mlstm_sel": _mlstm_select(),
        "rope_cos_q": cos_q, "rope_sin_q": sin_q, "rope_cos_k": cos_k, "rope_sin_k": sin_k,
        "mlstm_conv_w": mlstm_conv_w[l].astype(F32), "mlstm_conv_b": r2(mlstm_conv_b[l]),
        "mlstm_norm_g": r2(mlstm_norm_g[l]),
        "w_mem_kv": w_mem_kv[l].astype(BF16),
        "w_branch_mlstm": w_branch_mlstm[l].astype(BF16),
        "w_branch_attn": w_branch_attn[l].astype(BF16),
        "w_branch_mem": w_branch_mem[l].astype(BF16),
        "w_out": w_out[l].astype(BF16),
        "ln1_g": r2(ln1_g[l]), "ln1_b": r2(ln1_b[l]),
        "w_ffn_up": w_ffn_up[l].astype(BF16),
        "ffn_conv_w": ffn_conv_w[l].astype(F32) * ffn_half,
        "ffn_conv_b": r2(ffn_conv_b[l]) * ffn_half,
        "w_ffn_down": w_ffn_down[l].astype(BF16),
        "ln2_g": r2(ln2_g[l]), "ln2_b": r2(ln2_b[l]),
    }


def _tile(T, want):
    t = min(want, T)
    assert T % t == 0
    return t


def _trunk(x, mem, w):
    B, T, _ = x.shape
    M = mem.shape[1]
    assert T % MLSTM_CHUNK == 0 and T % GRID_W == 0
    x2 = x.reshape(B * T, D_MODEL)
    mem2 = mem.reshape(B * M, D_MODEL)
    xn, qm, ktm, vom, qa, ka, va, qc, gbr, gi, gf = _in_proj(x2, T, w, _tile(T, 512))
    hm = _mlstm(qm, ktm, vom, gi, gf, T, w)
    ha = _attention(qa, ka, va, T, _tile(T, 2048))
    kvc = _mem_kv(mem2, w)
    x1 = _merge(xn, hm, ha, qc, kvc, gbr, T, M, w, _tile(T, 512))
    y = _ffn(x1, T, w, _tile(T, 512))
    return y.reshape(B, T, D_MODEL)


def kernel(x_prompt, x_sample, mem_prompt, mem_sample, ln_in_g, ln_in_b, w_in, mlstm_gate_bias, mlstm_conv_w, mlstm_conv_b, mlstm_norm_g, attn_q_norm_g, attn_k_norm_g, w_mem_kv, w_branch_mlstm, w_branch_attn, w_branch_mem, w_out, ln1_g, ln1_b, w_ffn_up, ffn_conv_w, ffn_conv_b, w_ffn_down, ln2_g, ln2_b):
    assert DEPTH == 1 and x_prompt.shape[1] == x_sample.shape[1]
    w = _prep_weights(0, x_prompt.shape[1], ln_in_g, ln_in_b, w_in, mlstm_gate_bias, mlstm_conv_w,
                      mlstm_conv_b, mlstm_norm_g, attn_q_norm_g, attn_k_norm_g, w_mem_kv,
                      w_branch_mlstm, w_branch_attn, w_branch_mem, w_out, ln1_g, ln1_b, w_ffn_up,
                      ffn_conv_w, ffn_conv_b, w_ffn_down, ln2_g, ln2_b)
    return (_trunk(x_prompt, mem_prompt, w), _trunk(x_sample, mem_sample, w))
```

```python
import functools

import jax
import jax.numpy as jnp
from jax import lax
from jax.experimental import pallas as pl
from jax.experimental.pallas import tpu as pltpu

F32 = jnp.float32
BF16 = jnp.bfloat16

D_MODEL = 1024
DEPTH = 1
GRID_W = 64
MLSTM_HEADS = 4
MLSTM_DH = 128
MLSTM_W = MLSTM_HEADS * MLSTM_DH
MLSTM_CHUNK = 128
N_GATES = 4 * MLSTM_HEADS
ATTN_DH = 128
ATTN_HEADS = 8
KV_HEADS = 2
ATTN_GROUP = ATTN_HEADS // KV_HEADS
ATTN_W = ATTN_HEADS * ATTN_DH
KV_W = KV_HEADS * ATTN_DH
ROPE_AXIS_DIM = ATTN_DH // 2
ROPE_THETA = 10000.0
MEM_HEADS = 4
MEM_DH = 128
MEM_W = MEM_HEADS * MEM_DH
N_BRANCH = 3
D_FF = ((8 * D_MODEL // 3 + 127) // 128) * 128
CONV_W = 3
DEEPNORM_ALPHA = (2.0 * DEPTH) ** 0.25
LN_EPS = 1e-5
LOG2_E = 1.4426950408889634

LANES = 128
SUBLANES = 8
VMEM_LIMIT_BYTES = 56 * 1024 * 1024

SEG_QKM = (0, 2 * MLSTM_W)
SEG_VOM = (SEG_QKM[1], SEG_QKM[1] + 2 * MLSTM_W)
SEG_QA = (SEG_VOM[1], SEG_VOM[1] + ATTN_W)
SEG_KA = (SEG_QA[1], SEG_QA[1] + KV_W)
SEG_VA = (SEG_KA[1], SEG_KA[1] + KV_W)
SEG_QC = (SEG_VA[1], SEG_VA[1] + MEM_W)
SEG_GBR = (SEG_QC[1], SEG_QC[1] + N_BRANCH * D_MODEL)
W_CAT = SEG_GBR[1]
GATE_ROW_ORDER = tuple(g * MLSTM_HEADS + h for g in (0, 2, 1, 3) for h in range(MLSTM_HEADS))

FFN_CHUNK = 256
ATTN_UNIT_ROWS = 128


def _params(n_axes):
    return pltpu.CompilerParams(dimension_semantics=("arbitrary",) * n_axes,
                                vmem_limit_bytes=VMEM_LIMIT_BYTES)


def _const_spec(shape):
    nd = len(shape)
    return pl.BlockSpec(shape, lambda *_: (0,) * nd, pipeline_mode=pl.Buffered(1))


def _layer_norm(x, g, b):
    mu = jnp.mean(x, axis=-1, keepdims=True)
    xc = x - mu
    var = jnp.mean(xc * xc, axis=-1, keepdims=True)
    return xc * lax.rsqrt(var + LN_EPS) * g + b


def _sigmoid(x):
    return 1.0 / (1.0 + jnp.exp(-x))


def _log_sigmoid(x):
    return jnp.minimum(x, 0.0) - jnp.log(1.0 + jnp.exp(-jnp.abs(x)))


def _dot(a, b):
    return jnp.dot(a, b, preferred_element_type=F32)


def _dot_nt(a, b):
    return lax.dot_general(a, b, (((1,), (1,)), ((), ())), preferred_element_type=F32)


def _split3(x):
    x1 = x.astype(BF16)
    r1 = x - x1.astype(F32)
    x2 = r1.astype(BF16)
    r2 = r1 - x2.astype(F32)
    return x1, x2, r2.astype(BF16)


def _rms_rope(xh, cos_g, sin_g):
    ms = jnp.mean(xh * xh, axis=-1, keepdims=True)
    xh = xh * lax.rsqrt(ms + LN_EPS)
    return xh * cos_g + pltpu.roll(xh, LANES // 2, 1) * sin_g


def _proj_kernel(xp_ref, x_ref, xn_ref, lng_ref, lnb_ref, w_ref, wgt_ref, gbr_row_ref,
                 cq_ref, sq_ref, ck_ref, sk_ref, cw_ref, cb_ref,
                 xno_ref, qm_ref, ktm_ref, vom_ref, qa_ref, ka_ref, va_ref, qc_ref, gbr_ref, gi_ref, gf_ref,
                 *, tiles_per_seq):
    tm = x_ref.shape[0]
    lng = lng_ref[...]
    lnb = lnb_ref[...]
    xn = _layer_norm(x_ref[...], lng, lnb)
    xno_ref[...] = xn
    xb = xn.astype(BF16)

    def proj(lo, hi):
        return _dot(xb, w_ref[:, lo:hi])

    j = pl.program_id(0) % tiles_per_seq
    hr = xp_ref.shape[0]
    te = tm + 2 * hr
    xbe = jnp.concatenate([_layer_norm(xp_ref[...], lng, lnb).astype(BF16), xb,
                           _layer_norm(xn_ref[...], lng, lnb).astype(BF16)], axis=0)
    keep_prev = jnp.where(j > 0, 1.0, 0.0).astype(F32)
    keep_next = jnp.where(j < tiles_per_seq - 1, 1.0, 0.0).astype(F32)

    def conv_silu(lo, scale):
        cols = slice(lo, lo + MLSTM_W)
        ue = _dot(xbe, w_ref[:, SEG_QKM[0] + lo:SEG_QKM[0] + lo + MLSTM_W])
        ue = jnp.concatenate([ue[:hr] * keep_prev, ue[hr:hr + tm], ue[hr + tm:] * keep_next], axis=0)
        u_prev = pltpu.roll(ue, 1, 0)[hr:hr + tm]
        u_next = pltpu.roll(ue, te - 1, 0)[hr:hr + tm]
        y = u_prev * cw_ref[0:1, cols] + cb_ref[:, cols]
        y = y + ue[hr:hr + tm] * cw_ref[1:2, cols]
        y = y + u_next * cw_ref[2:3, cols]
        y = y * _sigmoid(y)
        return y if scale == 1.0 else y * scale

    def q_job():
        qm_ref[...] = conv_silu(0, 1.0).astype(BF16)

    def k_job():
        yk = conv_silu(MLSTM_W, MLSTM_DH ** -0.5)
        for a in range(tm // MLSTM_CHUNK):
            for h in range(MLSTM_HEADS):
                blk = yk[a * MLSTM_CHUNK:(a + 1) * MLSTM_CHUNK, h * MLSTM_DH:(h + 1) * MLSTM_DH]
                r0 = a * MLSTM_W + h * MLSTM_DH
                ktm_ref[r0:r0 + MLSTM_DH, :] = blk.T.astype(BF16)

    def plain_job(out_ref, dst, src, piece):
        def job():
            out_ref[:, dst:dst + piece] = proj(src, src + piece).astype(out_ref.dtype)
        return job

    def rope_job(out_ref, dst, src, c_ref, s_ref):
        def job():
            blk = proj(src, src + pair)
            for jj in range(2):
                xh = _rms_rope(blk[:, jj * ATTN_DH:(jj + 1) * ATTN_DH], c_ref[...], s_ref[...])
                out_ref[:, dst + jj * ATTN_DH:dst + (jj + 1) * ATTN_DH] = xh.astype(BF16)
        return job

    pair = 2 * ATTN_DH
    vector_jobs = [q_job, k_job]
    vector_jobs += [rope_job(qa_ref, c - SEG_QA[0], c, cq_ref, sq_ref) for c in range(SEG_QA[0], SEG_QA[1], pair)]
    vector_jobs += [rope_job(ka_ref, c - SEG_KA[0], c, ck_ref, sk_ref) for c in range(SEG_KA[0], SEG_KA[1], pair)]
    plain_jobs = []
    for out_ref, seg, piece in ((vom_ref, SEG_VOM, 512), (va_ref, SEG_VA, KV_W),
                                (qc_ref, SEG_QC, 512), (gbr_ref, SEG_GBR, 512)):
        plain_jobs += [plain_job(out_ref, c - seg[0], c, piece) for c in range(seg[0], seg[1], piece)]

    while vector_jobs or plain_jobs:
        if plain_jobs:
            plain_jobs.pop(0)()
        if vector_jobs:
            vector_jobs.pop(0)()

    grow = _dot_nt(wgt_ref[...], xb)
    nd = N_GATES // 2
    for j in range(tm // LANES):
        blk = grow[:, j * LANES:(j + 1) * LANES] + gbr_row_ref[...]
        gi_ref[j * nd:(j + 1) * nd, :] = blk[:nd]
        gf_ref[j * nd:(j + 1) * nd, :] = blk[nd:]


def _in_proj(x2, T, w, tm):
    n = x2.shape[0]
    grid = (n // tm,)
    tiles_per_seq = T // tm
    row = lambda i: (i, 0)
    hb = 2 * SUBLANES
    nhb = n // hb
    out_shapes = (
        jax.ShapeDtypeStruct((n, D_MODEL), F32),
        jax.ShapeDtypeStruct((n, MLSTM_W), BF16),
        jax.ShapeDtypeStruct((n // MLSTM_CHUNK * MLSTM_W, MLSTM_CHUNK), BF16),
        jax.ShapeDtypeStruct((n, 2 * MLSTM_W), BF16),
        jax.ShapeDtypeStruct((n, ATTN_W), BF16),
        jax.ShapeDtypeStruct((n, KV_W), BF16),
        jax.ShapeDtypeStruct((n, KV_W), BF16),
        jax.ShapeDtypeStruct((n, MEM_W), BF16),
        jax.ShapeDtypeStruct((n, N_BRANCH * D_MODEL), BF16),
        jax.ShapeDtypeStruct((n // LANES * (N_GATES // 2), LANES), F32),
        jax.ShapeDtypeStruct((n // LANES * (N_GATES // 2), LANES), F32),
    )
    out_specs = (
        pl.BlockSpec((tm, D_MODEL), row),
        pl.BlockSpec((tm, MLSTM_W), row),
        pl.BlockSpec((tm // MLSTM_CHUNK * MLSTM_W, MLSTM_CHUNK), row),
        pl.BlockSpec((tm, 2 * MLSTM_W), row),
        pl.BlockSpec((tm, ATTN_W), row),
        pl.BlockSpec((tm, KV_W), row),
        pl.BlockSpec((tm, KV_W), row),
        pl.BlockSpec((tm, MEM_W), row),
        pl.BlockSpec((tm, N_BRANCH * D_MODEL), row),
        pl.BlockSpec((tm // LANES * (N_GATES // 2), LANES), row),
        pl.BlockSpec((tm // LANES * (N_GATES // 2), LANES), row),
    )
    in_specs = [
        pl.BlockSpec((hb, D_MODEL), lambda i: (jnp.maximum(i * (tm // hb) - 1, 0), 0)),
        pl.BlockSpec((tm, D_MODEL), row),
        pl.BlockSpec((hb, D_MODEL), lambda i: (jnp.minimum((i + 1) * (tm // hb), nhb - 1), 0)),
        _const_spec((1, D_MODEL)), _const_spec((1, D_MODEL)),
        _const_spec((D_MODEL, W_CAT)), _const_spec((N_GATES, D_MODEL)),
        _const_spec((N_GATES, LANES)),
    ] + [pl.BlockSpec((tm, LANES), lambda i: (i % tiles_per_seq, 0))] * 4 + [
        _const_spec((CONV_W, 2 * MLSTM_W)), _const_spec((1, 2 * MLSTM_W)),
    ]
    return pl.pallas_call(
        functools.partial(_proj_kernel, tiles_per_seq=tiles_per_seq),
        grid=grid, in_specs=in_specs, out_specs=out_specs, out_shape=out_shapes,
        compiler_params=_params(1), name="in_proj",
    )(x2, x2, x2, w["ln_in_g"], w["ln_in_b"], w["w_cat"], w["w_gt"], w["gb_row"],
      w["rope_cos_q"], w["rope_sin_q"], w["rope_cos_k"], w["rope_sin_k"],
      w["mlstm_conv_w"], w["mlstm_conv_b"])


N_DIRHEADS = 2 * MLSTM_HEADS
STAT_GROUPS = 6
STAT_BLOCK = (0, 0, 0, 1, 1, 1)
STAT_BLOCKS = 2


def _mlstm_select():
    lane = jnp.arange(LANES)
    grp, j = lane // N_DIRHEADS, lane % N_DIRHEADS
    blk = jnp.array(STAT_BLOCK + (-1,) * (LANES // N_DIRHEADS - STAT_GROUPS))[grp]
    out_blk = jnp.arange(STAT_BLOCKS * LANES) // LANES
    sel = (j[None, :, None] == jnp.arange(N_DIRHEADS)[:, None, None]) & (blk[None, :, None] == out_blk[None, None, :])
    return sel.astype(BF16)


def _mlstm_kernel(q_ref, kt_ref, vom_ref, gi_ref, gf_ref, sel_ref, ng_ref, hm_ref,
                  ktw_s, lhs_s, e_s, stat_s, gt_s, ma_s, mpf_s, mpb_s, r_s, wa_s, sp_s,
                  hf_s, st_s):
    T = q_ref.shape[0]
    L = MLSTM_CHUNK
    NC = T // L
    DH = MLSTM_DH
    H = MLSTM_HEADS

    ri = lax.broadcasted_iota(jnp.int32, (L, L), 0)
    ci = lax.broadcasted_iota(jnp.int32, (L, L), 1)
    causal = ci <= ri
    anti = ci >= ri
    tril = jnp.where(causal, 1.0, 0.0).astype(BF16)
    triu = jnp.where(anti, 1.0, 0.0).astype(BF16)
    ones_ll = jnp.ones((L, L), BF16)

    R = NC * N_DIRHEADS
    row_r = lax.broadcasted_iota(jnp.int32, (R, L), 0)
    lane_r = lax.broadcasted_iota(jnp.int32, (R, L), 1)
    is_fwd = (row_r % N_DIRHEADS) < H
    is_fwd8 = lax.broadcasted_iota(jnp.int32, (N_DIRHEADS, L), 0) < H

    li = gi_ref[...]
    y1, y2, y3 = _split3(_log_sigmoid(gf_ref[...]))

    def lane_sums(m):
        return _dot(y1, m) + _dot(y2, m) + _dot(y3, m)

    bl = jnp.where(is_fwd, lane_sums(triu), lane_sums(tril))
    gt = lane_sums(ones_ll)
    r = li - bl
    pm = r
    sm = r
    for sh in (1, 2, 4, 8, 16, 32, 64):
        pm = jnp.maximum(pm, jnp.where(lane_r >= sh, pltpu.roll(pm, sh, 1), -jnp.inf))
        sm = jnp.maximum(sm, jnp.where(lane_r < L - sh, pltpu.roll(sm, L - sh, 1), -jnp.inf))
    cm = jnp.where(is_fwd, pm, sm)
    a = gt - bl + li
    ma = jnp.broadcast_to(jnp.max(a, axis=-1, keepdims=True), (R, L))
    gt_s[...] = gt
    ma_s[...] = ma

    def chunk_rows(c):
        return pl.ds(pl.multiple_of(c * N_DIRHEADS, N_DIRHEADS), N_DIRHEADS)

    def stabiliser_scan(i, m):
        rf = chunk_rows(i)
        rb = chunk_rows(NC - 1 - i)
        mpf_s[rf, :] = m
        mpb_s[rb, :] = m
        gt8 = jnp.where(is_fwd8, gt_s[rf, :], gt_s[rb, :])
        ma8 = jnp.where(is_fwd8, ma_s[rf, :], ma_s[rb, :])
        return jnp.maximum(gt8 + m, ma8)

    lax.fori_loop(0, NC, stabiliser_scan, jnp.zeros((N_DIRHEADS, L), F32))

    m_prev = jnp.where(is_fwd, mpf_s[...], mpb_s[...])
    m_new = jnp.maximum(gt + m_prev, ma)
    sp_s[...] = jnp.exp(gt + m_prev - m_new)
    wa_s[...] = jnp.exp(a - m_new)
    r_s[...] = r
    ma_s[...] = m_prev
    u = -jnp.maximum(m_prev, cm)
    stat_vals = list(_split3(u)) + list(_split3(jnp.exp(u - bl)))
    for k, t in enumerate(stat_vals):
        stat_s[k] = t.astype(F32)
    stat_pad = jnp.zeros((L - STAT_GROUPS * N_DIRHEADS, L), F32)

    def prologue(c, carry):
        r0 = pl.multiple_of(c * L, L)
        rows = pl.ds(r0, L)
        rows8 = chunk_rows(c)
        stats = [stat_s[k, rows8, :] for k in range(STAT_GROUPS)] + [stat_pad]
        stats_t = jnp.concatenate(stats, axis=0).T.astype(BF16)
        wa = wa_s[rows8, :]
        r8 = r_s[rows8, :]
        mp8 = ma_s[rows8, :]
        kt_rows = pl.multiple_of(c * MLSTM_W, MLSTM_W)
        for h in range(H):
            hs = slice(h * DH, (h + 1) * DH)
            q = q_ref[rows, hs]
            kt_b = kt_ref[pl.ds(kt_rows + h * DH, DH), :]
            s = _dot(q, kt_b)
            kt = kt_b.astype(F32)
            for d in range(2):
                j = H * d + h
                bc = _dot(stats_t, sel_ref[j])
                u_b = bc[:, :L]
                p = jnp.where(causal if d == 0 else anti, jnp.exp(u_b + r8[j:j + 1, :]), 0.0)
                w_inter = jnp.exp(u_b + mp8[j:j + 1, :])
                lhs_s[d, rows, 2 * h * DH:(2 * h + 1) * DH] = (s * p).astype(BF16)
                lhs_s[d, rows, (2 * h + 1) * DH:(2 * h + 2) * DH] = w_inter.astype(BF16) * q
                e_s[j, rows, :] = bc[:, L:]
                ktw_s[d, c, hs, :] = (kt * wa[j:j + 1, :]).astype(BF16)
        return carry

    lax.fori_loop(0, NC, prologue, 0, unroll=4)

    ones_blk = jnp.ones((L, DH), BF16)
    ng = ng_ref[...]

    def run_direction(d):
        st_s[...] = jnp.zeros_like(st_s)

        def body(i, carry):
            c = i if d == 0 else NC - 1 - i
            r0 = pl.multiple_of(c * L, L)
            rows = pl.ds(r0, L)
            for h in range(H):
                j = H * d + h
                hs = slice(h * DH, (h + 1) * DH)
                lhs = lhs_s[d, rows, 2 * h * DH:(2 * h + 2) * DH]
                vaug = jnp.concatenate([vom_ref[rows, hs], ones_blk], axis=1)
                c_prev = st_s[h]
                tot = _dot(lhs, jnp.concatenate([vaug, c_prev.astype(BF16)], axis=0))
                h_out = tot[:, :DH] / jnp.maximum(jnp.abs(tot[:, DH:]), e_s[j, rows, :])
                sp = sp_s[pl.ds(c * N_DIRHEADS + j, 1), :]
                st_s[h] = jnp.concatenate([sp, sp], axis=1) * c_prev + _dot(ktw_s[d, c, hs, :], vaug)

                if d == 0:
                    hf_s[rows, hs] = h_out
                else:
                    hh = hf_s[rows, hs] + h_out
                    mu = jnp.mean(hh, axis=-1, keepdims=True)
                    hc = hh - mu
                    var = jnp.mean(hc * hc, axis=-1, keepdims=True)
                    hn = hc * lax.rsqrt(var + LN_EPS) * ng[:, hs]
                    o = vom_ref[rows, MLSTM_W + h * DH:MLSTM_W + (h + 1) * DH].astype(F32)
                    hm_ref[rows, hs] = (hn * _sigmoid(o)).astype(BF16)
            return carry

        lax.fori_loop(0, NC, body, 0, unroll=4)

    run_direction(0)
    run_direction(1)


def _mlstm(qm, ktm, vom, gi, gf, T, w):
    n = qm.shape[0]
    B = n // T
    NC = T // MLSTM_CHUNK
    seq = lambda b: (b, 0)
    gate_major = pltpu.VMEM((NC * N_DIRHEADS, MLSTM_CHUNK), F32)
    return pl.pallas_call(
        _mlstm_kernel, grid=(B,),
        in_specs=[
            pl.BlockSpec((T, MLSTM_W), seq),
            pl.BlockSpec((NC * MLSTM_W, MLSTM_CHUNK), seq),
            pl.BlockSpec((T, 2 * MLSTM_W), seq),
            pl.BlockSpec((NC * N_DIRHEADS, MLSTM_CHUNK), seq),
            pl.BlockSpec((NC * N_DIRHEADS, MLSTM_CHUNK), seq),
            _const_spec((N_DIRHEADS, LANES, STAT_BLOCKS * MLSTM_CHUNK)),
            _const_spec((1, MLSTM_W)),
        ],
        out_specs=pl.BlockSpec((T, MLSTM_W), seq),
        out_shape=jax.ShapeDtypeStruct((n, MLSTM_W), BF16),
        scratch_shapes=[
            pltpu.VMEM((2, NC, MLSTM_W, MLSTM_CHUNK), BF16),
            pltpu.VMEM((2, T, 2 * MLSTM_W), BF16),
            pltpu.VMEM((N_DIRHEADS, T, LANES), F32),
            pltpu.VMEM((STAT_GROUPS, NC * N_DIRHEADS, MLSTM_CHUNK), F32),
            gate_major, gate_major, gate_major, gate_major,
            gate_major, gate_major, gate_major,
            pltpu.VMEM((T, MLSTM_W), F32),
            pltpu.VMEM((MLSTM_HEADS, MLSTM_DH, 2 * MLSTM_DH), F32),
        ],
        compiler_params=_params(1), name="mlstm",
    )(qm, ktm, vom, gi, gf, w["mlstm_sel"], w["mlstm_norm_g"])


def _attn_kernel(q_ref, k_ref, v_ref, o_ref, vaug_s):
    @pl.when(pl.program_id(2) == 0)
    def _():
        vaug_s[:, :ATTN_DH] = v_ref[...]
        vaug_s[:, ATTN_DH:] = jnp.ones((v_ref.shape[0], ATTN_DH), BF16)

    k = k_ref[...]
    tq = q_ref.shape[0]
    rows_per_unit = min(tq, ATTN_UNIT_ROWS)
    for g in range(ATTN_GROUP):
        hs = slice(g * ATTN_DH, (g + 1) * ATTN_DH)
        for r0 in range(0, tq, rows_per_unit):
            rs = slice(r0, r0 + rows_per_unit)
            s = _dot_nt(q_ref[rs, hs], k)
            m = jnp.max(s, axis=-1, keepdims=True)
            p = jnp.exp2(s - m)
            oa = _dot(p.astype(BF16), vaug_s[...])
            o_ref[rs, hs] = (oa[:, :ATTN_DH] / oa[:, ATTN_DH:ATTN_DH + 1]).astype(o_ref.dtype)


def _attention(qa, ka, va, T, tq):
    n = qa.shape[0]
    B = n // T
    nq = T // tq
    gw = ATTN_GROUP * ATTN_DH
    return pl.pallas_call(
        _attn_kernel, grid=(B, KV_HEADS, nq),
        in_specs=[
            pl.BlockSpec((tq, gw), lambda b, h, i: (b * nq + i, h)),
            pl.BlockSpec((T, ATTN_DH), lambda b, h, i: (b, h)),
            pl.BlockSpec((T, ATTN_DH), lambda b, h, i: (b, h)),
        ],
        out_specs=pl.BlockSpec((tq, gw), lambda b, h, i: (b * nq + i, h)),
        out_shape=jax.ShapeDtypeStruct((n, ATTN_W), BF16),
        scratch_shapes=[pltpu.VMEM((T, 2 * ATTN_DH), BF16)],
        compiler_params=_params(3), name="gqa_attn",
    )(qa, ka, va)


def _mem_kv_kernel(m_ref, w_ref, o_ref):
    o_ref[...] = _dot(m_ref[...].astype(BF16), w_ref[...]).astype(o_ref.dtype)


def _mem_kv(mem2, w):
    n = mem2.shape[0]
    tm = _tile(n, 1024)
    return pl.pallas_call(
        _mem_kv_kernel, grid=(n // tm,),
        in_specs=[pl.BlockSpec((tm, D_MODEL), lambda b: (b, 0)),
                  _const_spec((D_MODEL, 2 * MEM_W))],
        out_specs=pl.BlockSpec((tm, 2 * MEM_W), lambda b: (b, 0)),
        out_shape=jax.ShapeDtypeStruct((n, 2 * MEM_W), BF16),
        compiler_params=_params(1), name="mem_kv",
    )(mem2, w["w_mem_kv"])


def _merge_kernel(xn_ref, hm_ref, ha_ref, qc_ref, kv_ref, gbr_ref,
                  wbm_ref, wba_ref, wbc_ref, wo_ref, l1g_ref, l1b_ref, x1_ref):
    ones_blk = jnp.ones((kv_ref.shape[0], MEM_DH), BF16)
    hc_parts = []
    for h in range(MEM_HEADS):
        hs = slice(h * MEM_DH, (h + 1) * MEM_DH)
        kc = kv_ref[:, hs]
        vaug = jnp.concatenate([kv_ref[:, MEM_W + h * MEM_DH:MEM_W + (h + 1) * MEM_DH], ones_blk], axis=1)
        s = _dot_nt(qc_ref[:, hs], kc)
        p = jnp.exp2(s - jnp.max(s, axis=-1, keepdims=True))
        oa = _dot(p.astype(BF16), vaug)
        hc_parts.append((oa[:, :MEM_DH] / oa[:, MEM_DH:MEM_DH + 1]).astype(BF16))
    hc = jnp.concatenate(hc_parts, axis=1)

    def gate(j):
        return _sigmoid(gbr_ref[:, j * D_MODEL:(j + 1) * D_MODEL].astype(F32))

    merged = gate(0) * _dot(hm_ref[...], wbm_ref[...])
    merged = merged + gate(1) * _dot(ha_ref[...], wba_ref[...])
    merged = merged + gate(2) * _dot(hc, wbc_ref[...])
    mixed = _dot(merged.astype(BF16), wo_ref[...])
    x1_ref[...] = _layer_norm(DEEPNORM_ALPHA * xn_ref[...] + mixed, l1g_ref[...], l1b_ref[...])


def _merge(xn, hm, ha, qc, kvc, gbr, T, M, w, tm):
    n = xn.shape[0]
    tiles_per_seq = T // tm
    row = lambda i: (i, 0)
    return pl.pallas_call(
        _merge_kernel, grid=(n // tm,),
        in_specs=[
            pl.BlockSpec((tm, D_MODEL), row),
            pl.BlockSpec((tm, MLSTM_W), row),
            pl.BlockSpec((tm, ATTN_W), row),
            pl.BlockSpec((tm, MEM_W), row),
            pl.BlockSpec((M, 2 * MEM_W), lambda i: (i // tiles_per_seq, 0)),
            pl.BlockSpec((tm, N_BRANCH * D_MODEL), row),
            _const_spec((MLSTM_W, D_MODEL)), _const_spec((ATTN_W, D_MODEL)),
            _const_spec((MEM_W, D_MODEL)), _const_spec((D_MODEL, D_MODEL)),
            _const_spec((1, D_MODEL)), _const_spec((1, D_MODEL)),
        ],
        out_specs=pl.BlockSpec((tm, D_MODEL), row),
        out_shape=jax.ShapeDtypeStruct((n, D_MODEL), F32),
        compiler_params=_params(1), name="merge_out",
    )(xn, hm, ha, qc, kvc, gbr, w["w_branch_mlstm"],
      w["w_branch_attn"], w["w_branch_mem"], w["w_out"], w["ln1_g"], w["ln1_b"])


GELU_C = 0.7978845608028654
GELU_C3 = GELU_C * 0.044715


def _gelu_tanh_x2(x):
    return x * (1.0 + jnp.tanh(x * (GELU_C + GELU_C3 * (x * x))))


def _ffn_kernel(xp_ref, x_ref, xn_ref, wup_ref, cw_ref, cb_ref, wdn_ref, l2g_ref, l2b_ref,
                y_ref, xe_s, act_s, *, tiles_per_seq):
    tm = x_ref.shape[0]
    i = pl.program_id(0)
    j = i % tiles_per_seq
    prev = xp_ref[...] * jnp.where(j > 0, 1.0, 0.0).astype(F32)
    nxt = xn_ref[...] * jnp.where(j < tiles_per_seq - 1, 1.0, 0.0).astype(F32)
    xe_s[...] = jnp.concatenate([prev, x_ref[...], nxt], axis=0).astype(BF16)
    te = tm + 2 * SUBLANES

    def conv(u, cw, cb):
        up = pltpu.roll(u, 1, 0)[SUBLANES:SUBLANES + tm, :]
        uc = u[SUBLANES:SUBLANES + tm, :]
        un = pltpu.roll(u, te - 1, 0)[SUBLANES:SUBLANES + tm, :]
        y = up * cw[0:1, :] + cb
        y = y + uc * cw[1:2, :]
        return y + un * cw[2:3, :]

    def up_conv(lo):
        cols = slice(lo, lo + FFN_CHUNK)
        return conv(_dot(xe_s[...], wup_ref[:, cols]), cw_ref[:, cols], cb_ref[:, cols])

    for c in range(0, D_FF, FFN_CHUNK):
        act_s[:, c:c + FFN_CHUNK] = (_gelu_tanh_x2(up_conv(c)) * up_conv(D_FF + c)).astype(BF16)

    ff = _dot(act_s[...], wdn_ref[...])
    y_ref[...] = _layer_norm(DEEPNORM_ALPHA * x_ref[...] + ff, l2g_ref[...], l2b_ref[...])


def _ffn(x1, T, w, tm):
    n = x1.shape[0]
    tiles_per_seq = T // tm
    r8 = tm // SUBLANES
    nblk8 = n // SUBLANES
    row = lambda i: (i, 0)
    return pl.pallas_call(
        functools.partial(_ffn_kernel, tiles_per_seq=tiles_per_seq), grid=(n // tm,),
        in_specs=[
            pl.BlockSpec((SUBLANES, D_MODEL), lambda i: (jnp.maximum(i * r8 - 1, 0), 0)),
            pl.BlockSpec((tm, D_MODEL), row),
            pl.BlockSpec((SUBLANES, D_MODEL), lambda i: (jnp.minimum((i + 1) * r8, nblk8 - 1), 0)),
            _const_spec((D_MODEL, 2 * D_FF)),
            _const_spec((CONV_W, 2 * D_FF)),
            _const_spec((1, 2 * D_FF)),
            _const_spec((D_FF, D_MODEL)),
            _const_spec((1, D_MODEL)), _const_spec((1, D_MODEL)),
        ],
        out_specs=pl.BlockSpec((tm, D_MODEL), row),
        out_shape=jax.ShapeDtypeStruct((n, D_MODEL), F32),
        scratch_shapes=[pltpu.VMEM((tm + 2 * SUBLANES, D_MODEL), BF16),
                        pltpu.VMEM((tm, D_FF), BF16)],
        compiler_params=_params(1), name="conv_ffn",
    )(x1, x1, x1, w["w_ffn_up"], w["ffn_conv_w"], w["ffn_conv_b"], w["w_ffn_down"],
      w["ln2_g"], w["ln2_b"])


def _rope_tables(T):
    rows = T // GRID_W
    row = jnp.repeat(jnp.arange(rows, dtype=F32), GRID_W)
    col = jnp.tile(jnp.arange(GRID_W, dtype=F32), rows)
    inv_freq = ROPE_THETA ** (-jnp.arange(0, ROPE_AXIS_DIM, 2, dtype=F32) / ROPE_AXIS_DIM)
    ang_r = row[:, None] * inv_freq
    ang_c = col[:, None] * inv_freq
    cos = jnp.concatenate([jnp.cos(ang_r), jnp.cos(ang_c)] * 2, axis=-1)
    sin = jnp.concatenate([-jnp.sin(ang_r), -jnp.sin(ang_c), jnp.sin(ang_r), jnp.sin(ang_c)], axis=-1)
    return cos, sin


def _rope_head_perm():
    p = jnp.arange(ATTN_DH)
    quarter = ROPE_AXIS_DIM // 2
    half, axis, j = p // ROPE_AXIS_DIM, (p % ROPE_AXIS_DIM) // quarter, p % quarter
    return axis * ROPE_AXIS_DIM + half * quarter + j


def _gained_tables(cos, sin, g, scale):
    g = g.astype(F32)[_rope_head_perm()]
    return cos * (g * scale), sin * (jnp.roll(g, ATTN_DH // 2) * scale)


def _prep_weights(l, T, ln_in_g, ln_in_b, w_in, mlstm_gate_bias, mlstm_conv_w, mlstm_conv_b,
                  mlstm_norm_g, attn_q_norm_g, attn_k_norm_g, w_mem_kv, w_branch_mlstm,
                  w_branch_attn, w_branch_mem, w_out, ln1_g, ln1_b, w_ffn_up, ffn_conv_w,
                  ffn_conv_b, w_ffn_down, ln2_g, ln2_b):
    wi = w_in[l]
    g0 = 4 * MLSTM_W
    g1 = g0 + N_GATES
    w_gate = wi[:, g0:g1]
    qk0 = g1
    qk1 = qk0 + ATTN_W + KV_W
    qc0 = qk1 + KV_W
    qc1 = qc0 + MEM_W
    quarter = ROPE_AXIS_DIM // 2
    w_qk = wi[:, qk0:qk1].reshape(D_MODEL, ATTN_HEADS + KV_HEADS, 2, 2, quarter)
    w_qk = jnp.swapaxes(w_qk, 2, 3).reshape(D_MODEL, ATTN_W + KV_W)
    w_qc = wi[:, qc0:qc1] * (MEM_DH ** -0.5 * LOG2_E)
    w_cat = jnp.concatenate([wi[:, :g0], w_qk, wi[:, qk1:qc0], w_qc, wi[:, qc1:]], axis=1).astype(BF16)
    gate_rows = jnp.array(GATE_ROW_ORDER)
    w_gate = w_gate[:, gate_rows]
    gb = mlstm_gate_bias[l].astype(F32)[gate_rows]
    cos, sin = _rope_tables(T)
    cos_q, sin_q = _gained_tables(cos, sin, attn_q_norm_g[l], ATTN_DH ** -0.5 * LOG2_E)
    cos_k, sin_k = _gained_tables(cos, sin, attn_k_norm_g[l], 1.0)
    r2 = lambda a: a.reshape(1, -1).astype(F32)
    ffn_half = jnp.where(jnp.arange(2 * D_FF) >= D_FF, 0.5, 1.0).astype(F32)[None, :]
    return {
        "ln_in_g": r2(ln_in_g), "ln_in_b": r2(ln_in_b),
        "w_cat": w_cat, "w_gt": w_gate.T.astype(BF16),
        "gb_row": jnp.broadcast_to(gb[:, None], (N_GATES, LANES)),
        "mlstm_sel": _mlstm_select(),
        "rope_cos_q": cos_q, "rope_sin_q": sin_q, "rope_cos_k": cos_k, "rope_sin_k": sin_k,
        "mlstm_conv_w": mlstm_conv_w[l].astype(F32), "mlstm_conv_b": r2(mlstm_conv_b[l]),
        "mlstm_norm_g": r2(mlstm_norm_g[l]),
        "w_mem_kv": w_mem_kv[l].astype(BF16),
        "w_branch_mlstm": w_branch_mlstm[l].astype(BF16),
        "w_branch_attn": w_branch_attn[l].astype(BF16),
        "w_branch_mem": w_branch_mem[l].astype(BF16),
        "w_out": w_out[l].astype(BF16),
        "ln1_g": r2(ln1_g[l]), "ln1_b": r2(ln1_b[l]),
        "w_ffn_up": w_ffn_up[l].astype(BF16),
        "ffn_conv_w": ffn_conv_w[l].astype(F32) * ffn_half,
        "ffn_conv_b": r2(ffn_conv_b[l]) * ffn_half,
        "w_ffn_down": w_ffn_down[l].astype(BF16),
        "ln2_g": r2(ln2_g[l]), "ln2_b": r2(ln2_b[l]),
    }


def _tile(T, want):
    t = min(want, T)
    assert T % t == 0
    return t


def _trunk(x, mem, w):
    B, T, _ = x.shape
    M = mem.shape[1]
    assert T % MLSTM_CHUNK == 0 and T % GRID_W == 0
    x2 = x.reshape(B * T, D_MODEL)
    mem2 = mem.reshape(B * M, D_MODEL)
    xn, qm, ktm, vom, qa, ka, va, qc, gbr, gi, gf = _in_proj(x2, T, w, _tile(T, 512))
    hm = _mlstm(qm, ktm, vom, gi, gf, T, w)
    ha = _attention(qa, ka, va, T, _tile(T, 2048))
    kvc = _mem_kv(mem2, w)
    x1 = _merge(xn, hm, ha, qc, kvc, gbr, T, M, w, _tile(T, 512))
    y = _ffn(x1, T, w, _tile(T, 512))
    return y.reshape(B, T, D_MODEL)


def kernel(x_prompt, x_sample, mem_prompt, mem_sample, ln_in_g, ln_in_b, w_in, mlstm_gate_bias, mlstm_conv_w, mlstm_conv_b, mlstm_norm_g, attn_q_norm_g, attn_k_norm_g, w_mem_kv, w_branch_mlstm, w_branch_attn, w_branch_mem, w_out, ln1_g, ln1_b, w_ffn_up, ffn_conv_w, ffn_conv_b, w_ffn_down, ln2_g, ln2_b):
    assert DEPTH == 1 and x_prompt.shape[1] == x_sample.shape[1]
    w = _prep_weights(0, x_prompt.shape[1], ln_in_g, ln_in_b, w_in, mlstm_gate_bias, mlstm_conv_w,
                      mlstm_conv_b, mlstm_norm_g, attn_q_norm_g, attn_k_norm_g, w_mem_kv,
                      w_branch_mlstm, w_branch_attn, w_branch_mem, w_out, ln1_g, ln1_b, w_ffn_up,
                      ffn_conv_w, ffn_conv_b, w_ffn_down, ln2_g, ln2_b)
    return (_trunk(x_prompt, mem_prompt, w), _trunk(x_sample, mem_sample, w))
```

```python
import functools

import jax
import jax.numpy as jnp
from jax import lax
from jax.experimental import pallas as pl
from jax.experimental.pallas import tpu as pltpu

F32 = jnp.float32
BF16 = jnp.bfloat16

D_MODEL = 1024
DEPTH = 1
GRID_W = 64
MLSTM_HEADS = 4
MLSTM_DH = 128
MLSTM_W = MLSTM_HEADS * MLSTM_DH
MLSTM_CHUNK = 128
N_GATES = 4 * MLSTM_HEADS
ATTN_DH = 128
ATTN_HEADS = 8
KV_HEADS = 2
ATTN_GROUP = ATTN_HEADS // KV_HEADS
ATTN_W = ATTN_HEADS * ATTN_DH
KV_W = KV_HEADS * ATTN_DH
ROPE_AXIS_DIM = ATTN_DH // 2
ROPE_THETA = 10000.0
MEM_HEADS = 4
MEM_DH = 128
MEM_W = MEM_HEADS * MEM_DH
N_BRANCH = 3
D_FF = ((8 * D_MODEL // 3 + 127) // 128) * 128
CONV_W = 3
DEEPNORM_ALPHA = (2.0 * DEPTH) ** 0.25
LN_EPS = 1e-5
LOG2_E = 1.4426950408889634

LANES = 128
SUBLANES = 8
VMEM_LIMIT_BYTES = 56 * 1024 * 1024

SEG_QKM = (0, 2 * MLSTM_W)
SEG_VOM = (SEG_QKM[1], SEG_QKM[1] + 2 * MLSTM_W)
SEG_QA = (SEG_VOM[1], SEG_VOM[1] + ATTN_W)
SEG_KA = (SEG_QA[1], SEG_QA[1] + KV_W)
SEG_VA = (SEG_KA[1], SEG_KA[1] + KV_W)
SEG_QC = (SEG_VA[1], SEG_VA[1] + MEM_W)
SEG_GBR = (SEG_QC[1], SEG_QC[1] + N_BRANCH * D_MODEL)
W_CAT = SEG_GBR[1]
GATE_ROW_ORDER = tuple(g * MLSTM_HEADS + h for g in (0, 2, 1, 3) for h in range(MLSTM_HEADS))

FFN_CHUNK = 256
ATTN_UNIT_ROWS = 128


def _params(n_axes):
    return pltpu.CompilerParams(dimension_semantics=("arbitrary",) * n_axes,
                                vmem_limit_bytes=VMEM_LIMIT_BYTES)


def _const_spec(shape):
    nd = len(shape)
    return pl.BlockSpec(shape, lambda *_: (0,) * nd, pipeline_mode=pl.Buffered(1))


def _layer_norm(x, g, b):
    mu = jnp.mean(x, axis=-1, keepdims=True)
    xc = x - mu
    var = jnp.mean(xc * xc, axis=-1, keepdims=True)
    return xc * lax.rsqrt(var + LN_EPS) * g + b


def _sigmoid(x):
    return 1.0 / (1.0 + jnp.exp(-x))


def _log_sigmoid(x):
    return jnp.minimum(x, 0.0) - jnp.log(1.0 + jnp.exp(-jnp.abs(x)))


def _dot(a, b):
    return jnp.dot(a, b, preferred_element_type=F32)


def _dot_nt(a, b):
    return lax.dot_general(a, b, (((1,), (1,)), ((), ())), preferred_element_type=F32)


def _split3(x):
    x1 = x.astype(BF16)
    r1 = x - x1.astype(F32)
    x2 = r1.astype(BF16)
    r2 = r1 - x2.astype(F32)
    return x1, x2, r2.astype(BF16)


def _rms_rope(xh, cos_g, sin_g):
    ms = jnp.mean(xh * xh, axis=-1, keepdims=True)
    xh = xh * lax.rsqrt(ms + LN_EPS)
    return xh * cos_g + pltpu.roll(xh, LANES // 2, 1) * sin_g


def _proj_kernel(xp_ref, x_ref, xn_ref, lng_ref, lnb_ref, w_ref, wgt_ref, gbr_row_ref,
                 cq_ref, sq_ref, ck_ref, sk_ref, cw_ref, cb_ref,
                 xno_ref, qm_ref, ktm_ref, vom_ref, qa_ref, ka_ref, va_ref, qc_ref, gbr_ref, gi_ref, gf_ref,
                 *, tiles_per_seq):
    tm = x_ref.shape[0]
    lng = lng_ref[...]
    lnb = lnb_ref[...]
    xn = _layer_norm(x_ref[...], lng, lnb)
    xno_ref[...] = xn
    xb = xn.astype(BF16)

    def proj(lo, hi):
        return _dot(xb, w_ref[:, lo:hi])

    j = pl.program_id(0) % tiles_per_seq
    hr = xp_ref.shape[0]
    te = tm + 2 * hr
    xbe = jnp.concatenate([_layer_norm(xp_ref[...], lng, lnb).astype(BF16), xb,
                           _layer_norm(xn_ref[...], lng, lnb).astype(BF16)], axis=0)
    keep_prev = jnp.where(j > 0, 1.0, 0.0).astype(F32)
    keep_next = jnp.where(j < tiles_per_seq - 1, 1.0, 0.0).astype(F32)

    def conv_silu(lo, scale):
        cols = slice(lo, lo + MLSTM_W)
        ue = _dot(xbe, w_ref[:, SEG_QKM[0] + lo:SEG_QKM[0] + lo + MLSTM_W])
        ue = jnp.concatenate([ue[:hr] * keep_prev, ue[hr:hr + tm], ue[hr + tm:] * keep_next], axis=0)
        u_prev = pltpu.roll(ue, 1, 0)[hr:hr + tm]
        u_next = pltpu.roll(ue, te - 1, 0)[hr:hr + tm]
        y = u_prev * cw_ref[0:1, cols] + cb_ref[:, cols]
        y = y + ue[hr:hr + tm] * cw_ref[1:2, cols]
        y = y + u_next * cw_ref[2:3, cols]
        y = y * _sigmoid(y)
        return y if scale == 1.0 else y * scale

    def q_job():
        qm_ref[...] = conv_silu(0, 1.0).astype(BF16)

    def k_job():
        yk = conv_silu(MLSTM_W, MLSTM_DH ** -0.5)
        for a in range(tm // MLSTM_CHUNK):
            for h in range(MLSTM_HEADS):
                blk = yk[a * MLSTM_CHUNK:(a + 1) * MLSTM_CHUNK, h * MLSTM_DH:(h + 1) * MLSTM_DH]
                r0 = a * MLSTM_W + h * MLSTM_DH
                ktm_ref[r0:r0 + MLSTM_DH, :] = blk.T.astype(BF16)

    def plain_job(out_ref, dst, src, piece):
        def job():
            out_ref[:, dst:dst + piece] = proj(src, src + piece).astype(out_ref.dtype)
        return job

    def rope_job(out_ref, dst, src, c_ref, s_ref):
        def job():
            blk = proj(src, src + pair)
            for jj in range(2):
                xh = _rms_rope(blk[:, jj * ATTN_DH:(jj + 1) * ATTN_DH], c_ref[...], s_ref[...])
                out_ref[:, dst + jj * ATTN_DH:dst + (jj + 1) * ATTN_DH] = xh.astype(BF16)
        return job

    pair = 2 * ATTN_DH
    vector_jobs = [q_job, k_job]
    vector_jobs += [rope_job(qa_ref, c - SEG_QA[0], c, cq_ref, sq_ref) for c in range(SEG_QA[0], SEG_QA[1], pair)]
    vector_jobs += [rope_job(ka_ref, c - SEG_KA[0], c, ck_ref, sk_ref) for c in range(SEG_KA[0], SEG_KA[1], pair)]
    plain_jobs = []
    for out_ref, seg, piece in ((vom_ref, SEG_VOM, 512), (va_ref, SEG_VA, KV_W),
                                (qc_ref, SEG_QC, 512), (gbr_ref, SEG_GBR, 512)):
        plain_jobs += [plain_job(out_ref, c - seg[0], c, piece) for c in range(seg[0], seg[1], piece)]

    while vector_jobs or plain_jobs:
        if plain_jobs:
            plain_jobs.pop(0)()
        if vector_jobs:
            vector_jobs.pop(0)()

    grow = _dot_nt(wgt_ref[...], xb)
    nd = N_GATES // 2
    for j in range(tm // LANES):
        blk = grow[:, j * LANES:(j + 1) * LANES] + gbr_row_ref[...]
        gi_ref[j * nd:(j + 1) * nd, :] = blk[:nd]
        gf_ref[j * nd:(j + 1) * nd, :] = blk[nd:]


def _in_proj(x2, T, w, tm):
    n = x2.shape[0]
    grid = (n // tm,)
    tiles_per_seq = T // tm
    row = lambda i: (i, 0)
    hb = 2 * SUBLANES
    nhb = n // hb
    out_shapes = (
        jax.ShapeDtypeStruct((n, D_MODEL), F32),
        jax.ShapeDtypeStruct((n, MLSTM_W), BF16),
        jax.ShapeDtypeStruct((n // MLSTM_CHUNK * MLSTM_W, MLSTM_CHUNK), BF16),
        jax.ShapeDtypeStruct((n, 2 * MLSTM_W), BF16),
        jax.ShapeDtypeStruct((n, ATTN_W), BF16),
        jax.ShapeDtypeStruct((n, KV_W), BF16),
        jax.ShapeDtypeStruct((n, KV_W), BF16),
        jax.ShapeDtypeStruct((n, MEM_W), BF16),
        jax.ShapeDtypeStruct((n, N_BRANCH * D_MODEL), BF16),
        jax.ShapeDtypeStruct((n // LANES * (N_GATES // 2), LANES), F32),
        jax.ShapeDtypeStruct((n // LANES * (N_GATES // 2), LANES), F32),
    )
    out_specs = (
        pl.BlockSpec((tm, D_MODEL), row),
        pl.BlockSpec((tm, MLSTM_W), row),
        pl.BlockSpec((tm // MLSTM_CHUNK * MLSTM_W, MLSTM_CHUNK), row),
        pl.BlockSpec((tm, 2 * MLSTM_W), row),
        pl.BlockSpec((tm, ATTN_W), row),
        pl.BlockSpec((tm, KV_W), row),
        pl.BlockSpec((tm, KV_W), row),
        pl.BlockSpec((tm, MEM_W), row),
        pl.BlockSpec((tm, N_BRANCH * D_MODEL), row),
        pl.BlockSpec((tm // LANES * (N_GATES // 2), LANES), row),
        pl.BlockSpec((tm // LANES * (N_GATES // 2), LANES), row),
    )
    in_specs = [
        pl.BlockSpec((hb, D_MODEL), lambda i: (jnp.maximum(i * (tm // hb) - 1, 0), 0)),
        pl.BlockSpec((tm, D_MODEL), row),
        pl.BlockSpec((hb, D_MODEL), lambda i: (jnp.minimum((i + 1) * (tm // hb), nhb - 1), 0)),
        _const_spec((1, D_MODEL)), _const_spec((1, D_MODEL)),
        _const_spec((D_MODEL, W_CAT)), _const_spec((N_GATES, D_MODEL)),
        _const_spec((N_GATES, LANES)),
    ] + [pl.BlockSpec((tm, LANES), lambda i: (i % tiles_per_seq, 0))] * 4 + [
        _const_spec((CONV_W, 2 * MLSTM_W)), _const_spec((1, 2 * MLSTM_W)),
    ]
    return pl.pallas_call(
        functools.partial(_proj_kernel, tiles_per_seq=tiles_per_seq),
        grid=grid, in_specs=in_specs, out_specs=out_specs, out_shape=out_shapes,
        compiler_params=_params(1), name="in_proj",
    )(x2, x2, x2, w["ln_in_g"], w["ln_in_b"], w["w_cat"], w["w_gt"], w["gb_row"],
      w["rope_cos_q"], w["rope_sin_q"], w["rope_cos_k"], w["rope_sin_k"],
      w["mlstm_conv_w"], w["mlstm_conv_b"])


N_DIRHEADS = 2 * MLSTM_HEADS
STAT_GROUPS = 6
STAT_BLOCK = (0, 0, 0, 1, 1, 1)
STAT_BLOCKS = 2


def _mlstm_select():
    lane = jnp.arange(LANES)
    grp, j = lane // N_DIRHEADS, lane % N_DIRHEADS
    blk = jnp.array(STAT_BLOCK + (-1,) * (LANES // N_DIRHEADS - STAT_GROUPS))[grp]
    out_blk = jnp.arange(STAT_BLOCKS * LANES) // LANES
    sel = (j[None, :, None] == jnp.arange(N_DIRHEADS)[:, None, None]) & (blk[None, :, None] == out_blk[None, None, :])
    return sel.astype(BF16)


def _mlstm_kernel(q_ref, kt_ref, vom_ref, gi_ref, gf_ref, sel_ref, ng_ref, hm_ref,
                  ktw_s, lhs_s, e_s, stat_s, gt_s, ma_s, mpf_s, mpb_s, r_s, wa_s, sp_s,
                  hf_s, st_s):
    T = q_ref.shape[0]
    L = MLSTM_CHUNK
    NC = T // L
    DH = MLSTM_DH
    H = MLSTM_HEADS

    ri = lax.broadcasted_iota(jnp.int32, (L, L), 0)
    ci = lax.broadcasted_iota(jnp.int32, (L, L), 1)
    causal = ci <= ri
    anti = ci >= ri
    tril = jnp.where(causal, 1.0, 0.0).astype(BF16)
    triu = jnp.where(anti, 1.0, 0.0).astype(BF16)
    ones_ll = jnp.ones((L, L), BF16)

    R = NC * N_DIRHEADS
    row_r = lax.broadcasted_iota(jnp.int32, (R, L), 0)
    lane_r = lax.broadcasted_iota(jnp.int32, (R, L), 1)
    is_fwd = (row_r % N_DIRHEADS) < H
    is_fwd8 = lax.broadcasted_iota(jnp.int32, (N_DIRHEADS, L), 0) < H

    li = gi_ref[...]
    y1, y2, y3 = _split3(_log_sigmoid(gf_ref[...]))

    def lane_sums(m):
        return _dot(y1, m) + _dot(y2, m) + _dot(y3, m)

    bl = jnp.where(is_fwd, lane_sums(triu), lane_sums(tril))
    gt = lane_sums(ones_ll)
    r = li - bl
    pm = r
    sm = r
    for sh in (1, 2, 4, 8, 16, 32, 64):
        pm = jnp.maximum(pm, jnp.where(lane_r >= sh, pltpu.roll(pm, sh, 1), -jnp.inf))
        sm = jnp.maximum(sm, jnp.where(lane_r < L - sh, pltpu.roll(sm, L - sh, 1), -jnp.inf))
    cm = jnp.where(is_fwd, pm, sm)
    a = gt - bl + li
    ma = jnp.broadcast_to(jnp.max(a, axis=-1, keepdims=True), (R, L))
    gt_s[...] = gt
    ma_s[...] = ma

    def chunk_rows(c):
        return pl.ds(pl.multiple_of(c * N_DIRHEADS, N_DIRHEADS), N_DIRHEADS)

    def stabiliser_scan(i, m):
        rf = chunk_rows(i)
        rb = chunk_rows(NC - 1 - i)
        mpf_s[rf, :] = m
        mpb_s[rb, :] = m
        gt8 = jnp.where(is_fwd8, gt_s[rf, :], gt_s[rb, :])
        ma8 = jnp.where(is_fwd8, ma_s[rf, :], ma_s[rb, :])
        return jnp.maximum(gt8 + m, ma8)

    lax.fori_loop(0, NC, stabiliser_scan, jnp.zeros((N_DIRHEADS, L), F32))

    m_prev = jnp.where(is_fwd, mpf_s[...], mpb_s[...])
    m_new = jnp.maximum(gt + m_prev, ma)
    sp_s[...] = jnp.exp(gt + m_prev - m_new)
    wa_s[...] = jnp.exp(a - m_new)
    r_s[...] = r
    ma_s[...] = m_prev
    u = -jnp.maximum(m_prev, cm)
    stat_vals = list(_split3(u)) + list(_split3(jnp.exp(u - bl)))
    for k, t in enumerate(stat_vals):
        stat_s[k] = t.astype(F32)
    stat_pad = jnp.zeros((L - STAT_GROUPS * N_DIRHEADS, L), F32)

    def prologue(c, carry):
        r0 = pl.multiple_of(c * L, L)
        rows = pl.ds(r0, L)
        rows8 = chunk_rows(c)
        stats = [stat_s[k, rows8, :] for k in range(STAT_GROUPS)] + [stat_pad]
        stats_t = jnp.concatenate(stats, axis=0).T.astype(BF16)
        wa = wa_s[rows8, :]
        r8 = r_s[rows8, :]
        mp8 = ma_s[rows8, :]
        kt_rows = pl.multiple_of(c * MLSTM_W, MLSTM_W)
        for h in range(H):
            hs = slice(h * DH, (h + 1) * DH)
            q = q_ref[rows, hs]
            kt_b = kt_ref[pl.ds(kt_rows + h * DH, DH), :]
            s = _dot(q, kt_b)
            kt = kt_b.astype(F32)
            for d in range(2):
                j = H * d + h
                bc = _dot(stats_t, sel_ref[j])
                u_b = bc[:, :L]
                p = jnp.where(causal if d == 0 else anti, jnp.exp(u_b + r8[j:j + 1, :]), 0.0)
                w_inter = jnp.exp(u_b + mp8[j:j + 1, :])
                lhs_s[d, rows, 2 * h * DH:(2 * h + 1) * DH] = (s * p).astype(BF16)
                lhs_s[d, rows, (2 * h + 1) * DH:(2 * h + 2) * DH] = w_inter.astype(BF16) * q
                e_s[j, rows, :] = bc[:, L:]
                ktw_s[d, c, hs, :] = (kt * wa[j:j + 1, :]).astype(BF16)
        return carry

    lax.fori_loop(0, NC, prologue, 0, unroll=4)

    ones_blk = jnp.ones((L, DH), BF16)
    ng = ng_ref[...]

    def run_direction(d):
        st_s[...] = jnp.zeros_like(st_s)

        def body(i, carry):
            c = i if d == 0 else NC - 1 - i
            r0 = pl.multiple_of(c * L, L)
            rows = pl.ds(r0, L)
            for h in range(H):
                j = H * d + h
                hs = slice(h * DH, (h + 1) * DH)
                lhs = lhs_s[d, rows, 2 * h * DH:(2 * h + 2) * DH]
                vaug = jnp.concatenate([vom_ref[rows, hs], ones_blk], axis=1)
                c_prev = st_s[h]
                tot = _dot(lhs, jnp.concatenate([vaug, c_prev.astype(BF16)], axis=0))
                h_out = tot[:, :DH] / jnp.maximum(jnp.abs(tot[:, DH:]), e_s[j, rows, :])
                sp = sp_s[pl.ds(c * N_DIRHEADS + j, 1), :]
                st_s[h] = jnp.concatenate([sp, sp], axis=1) * c_prev + _dot(ktw_s[d, c, hs, :], vaug)

                if d == 0:
                    hf_s[rows, hs] = h_out
                else:
                    hh = hf_s[rows, hs] + h_out
                    mu = jnp.mean(hh, axis=-1, keepdims=True)
                    hc = hh - mu
                    var = jnp.mean(hc * hc, axis=-1, keepdims=True)
                    hn = hc * lax.rsqrt(var + LN_EPS) * ng[:, hs]
                    o = vom_ref[rows, MLSTM_W + h * DH:MLSTM_W + (h + 1) * DH].astype(F32)
                    hm_ref[rows, hs] = (hn * _sigmoid(o)).astype(BF16)
            return carry

        lax.fori_loop(0, NC, body, 0, unroll=4)

    run_direction(0)
    run_direction(1)


def _mlstm(qm, ktm, vom, gi, gf, T, w):
    n = qm.shape[0]
    B = n // T
    NC = T // MLSTM_CHUNK
    seq = lambda b: (b, 0)
    gate_major = pltpu.VMEM((NC * N_DIRHEADS, MLSTM_CHUNK), F32)
    return pl.pallas_call(
        _mlstm_kernel, grid=(B,),
        in_specs=[
            pl.BlockSpec((T, MLSTM_W), seq),
            pl.BlockSpec((NC * MLSTM_W, MLSTM_CHUNK), seq),
            pl.BlockSpec((T, 2 * MLSTM_W), seq),
            pl.BlockSpec((NC * N_DIRHEADS, MLSTM_CHUNK), seq),
            pl.BlockSpec((NC * N_DIRHEADS, MLSTM_CHUNK), seq),
            _const_spec((N_DIRHEADS, LANES, STAT_BLOCKS * MLSTM_CHUNK)),
            _const_spec((1, MLSTM_W)),
        ],
        out_specs=pl.BlockSpec((T, MLSTM_W), seq),
        out_shape=jax.ShapeDtypeStruct((n, MLSTM_W), BF16),
        scratch_shapes=[
            pltpu.VMEM((2, NC, MLSTM_W, MLSTM_CHUNK), BF16),
            pltpu.VMEM((2, T, 2 * MLSTM_W), BF16),
            pltpu.VMEM((N_DIRHEADS, T, LANES), F32),
            pltpu.VMEM((STAT_GROUPS, NC * N_DIRHEADS, MLSTM_CHUNK), F32),
            gate_major, gate_major, gate_major, gate_major,
            gate_major, gate_major, gate_major,
            pltpu.VMEM((T, MLSTM_W), F32),
            pltpu.VMEM((MLSTM_HEADS, MLSTM_DH, 2 * MLSTM_DH), F32),
        ],
        compiler_params=_params(1), name="mlstm",
    )(qm, ktm, vom, gi, gf, w["mlstm_sel"], w["mlstm_norm_g"])


def _attn_kernel(q_ref, k_ref, v_ref, o_ref, vaug_s):
    @pl.when(pl.program_id(2) == 0)
    def _():
        vaug_s[:, :ATTN_DH] = v_ref[...]
        vaug_s[:, ATTN_DH:] = jnp.ones((v_ref.shape[0], ATTN_DH), BF16)

    k = k_ref[...]
    tq = q_ref.shape[0]
    rows_per_unit = min(tq, ATTN_UNIT_ROWS)
    for g in range(ATTN_GROUP):
        hs = slice(g * ATTN_DH, (g + 1) * ATTN_DH)
        for r0 in range(0, tq, rows_per_unit):
            rs = slice(r0, r0 + rows_per_unit)
            s = _dot_nt(q_ref[rs, hs], k)
            m = jnp.max(s, axis=-1, keepdims=True)
            p = jnp.exp2(s - m)
            oa = _dot(p.astype(BF16), vaug_s[...])
            o_ref[rs, hs] = (oa[:, :ATTN_DH] / oa[:, ATTN_DH:ATTN_DH + 1]).astype(o_ref.dtype)


def _attention(qa, ka, va, T, tq):
    n = qa.shape[0]
    B = n // T
    nq = T // tq
    gw = ATTN_GROUP * ATTN_DH
    return pl.pallas_call(
        _attn_kernel, grid=(B, KV_HEADS, nq),
        in_specs=[
            pl.BlockSpec((tq, gw), lambda b, h, i: (b * nq + i, h)),
            pl.BlockSpec((T, ATTN_DH), lambda b, h, i: (b, h)),
            pl.BlockSpec((T, ATTN_DH), lambda b, h, i: (b, h)),
        ],
        out_specs=pl.BlockSpec((tq, gw), lambda b, h, i: (b * nq + i, h)),
        out_shape=jax.ShapeDtypeStruct((n, ATTN_W), BF16),
        scratch_shapes=[pltpu.VMEM((T, 2 * ATTN_DH), BF16)],
        compiler_params=_params(3), name="gqa_attn",
    )(qa, ka, va)


def _mem_kv_kernel(m_ref, w_ref, o_ref):
    o_ref[...] = _dot(m_ref[...].astype(BF16), w_ref[...]).astype(o_ref.dtype)


def _mem_kv(mem2, w):
    n = mem2.shape[0]
    tm = _tile(n, 1024)
    return pl.pallas_call(
        _mem_kv_kernel, grid=(n // tm,),
        in_specs=[pl.BlockSpec((tm, D_MODEL), lambda b: (b, 0)),
                  _const_spec((D_MODEL, 2 * MEM_W))],
        out_specs=pl.BlockSpec((tm, 2 * MEM_W), lambda b: (b, 0)),
        out_shape=jax.ShapeDtypeStruct((n, 2 * MEM_W), BF16),
        compiler_params=_params(1), name="mem_kv",
    )(mem2, w["w_mem_kv"])


def _merge_kernel(xn_ref, hm_ref, ha_ref, qc_ref, kv_ref, gbr_ref,
                  wbm_ref, wba_ref, wbc_ref, wo_ref, l1g_ref, l1b_ref, x1_ref):
    ones_blk = jnp.ones((kv_ref.shape[0], MEM_DH), BF16)
    hc_parts = []
    for h in range(MEM_HEADS):
        hs = slice(h * MEM_DH, (h + 1) * MEM_DH)
        kc = kv_ref[:, hs]
        vaug = jnp.concatenate([kv_ref[:, MEM_W + h * MEM_DH:MEM_W + (h + 1) * MEM_DH], ones_blk], axis=1)
        s = _dot_nt(qc_ref[:, hs], kc)
        p = jnp.exp2(s - jnp.max(s, axis=-1, keepdims=True))
        oa = _dot(p.astype(BF16), vaug)
        hc_parts.append((oa[:, :MEM_DH] / oa[:, MEM_DH:MEM_DH + 1]).astype(BF16))
    hc = jnp.concatenate(hc_parts, axis=1)

    def gate(j):
        return _sigmoid(gbr_ref[:, j * D_MODEL:(j + 1) * D_MODEL].astype(F32))

    merged = gate(0) * _dot(hm_ref[...], wbm_ref[...])
    merged = merged + gate(1) * _dot(ha_ref[...], wba_ref[...])
    merged = merged + gate(2) * _dot(hc, wbc_ref[...])
    mixed = _dot(merged.astype(BF16), wo_ref[...])
    x1_ref[...] = _layer_norm(DEEPNORM_ALPHA * xn_ref[...] + mixed, l1g_ref[...], l1b_ref[...])


def _merge(xn, hm, ha, qc, kvc, gbr, T, M, w, tm):
    n = xn.shape[0]
    tiles_per_seq = T // tm
    row = lambda i: (i, 0)
    return pl.pallas_call(
        _merge_kernel, grid=(n // tm,),
        in_specs=[
            pl.BlockSpec((tm, D_MODEL), row),
            pl.BlockSpec((tm, MLSTM_W), row),
            pl.BlockSpec((tm, ATTN_W), row),
            pl.BlockSpec((tm, MEM_W), row),
            pl.BlockSpec((M, 2 * MEM_W), lambda i: (i // tiles_per_seq, 0)),
            pl.BlockSpec((tm, N_BRANCH * D_MODEL), row),
            _const_spec((MLSTM_W, D_MODEL)), _const_spec((ATTN_W, D_MODEL)),
            _const_spec((MEM_W, D_MODEL)), _const_spec((D_MODEL, D_MODEL)),
            _const_spec((1, D_MODEL)), _const_spec((1, D_MODEL)),
        ],
        out_specs=pl.BlockSpec((tm, D_MODEL), row),
        out_shape=jax.ShapeDtypeStruct((n, D_MODEL), F32),
        compiler_params=_params(1), name="merge_out",
    )(xn, hm, ha, qc, kvc, gbr, w["w_branch_mlstm"],
      w["w_branch_attn"], w["w_branch_mem"], w["w_out"], w["ln1_g"], w["ln1_b"])


GELU_C = 0.7978845608028654
GELU_C3 = GELU_C * 0.044715


def _gelu_tanh_x2(x):
    return x * (1.0 + jnp.tanh(x * (GELU_C + GELU_C3 * (x * x))))


def _ffn_kernel(xp_ref, x_ref, xn_ref, wup_ref, cw_ref, cb_ref, wdn_ref, l2g_ref, l2b_ref,
                y_ref, xe_s, act_s, *, tiles_per_seq):
    tm = x_ref.shape[0]
    i = pl.program_id(0)
    j = i % tiles_per_seq
    prev = xp_ref[...] * jnp.where(j > 0, 1.0, 0.0).astype(F32)
    nxt = xn_ref[...] * jnp.where(j < tiles_per_seq - 1, 1.0, 0.0).astype(F32)
    xe_s[...] = jnp.concatenate([prev, x_ref[...], nxt], axis=0).astype(BF16)
    te = tm + 2 * SUBLANES

    def conv(u, cw, cb):
        up = pltpu.roll(u, 1, 0)[SUBLANES:SUBLANES + tm, :]
        uc = u[SUBLANES:SUBLANES + tm, :]
        un = pltpu.roll(u, te - 1, 0)[SUBLANES:SUBLANES + tm, :]
        y = up * cw[0:1, :] + cb
        y = y + uc * cw[1:2, :]
        return y + un * cw[2:3, :]

    def up_conv(lo):
        cols = slice(lo, lo + FFN_CHUNK)
        return conv(_dot(xe_s[...], wup_ref[:, cols]), cw_ref[:, cols], cb_ref[:, cols])

    for c in range(0, D_FF, FFN_CHUNK):
        act_s[:, c:c + FFN_CHUNK] = (_gelu_tanh_x2(up_conv(c)) * up_conv(D_FF + c)).astype(BF16)

    ff = _dot(act_s[...], wdn_ref[...])
    y_ref[...] = _layer_norm(DEEPNORM_ALPHA * x_ref[...] + ff, l2g_ref[...], l2b_ref[...])


def _ffn(x1, T, w, tm):
    n = x1.shape[0]
    tiles_per_seq = T // tm
    r8 = tm // SUBLANES
    nblk8 = n // SUBLANES
    row = lambda i: (i, 0)
    return pl.pallas_call(
        functools.partial(_ffn_kernel, tiles_per_seq=tiles_per_seq), grid=(n // tm,),
        in_specs=[
            pl.BlockSpec((SUBLANES, D_MODEL), lambda i: (jnp.maximum(i * r8 - 1, 0), 0)),
            pl.BlockSpec((tm, D_MODEL), row),
            pl.BlockSpec((SUBLANES, D_MODEL), lambda i: (jnp.minimum((i + 1) * r8, nblk8 - 1), 0)),
            _const_spec((D_MODEL, 2 * D_FF)),
            _const_spec((CONV_W, 2 * D_FF)),
            _const_spec((1, 2 * D_FF)),
            _const_spec((D_FF, D_MODEL)),
            _const_spec((1, D_MODEL)), _const_spec((1, D_MODEL)),
        ],
        out_specs=pl.BlockSpec((tm, D_MODEL), row),
        out_shape=jax.ShapeDtypeStruct((n, D_MODEL), F32),
        scratch_shapes=[pltpu.VMEM((tm + 2 * SUBLANES, D_MODEL), BF16),
                        pltpu.VMEM((tm, D_FF), BF16)],
        compiler_params=_params(1), name="conv_ffn",
    )(x1, x1, x1, w["w_ffn_up"], w["ffn_conv_w"], w["ffn_conv_b"], w["w_ffn_down"],
      w["ln2_g"], w["ln2_b"])


def _rope_tables(T):
    rows = T // GRID_W
    row = jnp.repeat(jnp.arange(rows, dtype=F32), GRID_W)
    col = jnp.tile(jnp.arange(GRID_W, dtype=F32), rows)
    inv_freq = ROPE_THETA ** (-jnp.arange(0, ROPE_AXIS_DIM, 2, dtype=F32) / ROPE_AXIS_DIM)
    ang_r = row[:, None] * inv_freq
    ang_c = col[:, None] * inv_freq
    cos = jnp.concatenate([jnp.cos(ang_r), jnp.cos(ang_c)] * 2, axis=-1)
    sin = jnp.concatenate([-jnp.sin(ang_r), -jnp.sin(ang_c), jnp.sin(ang_r), jnp.sin(ang_c)], axis=-1)
    return cos, sin


def _rope_head_perm():
    p = jnp.arange(ATTN_DH)
    quarter = ROPE_AXIS_DIM // 2
    half, axis, j = p // ROPE_AXIS_DIM, (p % ROPE_AXIS_DIM) // quarter, p % quarter
    return axis * ROPE_AXIS_DIM + half * quarter + j


def _gained_tables(cos, sin, g, scale):
    g = g.astype(F32)[_rope_head_perm()]
    return cos * (g * scale), sin * (jnp.roll(g, ATTN_DH // 2) * scale)


def _prep_weights(l, T, ln_in_g, ln_in_b, w_in, mlstm_gate_bias, mlstm_conv_w, mlstm_conv_b,
                  mlstm_norm_g, attn_q_norm_g, attn_k_norm_g, w_mem_kv, w_branch_mlstm,
                  w_branch_attn, w_branch_mem, w_out, ln1_g, ln1_b, w_ffn_up, ffn_conv_w,
                  ffn_conv_b, w_ffn_down, ln2_g, ln2_b):
    wi = w_in[l]
    g0 = 4 * MLSTM_W
    g1 = g0 + N_GATES
    w_gate = wi[:, g0:g1]
    qk0 = g1
    qk1 = qk0 + ATTN_W + KV_W
    qc0 = qk1 + KV_W
    qc1 = qc0 + MEM_W
    quarter = ROPE_AXIS_DIM // 2
    w_qk = wi[:, qk0:qk1].reshape(D_MODEL, ATTN_HEADS + KV_HEADS, 2, 2, quarter)
    w_qk = jnp.swapaxes(w_qk, 2, 3).reshape(D_MODEL, ATTN_W + KV_W)
    w_qc = wi[:, qc0:qc1] * (MEM_DH ** -0.5 * LOG2_E)
    w_cat = jnp.concatenate([wi[:, :g0], w_qk, wi[:, qk1:qc0], w_qc, wi[:, qc1:]], axis=1).astype(BF16)
    gate_rows = jnp.array(GATE_ROW_ORDER)
    w_gate = w_gate[:, gate_rows]
    gb = mlstm_gate_bias[l].astype(F32)[gate_rows]
    cos, sin = _rope_tables(T)
    cos_q, sin_q = _gained_tables(cos, sin, attn_q_norm_g[l], ATTN_DH ** -0.5 * LOG2_E)
    cos_k, sin_k = _gained_tables(cos, sin, attn_k_norm_g[l], 1.0)
    r2 = lambda a: a.reshape(1, -1).astype(F32)
    ffn_half = jnp.where(jnp.arange(2 * D_FF) >= D_FF, 0.5, 1.0).astype(F32)[None, :]
    return {
        "ln_in_g": r2(ln_in_g), "ln_in_b": r2(ln_in_b),
        "w_cat": w_cat, "w_gt": w_gate.T.astype(BF16),
        "gb_row": jnp.broadcast_to(gb[:, None], (N_GATES, LANES)),
        "mlstm_sel": _mlstm_select(),
        "rope_cos_q": cos_q, "rope_sin_q": sin_q, "rope_cos_k": cos_k, "rope_sin_k": sin_k,
        "mlstm_conv_w": mlstm_conv_w[l].astype(F32), "mlstm_conv_b": r2(mlstm_conv_b[l]),
        "mlstm_norm_g": r2(mlstm_norm_g[l]),
        "w_mem_kv": w_mem_kv[l].astype(BF16),
        "w_branch_mlstm": w_branch_mlstm[l].astype(BF16),
        "w_branch_attn": w_branch_attn[l].astype(BF16),
        "w_branch_mem": w_branch_mem[l].astype(BF16),
        "w_out": w_out[l].astype(BF16),
        "ln1_g": r2(ln1_g[l]), "ln1_b": r2(ln1_b[l]),
        "w_ffn_up": w_ffn_up[l].astype(BF16),
        "ffn_conv_w": ffn_conv_w[l].astype(F32) * ffn_half,
        "ffn_conv_b": r2(ffn_conv_b[l]) * ffn_half,
        "w_ffn_down": w_ffn_down[l].astype(BF16),
        "ln2_g": r2(ln2_g[l]), "ln2_b": r2(ln2_b[l]),
    }


def _tile(T, want):
    t = min(want, T)
    assert T % t == 0
    return t


def _trunk(x, mem, w):
    B, T, _ = x.shape
    M = mem.shape[1]
    assert T % MLSTM_CHUNK == 0 and T % GRID_W == 0
    x2 = x.reshape(B * T, D_MODEL)
    mem2 = mem.reshape(B * M, D_MODEL)
    xn, qm, ktm, vom, qa, ka, va, qc, gbr, gi, gf = _in_proj(x2, T, w, _tile(T, 512))
    hm = _mlstm(qm, ktm, vom, gi, gf, T, w)
    ha = _attention(qa, ka, va, T, _tile(T, 2048))
    kvc = _mem_kv(mem2, w)
    x1 = _merge(xn, hm, ha, qc, kvc, gbr, T, M, w, _tile(T, 512))
    y = _ffn(x1, T, w, _tile(T, 1024))
    return y.reshape(B, T, D_MODEL)


def kernel(x_prompt, x_sample, mem_prompt, mem_sample, ln_in_g, ln_in_b, w_in, mlstm_gate_bias, mlstm_conv_w, mlstm_conv_b, mlstm_norm_g, attn_q_norm_g, attn_k_norm_g, w_mem_kv, w_branch_mlstm, w_branch_attn, w_branch_mem, w_out, ln1_g, ln1_b, w_ffn_up, ffn_conv_w, ffn_conv_b, w_ffn_down, ln2_g, ln2_b):
    assert DEPTH == 1 and x_prompt.shape[1] == x_sample.shape[1]
    w = _prep_weights(0, x_prompt.shape[1], ln_in_g, ln_in_b, w_in, mlstm_gate_bias, mlstm_conv_w,
                      mlstm_conv_b, mlstm_norm_g, attn_q_norm_g, attn_k_norm_g, w_mem_kv,
                      w_branch_mlstm, w_branch_attn, w_branch_mem, w_out, ln1_g, ln1_b, w_ffn_up,
                      ffn_conv_w, ffn_conv_b, w_ffn_down, ln2_g, ln2_b)
    return (_trunk(x_prompt, mem_prompt, w), _trunk(x_sample, mem_sample, w))
```

```python
import functools

import jax
import jax.numpy as jnp
from jax import lax
from jax.experimental import pallas as pl
from jax.experimental.pallas import tpu as pltpu

F32 = jnp.float32
BF16 = jnp.bfloat16

D_MODEL = 1024
DEPTH = 1
GRID_W = 64
MLSTM_HEADS = 4
MLSTM_DH = 128
MLSTM_W = MLSTM_HEADS * MLSTM_DH
MLSTM_CHUNK = 128
N_GATES = 4 * MLSTM_HEADS
ATTN_DH = 128
ATTN_HEADS = 8
KV_HEADS = 2
ATTN_GROUP = ATTN_HEADS // KV_HEADS
ATTN_W = ATTN_HEADS * ATTN_DH
KV_W = KV_HEADS * ATTN_DH
ROPE_AXIS_DIM = ATTN_DH // 2
ROPE_THETA = 10000.0
MEM_HEADS = 4
MEM_DH = 128
MEM_W = MEM_HEADS * MEM_DH
N_BRANCH = 3
D_FF = ((8 * D_MODEL // 3 + 127) // 128) * 128
CONV_W = 3
DEEPNORM_ALPHA = (2.0 * DEPTH) ** 0.25
LN_EPS = 1e-5
LOG2_E = 1.4426950408889634

LANES = 128
SUBLANES = 8
VMEM_LIMIT_BYTES = 56 * 1024 * 1024

SEG_QKM = (0, 2 * MLSTM_W)
SEG_VOM = (SEG_QKM[1], SEG_QKM[1] + 2 * MLSTM_W)
SEG_QA = (SEG_VOM[1], SEG_VOM[1] + ATTN_W)
SEG_KA = (SEG_QA[1], SEG_QA[1] + KV_W)
SEG_VA = (SEG_KA[1], SEG_KA[1] + KV_W)
SEG_QC = (SEG_VA[1], SEG_VA[1] + MEM_W)
SEG_GBR = (SEG_QC[1], SEG_QC[1] + N_BRANCH * D_MODEL)
W_CAT = SEG_GBR[1]
GATE_ROW_ORDER = tuple(g * MLSTM_HEADS + h for g in (0, 2, 1, 3) for h in range(MLSTM_HEADS))

FFN_CHUNK = 256
ATTN_UNIT_ROWS = 128


def _params(n_axes):
    return pltpu.CompilerParams(dimension_semantics=("arbitrary",) * n_axes,
                                vmem_limit_bytes=VMEM_LIMIT_BYTES)


def _const_spec(shape):
    nd = len(shape)
    return pl.BlockSpec(shape, lambda *_: (0,) * nd, pipeline_mode=pl.Buffered(1))


def _layer_norm(x, g, b):
    mu = jnp.mean(x, axis=-1, keepdims=True)
    xc = x - mu
    var = jnp.mean(xc * xc, axis=-1, keepdims=True)
    return xc * lax.rsqrt(var + LN_EPS) * g + b


def _sigmoid(x):
    return 1.0 / (1.0 + jnp.exp(-x))


def _log_sigmoid(x):
    return jnp.minimum(x, 0.0) - jnp.log(1.0 + jnp.exp(-jnp.abs(x)))


def _dot(a, b):
    return jnp.dot(a, b, preferred_element_type=F32)


def _dot_nt(a, b):
    return lax.dot_general(a, b, (((1,), (1,)), ((), ())), preferred_element_type=F32)


def _split3(x):
    x1 = x.astype(BF16)
    r1 = x - x1.astype(F32)
    x2 = r1.astype(BF16)
    r2 = r1 - x2.astype(F32)
    return x1, x2, r2.astype(BF16)


def _rms_rope(xh, cos_g, sin_g):
    ms = jnp.mean(xh * xh, axis=-1, keepdims=True)
    xh = xh * lax.rsqrt(ms + LN_EPS)
    return xh * cos_g + pltpu.roll(xh, LANES // 2, 1) * sin_g


def _proj_kernel(xp_ref, x_ref, xn_ref, lng_ref, lnb_ref, w_ref, wgt_ref, gbr_row_ref,
                 cq_ref, sq_ref, ck_ref, sk_ref, cw_ref, cb_ref,
                 xno_ref, qm_ref, ktm_ref, vom_ref, qa_ref, ka_ref, va_ref, qc_ref, gbr_ref, gi_ref, gf_ref,
                 *, tiles_per_seq):
    tm = x_ref.shape[0]
    lng = lng_ref[...]
    lnb = lnb_ref[...]
    xn = _layer_norm(x_ref[...], lng, lnb)
    xno_ref[...] = xn
    xb = xn.astype(BF16)

    def proj(lo, hi):
        return _dot(xb, w_ref[:, lo:hi])

    j = pl.program_id(0) % tiles_per_seq
    hr = xp_ref.shape[0]
    te = tm + 2 * hr
    xbe = jnp.concatenate([_layer_norm(xp_ref[...], lng, lnb).astype(BF16), xb,
                           _layer_norm(xn_ref[...], lng, lnb).astype(BF16)], axis=0)
    keep_prev = jnp.where(j > 0, 1.0, 0.0).astype(F32)
    keep_next = jnp.where(j < tiles_per_seq - 1, 1.0, 0.0).astype(F32)

    def conv_silu(lo, scale):
        cols = slice(lo, lo + MLSTM_W)
        ue = _dot(xbe, w_ref[:, SEG_QKM[0] + lo:SEG_QKM[0] + lo + MLSTM_W])
        ue = jnp.concatenate([ue[:hr] * keep_prev, ue[hr:hr + tm], ue[hr + tm:] * keep_next], axis=0)
        u_prev = pltpu.roll(ue, 1, 0)[hr:hr + tm]
        u_next = pltpu.roll(ue, te - 1, 0)[hr:hr + tm]
        y = u_prev * cw_ref[0:1, cols] + cb_ref[:, cols]
        y = y + ue[hr:hr + tm] * cw_ref[1:2, cols]
        y = y + u_next * cw_ref[2:3, cols]
        y = y * _sigmoid(y)
        return y if scale == 1.0 else y * scale

    def q_job():
        qm_ref[...] = conv_silu(0, 1.0).astype(BF16)

    def k_job():
        yk = conv_silu(MLSTM_W, MLSTM_DH ** -0.5)
        for a in range(tm // MLSTM_CHUNK):
            for h in range(MLSTM_HEADS):
                blk = yk[a * MLSTM_CHUNK:(a + 1) * MLSTM_CHUNK, h * MLSTM_DH:(h + 1) * MLSTM_DH]
                r0 = a * MLSTM_W + h * MLSTM_DH
                ktm_ref[r0:r0 + MLSTM_DH, :] = blk.T.astype(BF16)

    def plain_job(out_ref, dst, src, piece):
        def job():
            out_ref[:, dst:dst + piece] = proj(src, src + piece).astype(out_ref.dtype)
        return job

    def rope_job(out_ref, dst, src, c_ref, s_ref):
        def job():
            blk = proj(src, src + pair)
            for jj in range(2):
                xh = _rms_rope(blk[:, jj * ATTN_DH:(jj + 1) * ATTN_DH], c_ref[...], s_ref[...])
                out_ref[:, dst + jj * ATTN_DH:dst + (jj + 1) * ATTN_DH] = xh.astype(BF16)
        return job

    pair = 2 * ATTN_DH
    vector_jobs = [q_job, k_job]
    vector_jobs += [rope_job(qa_ref, c - SEG_QA[0], c, cq_ref, sq_ref) for c in range(SEG_QA[0], SEG_QA[1], pair)]
    vector_jobs += [rope_job(ka_ref, c - SEG_KA[0], c, ck_ref, sk_ref) for c in range(SEG_KA[0], SEG_KA[1], pair)]
    plain_jobs = []
    for out_ref, seg, piece in ((vom_ref, SEG_VOM, 512), (va_ref, SEG_VA, KV_W),
                                (qc_ref, SEG_QC, 512), (gbr_ref, SEG_GBR, 512)):
        plain_jobs += [plain_job(out_ref, c - seg[0], c, piece) for c in range(seg[0], seg[1], piece)]

    while vector_jobs or plain_jobs:
        if plain_jobs:
            plain_jobs.pop(0)()
        if vector_jobs:
            vector_jobs.pop(0)()

    grow = _dot_nt(wgt_ref[...], xb)
    nd = N_GATES // 2
    for j in range(tm // LANES):
        blk = grow[:, j * LANES:(j + 1) * LANES] + gbr_row_ref[...]
        gi_ref[j * nd:(j + 1) * nd, :] = blk[:nd]
        gf_ref[j * nd:(j + 1) * nd, :] = blk[nd:]


def _in_proj(x2, T, w, tm):
    n = x2.shape[0]
    grid = (n // tm,)
    tiles_per_seq = T // tm
    row = lambda i: (i, 0)
    hb = 2 * SUBLANES
    nhb = n // hb
    out_shapes = (
        jax.ShapeDtypeStruct((n, D_MODEL), F32),
        jax.ShapeDtypeStruct((n, MLSTM_W), BF16),
        jax.ShapeDtypeStruct((n // MLSTM_CHUNK * MLSTM_W, MLSTM_CHUNK), BF16),
        jax.ShapeDtypeStruct((n, 2 * MLSTM_W), BF16),
        jax.ShapeDtypeStruct((n, ATTN_W), BF16),
        jax.ShapeDtypeStruct((n, KV_W), BF16),
        jax.ShapeDtypeStruct((n, KV_W), BF16),
        jax.ShapeDtypeStruct((n, MEM_W), BF16),
        jax.ShapeDtypeStruct((n, N_BRANCH * D_MODEL), BF16),
        jax.ShapeDtypeStruct((n // LANES * (N_GATES // 2), LANES), F32),
        jax.ShapeDtypeStruct((n // LANES * (N_GATES // 2), LANES), F32),
    )
    out_specs = (
        pl.BlockSpec((tm, D_MODEL), row),
        pl.BlockSpec((tm, MLSTM_W), row),
        pl.BlockSpec((tm // MLSTM_CHUNK * MLSTM_W, MLSTM_CHUNK), row),
        pl.BlockSpec((tm, 2 * MLSTM_W), row),
        pl.BlockSpec((tm, ATTN_W), row),
        pl.BlockSpec((tm, KV_W), row),
        pl.BlockSpec((tm, KV_W), row),
        pl.BlockSpec((tm, MEM_W), row),
        pl.BlockSpec((tm, N_BRANCH * D_MODEL), row),
        pl.BlockSpec((tm // LANES * (N_GATES // 2), LANES), row),
        pl.BlockSpec((tm // LANES * (N_GATES // 2), LANES), row),
    )
    in_specs = [
        pl.BlockSpec((hb, D_MODEL), lambda i: (jnp.maximum(i * (tm // hb) - 1, 0), 0)),
        pl.BlockSpec((tm, D_MODEL), row),
        pl.BlockSpec((hb, D_MODEL), lambda i: (jnp.minimum((i + 1) * (tm // hb), nhb - 1), 0)),
        _const_spec((1, D_MODEL)), _const_spec((1, D_MODEL)),
        _const_spec((D_MODEL, W_CAT)), _const_spec((N_GATES, D_MODEL)),
        _const_spec((N_GATES, LANES)),
    ] + [pl.BlockSpec((tm, LANES), lambda i: (i % tiles_per_seq, 0))] * 4 + [
        _const_spec((CONV_W, 2 * MLSTM_W)), _const_spec((1, 2 * MLSTM_W)),
    ]
    return pl.pallas_call(
        functools.partial(_proj_kernel, tiles_per_seq=tiles_per_seq),
        grid=grid, in_specs=in_specs, out_specs=out_specs, out_shape=out_shapes,
        compiler_params=_params(1), name="in_proj",
    )(x2, x2, x2, w["ln_in_g"], w["ln_in_b"], w["w_cat"], w["w_gt"], w["gb_row"],
      w["rope_cos_q"], w["rope_sin_q"], w["rope_cos_k"], w["rope_sin_k"],
      w["mlstm_conv_w"], w["mlstm_conv_b"])


N_DIRHEADS = 2 * MLSTM_HEADS
STAT_GROUPS = 6
STAT_BLOCK = (0, 0, 0, 1, 1, 1)
STAT_BLOCKS = 2


def _mlstm_select():
    lane = jnp.arange(LANES)
    grp, j = lane // N_DIRHEADS, lane % N_DIRHEADS
    blk = jnp.array(STAT_BLOCK + (-1,) * (LANES // N_DIRHEADS - STAT_GROUPS))[grp]
    out_blk = jnp.arange(STAT_BLOCKS * LANES) // LANES
    sel = (j[None, :, None] == jnp.arange(N_DIRHEADS)[:, None, None]) & (blk[None, :, None] == out_blk[None, None, :])
    return sel.astype(BF16)


def _mlstm_kernel(q_ref, kt_ref, vom_ref, gi_ref, gf_ref, sel_ref, ng_ref, hm_ref,
                  ktw_s, lhs_s, e_s, stat_s, gt_s, ma_s, mpf_s, mpb_s, r_s, wa_s, sp_s,
                  hf_s, st_s):
    T = q_ref.shape[0]
    L = MLSTM_CHUNK
    NC = T // L
    DH = MLSTM_DH
    H = MLSTM_HEADS

    ri = lax.broadcasted_iota(jnp.int32, (L, L), 0)
    ci = lax.broadcasted_iota(jnp.int32, (L, L), 1)
    causal = ci <= ri
    anti = ci >= ri
    tril = jnp.where(causal, 1.0, 0.0).astype(BF16)
    triu = jnp.where(anti, 1.0, 0.0).astype(BF16)
    ones_ll = jnp.ones((L, L), BF16)

    R = NC * N_DIRHEADS
    row_r = lax.broadcasted_iota(jnp.int32, (R, L), 0)
    lane_r = lax.broadcasted_iota(jnp.int32, (R, L), 1)
    is_fwd = (row_r % N_DIRHEADS) < H
    is_fwd8 = lax.broadcasted_iota(jnp.int32, (N_DIRHEADS, L), 0) < H

    li = gi_ref[...]
    y1, y2, y3 = _split3(_log_sigmoid(gf_ref[...]))

    def lane_sums(m):
        return _dot(y1, m) + _dot(y2, m) + _dot(y3, m)

    bl = jnp.where(is_fwd, lane_sums(triu), lane_sums(tril))
    gt = lane_sums(ones_ll)
    r = li - bl
    pm = r
    sm = r
    for sh in (1, 2, 4, 8, 16, 32, 64):
        pm = jnp.maximum(pm, jnp.where(lane_r >= sh, pltpu.roll(pm, sh, 1), -jnp.inf))
        sm = jnp.maximum(sm, jnp.where(lane_r < L - sh, pltpu.roll(sm, L - sh, 1), -jnp.inf))
    cm = jnp.where(is_fwd, pm, sm)
    a = gt - bl + li
    ma = jnp.broadcast_to(jnp.max(a, axis=-1, keepdims=True), (R, L))
    gt_s[...] = gt
    ma_s[...] = ma

    def chunk_rows(c):
        return pl.ds(pl.multiple_of(c * N_DIRHEADS, N_DIRHEADS), N_DIRHEADS)

    def stabiliser_scan(i, m):
        rf = chunk_rows(i)
        rb = chunk_rows(NC - 1 - i)
        mpf_s[rf, :] = m
        mpb_s[rb, :] = m
        gt8 = jnp.where(is_fwd8, gt_s[rf, :], gt_s[rb, :])
        ma8 = jnp.where(is_fwd8, ma_s[rf, :], ma_s[rb, :])
        return jnp.maximum(gt8 + m, ma8)

    lax.fori_loop(0, NC, stabiliser_scan, jnp.zeros((N_DIRHEADS, L), F32))

    m_prev = jnp.where(is_fwd, mpf_s[...], mpb_s[...])
    m_new = jnp.maximum(gt + m_prev, ma)
    sp_s[...] = jnp.exp(gt + m_prev - m_new)
    wa_s[...] = jnp.exp(a - m_new)
    r_s[...] = r * LOG2_E
    ma_s[...] = m_prev * LOG2_E
    u = -jnp.maximum(m_prev, cm)
    stat_vals = list(_split3(u * LOG2_E)) + list(_split3(jnp.exp(u - bl)))
    for k, t in enumerate(stat_vals):
        stat_s[k] = t.astype(F32)
    stat_pad = jnp.zeros((L - STAT_GROUPS * N_DIRHEADS, L), F32)

    def prologue(c, carry):
        r0 = pl.multiple_of(c * L, L)
        rows = pl.ds(r0, L)
        rows8 = chunk_rows(c)
        stats = [stat_s[k, rows8, :] for k in range(STAT_GROUPS)] + [stat_pad]
        stats_t = jnp.concatenate(stats, axis=0).T.astype(BF16)
        wa = wa_s[rows8, :]
        r8 = r_s[rows8, :]
        mp8 = ma_s[rows8, :]
        kt_rows = pl.multiple_of(c * MLSTM_W, MLSTM_W)
        for h in range(H):
            hs = slice(h * DH, (h + 1) * DH)
            q = q_ref[rows, hs]
            kt_b = kt_ref[pl.ds(kt_rows + h * DH, DH), :]
            s = _dot(q, kt_b)
            kt = kt_b.astype(F32)
            for d in range(2):
                j = H * d + h
                bc = _dot(stats_t, sel_ref[j])
                u_b = bc[:, :L]
                p = jnp.where(causal if d == 0 else anti, jnp.exp2(u_b + r8[j:j + 1, :]), 0.0)
                w_inter = jnp.exp2(u_b + mp8[j:j + 1, :])
                lhs_s[d, rows, 2 * h * DH:(2 * h + 1) * DH] = (s * p).astype(BF16)
                lhs_s[d, rows, (2 * h + 1) * DH:(2 * h + 2) * DH] = w_inter.astype(BF16) * q
                e_s[j, rows, :] = bc[:, L:]
                ktw_s[d, c, hs, :] = (kt * wa[j:j + 1, :]).astype(BF16)
        return carry

    lax.fori_loop(0, NC, prologue, 0, unroll=4)

    ones_blk = jnp.ones((L, DH), BF16)
    ng = ng_ref[...]

    def run_direction(d):
        st_s[...] = jnp.zeros_like(st_s)

        def body(i, carry):
            c = i if d == 0 else NC - 1 - i
            r0 = pl.multiple_of(c * L, L)
            rows = pl.ds(r0, L)
            for h in range(H):
                j = H * d + h
                hs = slice(h * DH, (h + 1) * DH)
                lhs = lhs_s[d, rows, 2 * h * DH:(2 * h + 2) * DH]
                vaug = jnp.concatenate([vom_ref[rows, hs], ones_blk], axis=1)
                c_prev = st_s[h]
                tot = _dot(lhs, jnp.concatenate([vaug, c_prev.astype(BF16)], axis=0))
                h_out = tot[:, :DH] / jnp.maximum(jnp.abs(tot[:, DH:]), e_s[j, rows, :])
                sp = sp_s[pl.ds(c * N_DIRHEADS + j, 1), :]
                st_s[h] = jnp.concatenate([sp, sp], axis=1) * c_prev + _dot(ktw_s[d, c, hs, :], vaug)

                if d == 0:
                    hf_s[rows, hs] = h_out
                else:
                    hh = hf_s[rows, hs] + h_out
                    mu = jnp.mean(hh, axis=-1, keepdims=True)
                    hc = hh - mu
                    var = jnp.mean(hc * hc, axis=-1, keepdims=True)
                    hn = hc * lax.rsqrt(var + LN_EPS) * ng[:, hs]
                    o = vom_ref[rows, MLSTM_W + h * DH:MLSTM_W + (h + 1) * DH].astype(F32)
                    hm_ref[rows, hs] = (hn * _sigmoid(o)).astype(BF16)
            return carry

        lax.fori_loop(0, NC, body, 0, unroll=4)

    run_direction(0)
    run_direction(1)


def _mlstm(qm, ktm, vom, gi, gf, T, w):
    n = qm.shape[0]
    B = n // T
    NC = T // MLSTM_CHUNK
    seq = lambda b: (b, 0)
    gate_major = pltpu.VMEM((NC * N_DIRHEADS, MLSTM_CHUNK), F32)
    return pl.pallas_call(
        _mlstm_kernel, grid=(B,),
        in_specs=[
            pl.BlockSpec((T, MLSTM_W), seq),
            pl.BlockSpec((NC * MLSTM_W, MLSTM_CHUNK), seq),
            pl.BlockSpec((T, 2 * MLSTM_W), seq),
            pl.BlockSpec((NC * N_DIRHEADS, MLSTM_CHUNK), seq),
            pl.BlockSpec((NC * N_DIRHEADS, MLSTM_CHUNK), seq),
            _const_spec((N_DIRHEADS, LANES, STAT_BLOCKS * MLSTM_CHUNK)),
            _const_spec((1, MLSTM_W)),
        ],
        out_specs=pl.BlockSpec((T, MLSTM_W), seq),
        out_shape=jax.ShapeDtypeStruct((n, MLSTM_W), BF16),
        scratch_shapes=[
            pltpu.VMEM((2, NC, MLSTM_W, MLSTM_CHUNK), BF16),
            pltpu.VMEM((2, T, 2 * MLSTM_W), BF16),
            pltpu.VMEM((N_DIRHEADS, T, LANES), F32),
            pltpu.VMEM((STAT_GROUPS, NC * N_DIRHEADS, MLSTM_CHUNK), F32),
            gate_major, gate_major, gate_major, gate_major,
            gate_major, gate_major, gate_major,
            pltpu.VMEM((T, MLSTM_W), F32),
            pltpu.VMEM((MLSTM_HEADS, MLSTM_DH, 2 * MLSTM_DH), F32),
        ],
        compiler_params=_params(1), name="mlstm",
    )(qm, ktm, vom, gi, gf, w["mlstm_sel"], w["mlstm_norm_g"])


def _attn_kernel(q_ref, k_ref, v_ref, o_ref, vaug_s):
    @pl.when(pl.program_id(2) == 0)
    def _():
        vaug_s[:, :ATTN_DH] = v_ref[...]
        vaug_s[:, ATTN_DH:] = jnp.ones((v_ref.shape[0], ATTN_DH), BF16)

    k = k_ref[...]
    tq = q_ref.shape[0]
    rows_per_unit = min(tq, ATTN_UNIT_ROWS)
    for g in range(ATTN_GROUP):
        hs = slice(g * ATTN_DH, (g + 1) * ATTN_DH)
        for r0 in range(0, tq, rows_per_unit):
            rs = slice(r0, r0 + rows_per_unit)
            s = _dot_nt(q_ref[rs, hs], k)
            m = jnp.max(s, axis=-1, keepdims=True)
            p = jnp.exp2(s - m)
            oa = _dot(p.astype(BF16), vaug_s[...])
            o_ref[rs, hs] = (oa[:, :ATTN_DH] / oa[:, ATTN_DH:ATTN_DH + 1]).astype(o_ref.dtype)


def _attention(qa, ka, va, T, tq):
    n = qa.shape[0]
    B = n // T
    nq = T // tq
    gw = ATTN_GROUP * ATTN_DH
    return pl.pallas_call(
        _attn_kernel, grid=(B, KV_HEADS, nq),
        in_specs=[
            pl.BlockSpec((tq, gw), lambda b, h, i: (b * nq + i, h)),
            pl.BlockSpec((T, ATTN_DH), lambda b, h, i: (b, h)),
            pl.BlockSpec((T, ATTN_DH), lambda b, h, i: (b, h)),
        ],
        out_specs=pl.BlockSpec((tq, gw), lambda b, h, i: (b * nq + i, h)),
        out_shape=jax.ShapeDtypeStruct((n, ATTN_W), BF16),
        scratch_shapes=[pltpu.VMEM((T, 2 * ATTN_DH), BF16)],
        compiler_params=_params(3), name="gqa_attn",
    )(qa, ka, va)


def _mem_kv_kernel(m_ref, w_ref, o_ref):
    o_ref[...] = _dot(m_ref[...].astype(BF16), w_ref[...]).astype(o_ref.dtype)


def _mem_kv(mem2, w):
    n = mem2.shape[0]
    tm = _tile(n, 1024)
    return pl.pallas_call(
        _mem_kv_kernel, grid=(n // tm,),
        in_specs=[pl.BlockSpec((tm, D_MODEL), lambda b: (b, 0)),
                  _const_spec((D_MODEL, 2 * MEM_W))],
        out_specs=pl.BlockSpec((tm, 2 * MEM_W), lambda b: (b, 0)),
        out_shape=jax.ShapeDtypeStruct((n, 2 * MEM_W), BF16),
        compiler_params=_params(1), name="mem_kv",
    )(mem2, w["w_mem_kv"])


def _merge_kernel(xn_ref, hm_ref, ha_ref, qc_ref, kv_ref, gbr_ref,
                  wbm_ref, wba_ref, wbc_ref, wo_ref, l1g_ref, l1b_ref, x1_ref):
    ones_blk = jnp.ones((kv_ref.shape[0], MEM_DH), BF16)
    hc_parts = []
    for h in range(MEM_HEADS):
        hs = slice(h * MEM_DH, (h + 1) * MEM_DH)
        kc = kv_ref[:, hs]
        vaug = jnp.concatenate([kv_ref[:, MEM_W + h * MEM_DH:MEM_W + (h + 1) * MEM_DH], ones_blk], axis=1)
        s = _dot_nt(qc_ref[:, hs], kc)
        p = jnp.exp2(s - jnp.max(s, axis=-1, keepdims=True))
        oa = _dot(p.astype(BF16), vaug)
        hc_parts.append((oa[:, :MEM_DH] / oa[:, MEM_DH:MEM_DH + 1]).astype(BF16))
    hc = jnp.concatenate(hc_parts, axis=1)

    def gate(j):
        return _sigmoid(gbr_ref[:, j * D_MODEL:(j + 1) * D_MODEL].astype(F32))

    merged = gate(0) * _dot(hm_ref[...], wbm_ref[...])
    merged = merged + gate(1) * _dot(ha_ref[...], wba_ref[...])
    merged = merged + gate(2) * _dot(hc, wbc_ref[...])
    mixed = _dot(merged.astype(BF16), wo_ref[...])
    x1_ref[...] = _layer_norm(DEEPNORM_ALPHA * xn_ref[...] + mixed, l1g_ref[...], l1b_ref[...])


def _merge(xn, hm, ha, qc, kvc, gbr, T, M, w, tm):
    n = xn.shape[0]
    tiles_per_seq = T // tm
    row = lambda i: (i, 0)
    return pl.pallas_call(
        _merge_kernel, grid=(n // tm,),
        in_specs=[
            pl.BlockSpec((tm, D_MODEL), row),
            pl.BlockSpec((tm, MLSTM_W), row),
            pl.BlockSpec((tm, ATTN_W), row),
            pl.BlockSpec((tm, MEM_W), row),
            pl.BlockSpec((M, 2 * MEM_W), lambda i: (i // tiles_per_seq, 0)),
            pl.BlockSpec((tm, N_BRANCH * D_MODEL), row),
            _const_spec((MLSTM_W, D_MODEL)), _const_spec((ATTN_W, D_MODEL)),
            _const_spec((MEM_W, D_MODEL)), _const_spec((D_MODEL, D_MODEL)),
            _const_spec((1, D_MODEL)), _const_spec((1, D_MODEL)),
        ],
        out_specs=pl.BlockSpec((tm, D_MODEL), row),
        out_shape=jax.ShapeDtypeStruct((n, D_MODEL), F32),
        compiler_params=_params(1), name="merge_out",
    )(xn, hm, ha, qc, kvc, gbr, w["w_branch_mlstm"],
      w["w_branch_attn"], w["w_branch_mem"], w["w_out"], w["ln1_g"], w["ln1_b"])


GELU_C = 0.7978845608028654
GELU_C3 = GELU_C * 0.044715


def _gelu_tanh_x2(x):
    return x * (1.0 + jnp.tanh(x * (GELU_C + GELU_C3 * (x * x))))


def _ffn_kernel(xp_ref, x_ref, xn_ref, wup_ref, cw_ref, cb_ref, wdn_ref, l2g_ref, l2b_ref,
                y_ref, xe_s, act_s, *, tiles_per_seq):
    tm = x_ref.shape[0]
    i = pl.program_id(0)
    j = i % tiles_per_seq
    prev = xp_ref[...] * jnp.where(j > 0, 1.0, 0.0).astype(F32)
    nxt = xn_ref[...] * jnp.where(j < tiles_per_seq - 1, 1.0, 0.0).astype(F32)
    xe_s[...] = jnp.concatenate([prev, x_ref[...], nxt], axis=0).astype(BF16)
    te = tm + 2 * SUBLANES

    def conv(u, cw, cb):
        up = pltpu.roll(u, 1, 0)[SUBLANES:SUBLANES + tm, :]
        uc = u[SUBLANES:SUBLANES + tm, :]
        un = pltpu.roll(u, te - 1, 0)[SUBLANES:SUBLANES + tm, :]
        y = up * cw[0:1, :] + cb
        y = y + uc * cw[1:2, :]
        return y + un * cw[2:3, :]

    def up_conv(lo):
        cols = slice(lo, lo + FFN_CHUNK)
        return conv(_dot(xe_s[...], wup_ref[:, cols]), cw_ref[:, cols], cb_ref[:, cols])

    for c in range(0, D_FF, FFN_CHUNK):
        act_s[:, c:c + FFN_CHUNK] = (_gelu_tanh_x2(up_conv(c)) * up_conv(D_FF + c)).astype(BF16)

    ff = _dot(act_s[...], wdn_ref[...])
    y_ref[...] = _layer_norm(DEEPNORM_ALPHA * x_ref[...] + ff, l2g_ref[...], l2b_ref[...])


def _ffn(x1, T, w, tm):
    n = x1.shape[0]
    tiles_per_seq = T // tm
    r8 = tm // SUBLANES
    nblk8 = n // SUBLANES
    row = lambda i: (i, 0)
    return pl.pallas_call(
        functools.partial(_ffn_kernel, tiles_per_seq=tiles_per_seq), grid=(n // tm,),
        in_specs=[
            pl.BlockSpec((SUBLANES, D_MODEL), lambda i: (jnp.maximum(i * r8 - 1, 0), 0)),
            pl.BlockSpec((tm, D_MODEL), row),
            pl.BlockSpec((SUBLANES, D_MODEL), lambda i: (jnp.minimum((i + 1) * r8, nblk8 - 1), 0)),
            _const_spec((D_MODEL, 2 * D_FF)),
            _const_spec((CONV_W, 2 * D_FF)),
            _const_spec((1, 2 * D_FF)),
            _const_spec((D_FF, D_MODEL)),
            _const_spec((1, D_MODEL)), _const_spec((1, D_MODEL)),
        ],
        out_specs=pl.BlockSpec((tm, D_MODEL), row),
        out_shape=jax.ShapeDtypeStruct((n, D_MODEL), F32),
        scratch_shapes=[pltpu.VMEM((tm + 2 * SUBLANES, D_MODEL), BF16),
                        pltpu.VMEM((tm, D_FF), BF16)],
        compiler_params=_params(1), name="conv_ffn",
    )(x1, x1, x1, w["w_ffn_up"], w["ffn_conv_w"], w["ffn_conv_b"], w["w_ffn_down"],
      w["ln2_g"], w["ln2_b"])


def _rope_tables(T):
    rows = T // GRID_W
    row = jnp.repeat(jnp.arange(rows, dtype=F32), GRID_W)
    col = jnp.tile(jnp.arange(GRID_W, dtype=F32), rows)
    inv_freq = ROPE_THETA ** (-jnp.arange(0, ROPE_AXIS_DIM, 2, dtype=F32) / ROPE_AXIS_DIM)
    ang_r = row[:, None] * inv_freq
    ang_c = col[:, None] * inv_freq
    cos = jnp.concatenate([jnp.cos(ang_r), jnp.cos(ang_c)] * 2, axis=-1)
    sin = jnp.concatenate([-jnp.sin(ang_r), -jnp.sin(ang_c), jnp.sin(ang_r), jnp.sin(ang_c)], axis=-1)
    return cos, sin


def _rope_head_perm():
    p = jnp.arange(ATTN_DH)
    quarter = ROPE_AXIS_DIM // 2
    half, axis, j = p // ROPE_AXIS_DIM, (p % ROPE_AXIS_DIM) // quarter, p % quarter
    return axis * ROPE_AXIS_DIM + half * quarter + j


def _gained_tables(cos, sin, g, scale):
    g = g.astype(F32)[_rope_head_perm()]
    return cos * (g * scale), sin * (jnp.roll(g, ATTN_DH // 2) * scale)


def _prep_weights(l, T, ln_in_g, ln_in_b, w_in, mlstm_gate_bias, mlstm_conv_w, mlstm_conv_b,
                  mlstm_norm_g, attn_q_norm_g, attn_k_norm_g, w_mem_kv, w_branch_mlstm,
                  w_branch_attn, w_branch_mem, w_out, ln1_g, ln1_b, w_ffn_up, ffn_conv_w,
                  ffn_conv_b, w_ffn_down, ln2_g, ln2_b):
    wi = w_in[l]
    g0 = 4 * MLSTM_W
    g1 = g0 + N_GATES
    w_gate = wi[:, g0:g1]
    qk0 = g1
    qk1 = qk0 + ATTN_W + KV_W
    qc0 = qk1 + KV_W
    qc1 = qc0 + MEM_W
    quarter = ROPE_AXIS_DIM // 2
    w_qk = wi[:, qk0:qk1].reshape(D_MODEL, ATTN_HEADS + KV_HEADS, 2, 2, quarter)
    w_qk = jnp.swapaxes(w_qk, 2, 3).reshape(D_MODEL, ATTN_W + KV_W)
    w_qc = wi[:, qc0:qc1] * (MEM_DH ** -0.5 * LOG2_E)
    w_cat = jnp.concatenate([wi[:, :g0], w_qk, wi[:, qk1:qc0], w_qc, wi[:, qc1:]], axis=1).astype(BF16)
    gate_rows = jnp.array(GATE_ROW_ORDER)
    w_gate = w_gate[:, gate_rows]
    gb = mlstm_gate_bias[l].astype(F32)[gate_rows]
    cos, sin = _rope_tables(T)
    cos_q, sin_q = _gained_tables(cos, sin, attn_q_norm_g[l], ATTN_DH ** -0.5 * LOG2_E)
    cos_k, sin_k = _gained_tables(cos, sin, attn_k_norm_g[l], 1.0)
    r2 = lambda a: a.reshape(1, -1).astype(F32)
    ffn_half = jnp.where(jnp.arange(2 * D_FF) >= D_FF, 0.5, 1.0).astype(F32)[None, :]
    return {
        "ln_in_g": r2(ln_in_g), "ln_in_b": r2(ln_in_b),
        "w_cat": w_cat, "w_gt": w_gate.T.astype(BF16),
        "gb_row": jnp.broadcast_to(gb[:, None], (N_GATES, LANES)),
        "mlstm_sel": _mlstm_select(),
        "rope_cos_q": cos_q, "rope_sin_q": sin_q, "rope_cos_k": cos_k, "rope_sin_k": sin_k,
        "mlstm_conv_w": mlstm_conv_w[l].astype(F32), "mlstm_conv_b": r2(mlstm_conv_b[l]),
        "mlstm_norm_g": r2(mlstm_norm_g[l]),
        "w_mem_kv": w_mem_kv[l].astype(BF16),
        "w_branch_mlstm": w_branch_mlstm[l].astype(BF16),
        "w_branch_attn": w_branch_attn[l].astype(BF16),
        "w_branch_mem": w_branch_mem[l].astype(BF16),
        "w_out": w_out[l].astype(BF16),
        "ln1_g": r2(ln1_g[l]), "ln1_b": r2(ln1_b[l]),
        "w_ffn_up": w_ffn_up[l].astype(BF16),
        "ffn_conv_w": ffn_conv_w[l].astype(F32) * ffn_half,
        "ffn_conv_b": r2(ffn_conv_b[l]) * ffn_half,
        "w_ffn_down": w_ffn_down[l].astype(BF16),
        "ln2_g": r2(ln2_g[l]), "ln2_b": r2(ln2_b[l]),
    }


def _tile(T, want):
    t = min(want, T)
    assert T % t == 0
    return t


def _trunk(x, mem, w):
    B, T, _ = x.shape
    M = mem.shape[1]
    assert T % MLSTM_CHUNK == 0 and T % GRID_W == 0
    x2 = x.reshape(B * T, D_MODEL)
    mem2 = mem.reshape(B * M, D_MODEL)
    xn, qm, ktm, vom, qa, ka, va, qc, gbr, gi, gf = _in_proj(x2, T, w, _tile(T, 512))
    hm = _mlstm(qm, ktm, vom, gi, gf, T, w)
    ha = _attention(qa, ka, va, T, _tile(T, 2048))
    kvc = _mem_kv(mem2, w)
    x1 = _merge(xn, hm, ha, qc, kvc, gbr, T, M, w, _tile(T, 512))
    y = _ffn(x1, T, w, _tile(T, 512))
    return y.reshape(B, T, D_MODEL)


def kernel(x_prompt, x_sample, mem_prompt, mem_sample, ln_in_g, ln_in_b, w_in, mlstm_gate_bias, mlstm_conv_w, mlstm_conv_b, mlstm_norm_g, attn_q_norm_g, attn_k_norm_g, w_mem_kv, w_branch_mlstm, w_branch_attn, w_branch_mem, w_out, ln1_g, ln1_b, w_ffn_up, ffn_conv_w, ffn_conv_b, w_ffn_down, ln2_g, ln2_b):
    assert DEPTH == 1 and x_prompt.shape[1] == x_sample.shape[1]
    w = _prep_weights(0, x_prompt.shape[1], ln_in_g, ln_in_b, w_in, mlstm_gate_bias, mlstm_conv_w,
                      mlstm_conv_b, mlstm_norm_g, attn_q_norm_g, attn_k_norm_g, w_mem_kv,
                      w_branch_mlstm, w_branch_attn, w_branch_mem, w_out, ln1_g, ln1_b, w_ffn_up,
                      ffn_conv_w, ffn_conv_b, w_ffn_down, ln2_g, ln2_b)
    return (_trunk(x_prompt, mem_prompt, w), _trunk(x_sample, mem_sample, w))
```

```python
import functools

import jax
import jax.numpy as jnp
from jax import lax
from jax.experimental import pallas as pl
from jax.experimental.pallas import tpu as pltpu

F32 = jnp.float32
BF16 = jnp.bfloat16

D_MODEL = 1024
DEPTH = 1
GRID_W = 64
MLSTM_HEADS = 4
MLSTM_DH = 128
MLSTM_W = MLSTM_HEADS * MLSTM_DH
MLSTM_CHUNK = 128
N_GATES = 4 * MLSTM_HEADS
ATTN_DH = 128
ATTN_HEADS = 8
KV_HEADS = 2
ATTN_GROUP = ATTN_HEADS // KV_HEADS
ATTN_W = ATTN_HEADS * ATTN_DH
KV_W = KV_HEADS * ATTN_DH
ROPE_AXIS_DIM = ATTN_DH // 2
ROPE_THETA = 10000.0
MEM_HEADS = 4
MEM_DH = 128
MEM_W = MEM_HEADS * MEM_DH
N_BRANCH = 3
D_FF = ((8 * D_MODEL // 3 + 127) // 128) * 128
CONV_W = 3
DEEPNORM_ALPHA = (2.0 * DEPTH) ** 0.25
LN_EPS = 1e-5
LOG2_E = 1.4426950408889634

LANES = 128
SUBLANES = 8
VMEM_LIMIT_BYTES = 56 * 1024 * 1024

SEG_QKM = (0, 2 * MLSTM_W)
SEG_VOM = (SEG_QKM[1], SEG_QKM[1] + 2 * MLSTM_W)
SEG_QA = (SEG_VOM[1], SEG_VOM[1] + ATTN_W)
SEG_KA = (SEG_QA[1], SEG_QA[1] + KV_W)
SEG_VA = (SEG_KA[1], SEG_KA[1] + KV_W)
SEG_QC = (SEG_VA[1], SEG_VA[1] + MEM_W)
SEG_GBR = (SEG_QC[1], SEG_QC[1] + N_BRANCH * D_MODEL)
W_CAT = SEG_GBR[1]
GATE_ROW_ORDER = tuple(g * MLSTM_HEADS + h for g in (0, 2, 1, 3) for h in range(MLSTM_HEADS))

FFN_CHUNK = 256
ATTN_UNIT_ROWS = 128


def _params(n_axes):
    return pltpu.CompilerParams(dimension_semantics=("arbitrary",) * n_axes,
                                vmem_limit_bytes=VMEM_LIMIT_BYTES)


def _const_spec(shape):
    nd = len(shape)
    return pl.BlockSpec(shape, lambda *_: (0,) * nd, pipeline_mode=pl.Buffered(1))


def _layer_norm(x, g, b):
    mu = jnp.mean(x, axis=-1, keepdims=True)
    xc = x - mu
    var = jnp.mean(xc * xc, axis=-1, keepdims=True)
    return xc * lax.rsqrt(var + LN_EPS) * g + b


def _sigmoid(x):
    return 1.0 / (1.0 + jnp.exp(-x))


def _log_sigmoid(x):
    return jnp.minimum(x, 0.0) - jnp.log(1.0 + jnp.exp(-jnp.abs(x)))


def _dot(a, b):
    return jnp.dot(a, b, preferred_element_type=F32)


def _dot_nt(a, b):
    return lax.dot_general(a, b, (((1,), (1,)), ((), ())), preferred_element_type=F32)


def _split3(x):
    x1 = x.astype(BF16)
    r1 = x - x1.astype(F32)
    x2 = r1.astype(BF16)
    r2 = r1 - x2.astype(F32)
    return x1, x2, r2.astype(BF16)


def _rms_rope(xh, cos_g, sin_g):
    ms = jnp.mean(xh * xh, axis=-1, keepdims=True)
    xh = xh * lax.rsqrt(ms + LN_EPS)
    return xh * cos_g + pltpu.roll(xh, LANES // 2, 1) * sin_g


def _proj_kernel(xp_ref, x_ref, xn_ref, lng_ref, lnb_ref, w_ref, wg_ref, gbr_row_ref,
                 cq_ref, sq_ref, ck_ref, sk_ref, cw_ref, cb_ref,
                 xno_ref, qm_ref, ktm_ref, vom_ref, qa_ref, ka_ref, va_ref, qc_ref, gbr_ref, gi_ref, gf_ref,
                 wgt_s, *, tiles_per_seq):
    tm = x_ref.shape[0]

    @pl.when(pl.program_id(0) == 0)
    def _():
        wgt_s[...] = wg_ref[...].astype(F32).T[:N_GATES].astype(BF16)

    lng = lng_ref[...]
    lnb = lnb_ref[...]
    xn = _layer_norm(x_ref[...], lng, lnb)
    xno_ref[...] = xn
    xb = xn.astype(BF16)

    def proj(lo, hi):
        return _dot(xb, w_ref[:, lo:hi])

    j = pl.program_id(0) % tiles_per_seq
    hr = xp_ref.shape[0]
    te = tm + 2 * hr
    xbe = jnp.concatenate([_layer_norm(xp_ref[...], lng, lnb).astype(BF16), xb,
                           _layer_norm(xn_ref[...], lng, lnb).astype(BF16)], axis=0)
    keep_prev = jnp.where(j > 0, 1.0, 0.0).astype(F32)
    keep_next = jnp.where(j < tiles_per_seq - 1, 1.0, 0.0).astype(F32)

    def conv_silu(lo, scale):
        cols = slice(lo, lo + MLSTM_W)
        ue = _dot(xbe, w_ref[:, SEG_QKM[0] + lo:SEG_QKM[0] + lo + MLSTM_W])
        ue = jnp.concatenate([ue[:hr] * keep_prev, ue[hr:hr + tm], ue[hr + tm:] * keep_next], axis=0)
        u_prev = pltpu.roll(ue, 1, 0)[hr:hr + tm]
        u_next = pltpu.roll(ue, te - 1, 0)[hr:hr + tm]
        y = u_prev * cw_ref[0:1, cols] + cb_ref[:, cols]
        y = y + ue[hr:hr + tm] * cw_ref[1:2, cols]
        y = y + u_next * cw_ref[2:3, cols]
        y = y * _sigmoid(y)
        return y if scale == 1.0 else y * scale

    def q_job():
        qm_ref[...] = conv_silu(0, 1.0).astype(BF16)

    def k_job():
        yk = conv_silu(MLSTM_W, MLSTM_DH ** -0.5)
        for a in range(tm // MLSTM_CHUNK):
            for h in range(MLSTM_HEADS):
                blk = yk[a * MLSTM_CHUNK:(a + 1) * MLSTM_CHUNK, h * MLSTM_DH:(h + 1) * MLSTM_DH]
                r0 = a * MLSTM_W + h * MLSTM_DH
                ktm_ref[r0:r0 + MLSTM_DH, :] = blk.T.astype(BF16)

    def plain_job(out_ref, dst, src, piece):
        def job():
            out_ref[:, dst:dst + piece] = proj(src, src + piece).astype(out_ref.dtype)
        return job

    def rope_job(out_ref, dst, src, c_ref, s_ref):
        def job():
            blk = proj(src, src + pair)
            for jj in range(2):
                xh = _rms_rope(blk[:, jj * ATTN_DH:(jj + 1) * ATTN_DH], c_ref[...], s_ref[...])
                out_ref[:, dst + jj * ATTN_DH:dst + (jj + 1) * ATTN_DH] = xh.astype(BF16)
        return job

    pair = 2 * ATTN_DH
    vector_jobs = [q_job, k_job]
    vector_jobs += [rope_job(qa_ref, c - SEG_QA[0], c, cq_ref, sq_ref) for c in range(SEG_QA[0], SEG_QA[1], pair)]
    vector_jobs += [rope_job(ka_ref, c - SEG_KA[0], c, ck_ref, sk_ref) for c in range(SEG_KA[0], SEG_KA[1], pair)]
    plain_jobs = []
    for out_ref, seg, piece in ((vom_ref, SEG_VOM, 512), (va_ref, SEG_VA, KV_W),
                                (qc_ref, SEG_QC, 512), (gbr_ref, SEG_GBR, 512)):
        plain_jobs += [plain_job(out_ref, c - seg[0], c, piece) for c in range(seg[0], seg[1], piece)]

    while vector_jobs or plain_jobs:
        if plain_jobs:
            plain_jobs.pop(0)()
        if vector_jobs:
            vector_jobs.pop(0)()

    grow = _dot_nt(wgt_s[...], xb)
    nd = N_GATES // 2
    for j in range(tm // LANES):
        blk = grow[:, j * LANES:(j + 1) * LANES] + gbr_row_ref[...]
        gi_ref[j * nd:(j + 1) * nd, :] = blk[:nd]
        gf_ref[j * nd:(j + 1) * nd, :] = blk[nd:]


def _in_proj(x2, T, w, tm):
    n = x2.shape[0]
    grid = (n // tm,)
    tiles_per_seq = T // tm
    row = lambda i: (i, 0)
    hb = 2 * SUBLANES
    nhb = n // hb
    out_shapes = (
        jax.ShapeDtypeStruct((n, D_MODEL), F32),
        jax.ShapeDtypeStruct((n, MLSTM_W), BF16),
        jax.ShapeDtypeStruct((n // MLSTM_CHUNK * MLSTM_W, MLSTM_CHUNK), BF16),
        jax.ShapeDtypeStruct((n, 2 * MLSTM_W), BF16),
        jax.ShapeDtypeStruct((n, ATTN_W), BF16),
        jax.ShapeDtypeStruct((n, KV_W), BF16),
        jax.ShapeDtypeStruct((n, KV_W), BF16),
        jax.ShapeDtypeStruct((n, MEM_W), BF16),
        jax.ShapeDtypeStruct((n, N_BRANCH * D_MODEL), BF16),
        jax.ShapeDtypeStruct((n // LANES * (N_GATES // 2), LANES), F32),
        jax.ShapeDtypeStruct((n // LANES * (N_GATES // 2), LANES), F32),
    )
    out_specs = (
        pl.BlockSpec((tm, D_MODEL), row),
        pl.BlockSpec((tm, MLSTM_W), row),
        pl.BlockSpec((tm // MLSTM_CHUNK * MLSTM_W, MLSTM_CHUNK), row),
        pl.BlockSpec((tm, 2 * MLSTM_W), row),
        pl.BlockSpec((tm, ATTN_W), row),
        pl.BlockSpec((tm, KV_W), row),
        pl.BlockSpec((tm, KV_W), row),
        pl.BlockSpec((tm, MEM_W), row),
        pl.BlockSpec((tm, N_BRANCH * D_MODEL), row),
        pl.BlockSpec((tm // LANES * (N_GATES // 2), LANES), row),
        pl.BlockSpec((tm // LANES * (N_GATES // 2), LANES), row),
    )
    in_specs = [
        pl.BlockSpec((hb, D_MODEL), lambda i: (jnp.maximum(i * (tm // hb) - 1, 0), 0)),
        pl.BlockSpec((tm, D_MODEL), row),
        pl.BlockSpec((hb, D_MODEL), lambda i: (jnp.minimum((i + 1) * (tm // hb), nhb - 1), 0)),
        _const_spec((1, D_MODEL)), _const_spec((1, D_MODEL)),
        _const_spec((D_MODEL, W_CAT)), _const_spec((D_MODEL, LANES)),
        _const_spec((N_GATES, LANES)),
    ] + [pl.BlockSpec((tm, LANES), lambda i: (i % tiles_per_seq, 0))] * 4 + [
        _const_spec((CONV_W, 2 * MLSTM_W)), _const_spec((1, 2 * MLSTM_W)),
    ]
    return pl.pallas_call(
        functools.partial(_proj_kernel, tiles_per_seq=tiles_per_seq),
        grid=grid, in_specs=in_specs, out_specs=out_specs, out_shape=out_shapes,
        scratch_shapes=[pltpu.VMEM((N_GATES, D_MODEL), BF16)],
        compiler_params=_params(1), name="in_proj",
    )(x2, x2, x2, w["ln_in_g"], w["ln_in_b"], w["w_cat"], w["w_gate"], w["gb_row"],
      w["rope_cos_q"], w["rope_sin_q"], w["rope_cos_k"], w["rope_sin_k"],
      w["mlstm_conv_w"], w["mlstm_conv_b"])


N_DIRHEADS = 2 * MLSTM_HEADS
STAT_GROUPS = 6
STAT_BLOCK = (0, 0, 0, 1, 1, 1)
STAT_BLOCKS = 2


def _mlstm_select():
    lane = jnp.arange(LANES)
    grp, j = lane // N_DIRHEADS, lane % N_DIRHEADS
    blk = jnp.array(STAT_BLOCK + (-1,) * (LANES // N_DIRHEADS - STAT_GROUPS))[grp]
    out_blk = jnp.arange(STAT_BLOCKS * LANES) // LANES
    sel = (j[None, :, None] == jnp.arange(N_DIRHEADS)[:, None, None]) & (blk[None, :, None] == out_blk[None, None, :])
    return sel.astype(BF16)


def _mlstm_kernel(q_ref, kt_ref, vom_ref, gi_ref, gf_ref, sel_ref, ng_ref, hm_ref,
                  ktw_s, lhs_s, e_s, stat_s, gt_s, ma_s, mpf_s, mpb_s, r_s, wa_s, sp_s,
                  hf_s, st_s):
    T = q_ref.shape[0]
    L = MLSTM_CHUNK
    NC = T // L
    DH = MLSTM_DH
    H = MLSTM_HEADS

    ri = lax.broadcasted_iota(jnp.int32, (L, L), 0)
    ci = lax.broadcasted_iota(jnp.int32, (L, L), 1)
    causal = ci <= ri
    anti = ci >= ri
    tril = jnp.where(causal, 1.0, 0.0).astype(BF16)
    triu = jnp.where(anti, 1.0, 0.0).astype(BF16)
    ones_ll = jnp.ones((L, L), BF16)

    R = NC * N_DIRHEADS
    row_r = lax.broadcasted_iota(jnp.int32, (R, L), 0)
    lane_r = lax.broadcasted_iota(jnp.int32, (R, L), 1)
    is_fwd = (row_r % N_DIRHEADS) < H
    is_fwd8 = lax.broadcasted_iota(jnp.int32, (N_DIRHEADS, L), 0) < H

    li = gi_ref[...]
    y1, y2, y3 = _split3(_log_sigmoid(gf_ref[...]))

    def lane_sums(m):
        return _dot(y1, m) + _dot(y2, m) + _dot(y3, m)

    bl = jnp.where(is_fwd, lane_sums(triu), lane_sums(tril))
    gt = lane_sums(ones_ll)
    r = li - bl
    pm = r
    sm = r
    for sh in (1, 2, 4, 8, 16, 32, 64):
        pm = jnp.maximum(pm, jnp.where(lane_r >= sh, pltpu.roll(pm, sh, 1), -jnp.inf))
        sm = jnp.maximum(sm, jnp.where(lane_r < L - sh, pltpu.roll(sm, L - sh, 1), -jnp.inf))
    cm = jnp.where(is_fwd, pm, sm)
    a = gt - bl + li
    ma = jnp.broadcast_to(jnp.max(a, axis=-1, keepdims=True), (R, L))
    gt_s[...] = gt
    ma_s[...] = ma

    def chunk_rows(c):
        return pl.ds(pl.multiple_of(c * N_DIRHEADS, N_DIRHEADS), N_DIRHEADS)

    def stabiliser_scan(i, m):
        rf = chunk_rows(i)
        rb = chunk_rows(NC - 1 - i)
        mpf_s[rf, :] = m
        mpb_s[rb, :] = m
        gt8 = jnp.where(is_fwd8, gt_s[rf, :], gt_s[rb, :])
        ma8 = jnp.where(is_fwd8, ma_s[rf, :], ma_s[rb, :])
        return jnp.maximum(gt8 + m, ma8)

    lax.fori_loop(0, NC, stabiliser_scan, jnp.zeros((N_DIRHEADS, L), F32))

    m_prev = jnp.where(is_fwd, mpf_s[...], mpb_s[...])
    m_new = jnp.maximum(gt + m_prev, ma)
    sp_s[...] = jnp.exp(gt + m_prev - m_new)
    wa_s[...] = jnp.exp(a - m_new)
    r_s[...] = r * LOG2_E
    ma_s[...] = m_prev * LOG2_E
    u = -jnp.maximum(m_prev, cm)
    stat_vals = list(_split3(u * LOG2_E)) + list(_split3(jnp.exp(u - bl)))
    for k, t in enumerate(stat_vals):
        stat_s[k] = t.astype(F32)
    stat_pad = jnp.zeros((L - STAT_GROUPS * N_DIRHEADS, L), F32)

    def prologue(c, carry):
        r0 = pl.multiple_of(c * L, L)
        rows = pl.ds(r0, L)
        rows8 = chunk_rows(c)
        stats = [stat_s[k, rows8, :] for k in range(STAT_GROUPS)] + [stat_pad]
        stats_t = jnp.concatenate(stats, axis=0).T.astype(BF16)
        wa = wa_s[rows8, :]
        r8 = r_s[rows8, :]
        mp8 = ma_s[rows8, :]
        kt_rows = pl.multiple_of(c * MLSTM_W, MLSTM_W)
        for h in range(H):
            hs = slice(h * DH, (h + 1) * DH)
            q = q_ref[rows, hs]
            kt_b = kt_ref[pl.ds(kt_rows + h * DH, DH), :]
            s = _dot(q, kt_b)
            kt = kt_b.astype(F32)
            for d in range(2):
                j = H * d + h
                bc = _dot(stats_t, sel_ref[j])
                u_b = bc[:, :L]
                p = jnp.where(causal if d == 0 else anti, jnp.exp2(u_b + r8[j:j + 1, :]), 0.0)
                w_inter = jnp.exp2(u_b + mp8[j:j + 1, :])
                lhs_s[d, rows, 2 * h * DH:(2 * h + 1) * DH] = (s * p).astype(BF16)
                lhs_s[d, rows, (2 * h + 1) * DH:(2 * h + 2) * DH] = w_inter.astype(BF16) * q
                e_s[j, rows, :] = bc[:, L:]
                ktw_s[d, c, hs, :] = (kt * wa[j:j + 1, :]).astype(BF16)
        return carry

    lax.fori_loop(0, NC, prologue, 0, unroll=4)

    ones_blk = jnp.ones((L, DH), BF16)
    ng = ng_ref[...]

    def run_direction(d):
        st_s[...] = jnp.zeros_like(st_s)

        def body(i, carry):
            c = i if d == 0 else NC - 1 - i
            r0 = pl.multiple_of(c * L, L)
            rows = pl.ds(r0, L)
            for h in range(H):
                j = H * d + h
                hs = slice(h * DH, (h + 1) * DH)
                lhs = lhs_s[d, rows, 2 * h * DH:(2 * h + 2) * DH]
                vaug = jnp.concatenate([vom_ref[rows, hs], ones_blk], axis=1)
                c_prev = st_s[h]
                tot = _dot(lhs, jnp.concatenate([vaug, c_prev.astype(BF16)], axis=0))
                h_out = tot[:, :DH] / jnp.maximum(jnp.abs(tot[:, DH:]), e_s[j, rows, :])
                sp = sp_s[pl.ds(c * N_DIRHEADS + j, 1), :]
                st_s[h] = jnp.concatenate([sp, sp], axis=1) * c_prev + _dot(ktw_s[d, c, hs, :], vaug)

                if d == 0:
                    hf_s[rows, hs] = h_out
                else:
                    hh = hf_s[rows, hs] + h_out
                    mu = jnp.mean(hh, axis=-1, keepdims=True)
                    hc = hh - mu
                    var = jnp.mean(hc * hc, axis=-1, keepdims=True)
                    hn = hc * lax.rsqrt(var + LN_EPS) * ng[:, hs]
                    o = vom_ref[rows, MLSTM_W + h * DH:MLSTM_W + (h + 1) * DH].astype(F32)
                    hm_ref[rows, hs] = (hn * _sigmoid(o)).astype(BF16)
            return carry

        lax.fori_loop(0, NC, body, 0, unroll=4)

    run_direction(0)
    run_direction(1)


def _mlstm(qm, ktm, vom, gi, gf, T, w):
    n = qm.shape[0]
    B = n // T
    NC = T // MLSTM_CHUNK
    seq = lambda b: (b, 0)
    gate_major = pltpu.VMEM((NC * N_DIRHEADS, MLSTM_CHUNK), F32)
    return pl.pallas_call(
        _mlstm_kernel, grid=(B,),
        in_specs=[
            pl.BlockSpec((T, MLSTM_W), seq),
            pl.BlockSpec((NC * MLSTM_W, MLSTM_CHUNK), seq),
            pl.BlockSpec((T, 2 * MLSTM_W), seq),
            pl.BlockSpec((NC * N_DIRHEADS, MLSTM_CHUNK), seq),
            pl.BlockSpec((NC * N_DIRHEADS, MLSTM_CHUNK), seq),
            _const_spec((N_DIRHEADS, LANES, STAT_BLOCKS * MLSTM_CHUNK)),
            _const_spec((1, MLSTM_W)),
        ],
        out_specs=pl.BlockSpec((T, MLSTM_W), seq),
        out_shape=jax.ShapeDtypeStruct((n, MLSTM_W), BF16),
        scratch_shapes=[
            pltpu.VMEM((2, NC, MLSTM_W, MLSTM_CHUNK), BF16),
            pltpu.VMEM((2, T, 2 * MLSTM_W), BF16),
            pltpu.VMEM((N_DIRHEADS, T, LANES), F32),
            pltpu.VMEM((STAT_GROUPS, NC * N_DIRHEADS, MLSTM_CHUNK), F32),
            gate_major, gate_major, gate_major, gate_major,
            gate_major, gate_major, gate_major,
            pltpu.VMEM((T, MLSTM_W), F32),
            pltpu.VMEM((MLSTM_HEADS, MLSTM_DH, 2 * MLSTM_DH), F32),
        ],
        compiler_params=_params(1), name="mlstm",
    )(qm, ktm, vom, gi, gf, w["mlstm_sel"], w["mlstm_norm_g"])


def _attn_kernel(q_ref, k_ref, v_ref, o_ref, vaug_s):
    @pl.when(pl.program_id(2) == 0)
    def _():
        vaug_s[:, :ATTN_DH] = v_ref[...]
        vaug_s[:, ATTN_DH:] = jnp.ones((v_ref.shape[0], ATTN_DH), BF16)

    k = k_ref[...]
    tq = q_ref.shape[0]
    rows_per_unit = min(tq, ATTN_UNIT_ROWS)
    for g in range(ATTN_GROUP):
        hs = slice(g * ATTN_DH, (g + 1) * ATTN_DH)
        for r0 in range(0, tq, rows_per_unit):
            rs = slice(r0, r0 + rows_per_unit)
            s = _dot_nt(q_ref[rs, hs], k)
            m = jnp.max(s, axis=-1, keepdims=True)
            p = jnp.exp2(s - m)
            oa = _dot(p.astype(BF16), vaug_s[...])
            o_ref[rs, hs] = (oa[:, :ATTN_DH] / oa[:, ATTN_DH:ATTN_DH + 1]).astype(o_ref.dtype)


def _attention(qa, ka, va, T, tq):
    n = qa.shape[0]
    B = n // T
    nq = T // tq
    gw = ATTN_GROUP * ATTN_DH
    return pl.pallas_call(
        _attn_kernel, grid=(B, KV_HEADS, nq),
        in_specs=[
            pl.BlockSpec((tq, gw), lambda b, h, i: (b * nq + i, h)),
            pl.BlockSpec((T, ATTN_DH), lambda b, h, i: (b, h)),
            pl.BlockSpec((T, ATTN_DH), lambda b, h, i: (b, h)),
        ],
        out_specs=pl.BlockSpec((tq, gw), lambda b, h, i: (b * nq + i, h)),
        out_shape=jax.ShapeDtypeStruct((n, ATTN_W), BF16),
        scratch_shapes=[pltpu.VMEM((T, 2 * ATTN_DH), BF16)],
        compiler_params=_params(3), name="gqa_attn",
    )(qa, ka, va)


def _mem_kv_kernel(m_ref, w_ref, o_ref):
    o_ref[...] = _dot(m_ref[...].astype(BF16), w_ref[...]).astype(o_ref.dtype)


def _mem_kv(mem2, w):
    n = mem2.shape[0]
    tm = _tile(n, 1024)
    return pl.pallas_call(
        _mem_kv_kernel, grid=(n // tm,),
        in_specs=[pl.BlockSpec((tm, D_MODEL), lambda b: (b, 0)),
                  _const_spec((D_MODEL, 2 * MEM_W))],
        out_specs=pl.BlockSpec((tm, 2 * MEM_W), lambda b: (b, 0)),
        out_shape=jax.ShapeDtypeStruct((n, 2 * MEM_W), BF16),
        compiler_params=_params(1), name="mem_kv",
    )(mem2, w["w_mem_kv"])


def _merge_kernel(xn_ref, hm_ref, ha_ref, qc_ref, kv_ref, gbr_ref,
                  wbm_ref, wba_ref, wbc_ref, wo_ref, l1g_ref, l1b_ref, x1_ref):
    ones_blk = jnp.ones((kv_ref.shape[0], MEM_DH), BF16)
    hc_parts = []
    for h in range(MEM_HEADS):
        hs = slice(h * MEM_DH, (h + 1) * MEM_DH)
        kc = kv_ref[:, hs]
        vaug = jnp.concatenate([kv_ref[:, MEM_W + h * MEM_DH:MEM_W + (h + 1) * MEM_DH], ones_blk], axis=1)
        s = _dot_nt(qc_ref[:, hs], kc)
        p = jnp.exp2(s - jnp.max(s, axis=-1, keepdims=True))
        oa = _dot(p.astype(BF16), vaug)
        hc_parts.append((oa[:, :MEM_DH] / oa[:, MEM_DH:MEM_DH + 1]).astype(BF16))
    hc = jnp.concatenate(hc_parts, axis=1)

    def gate(j):
        return _sigmoid(gbr_ref[:, j * D_MODEL:(j + 1) * D_MODEL].astype(F32))

    merged = gate(0) * _dot(hm_ref[...], wbm_ref[...])
    merged = merged + gate(1) * _dot(ha_ref[...], wba_ref[...])
    merged = merged + gate(2) * _dot(hc, wbc_ref[...])
    mixed = _dot(merged.astype(BF16), wo_ref[...])
    x1_ref[...] = _layer_norm(DEEPNORM_ALPHA * xn_ref[...] + mixed, l1g_ref[...], l1b_ref[...])


def _merge(xn, hm, ha, qc, kvc, gbr, T, M, w, tm):
    n = xn.shape[0]
    tiles_per_seq = T // tm
    row = lambda i: (i, 0)
    return pl.pallas_call(
        _merge_kernel, grid=(n // tm,),
        in_specs=[
            pl.BlockSpec((tm, D_MODEL), row),
            pl.BlockSpec((tm, MLSTM_W), row),
            pl.BlockSpec((tm, ATTN_W), row),
            pl.BlockSpec((tm, MEM_W), row),
            pl.BlockSpec((M, 2 * MEM_W), lambda i: (i // tiles_per_seq, 0)),
            pl.BlockSpec((tm, N_BRANCH * D_MODEL), row),
            _const_spec((MLSTM_W, D_MODEL)), _const_spec((ATTN_W, D_MODEL)),
            _const_spec((MEM_W, D_MODEL)), _const_spec((D_MODEL, D_MODEL)),
            _const_spec((1, D_MODEL)), _const_spec((1, D_MODEL)),
        ],
        out_specs=pl.BlockSpec((tm, D_MODEL), row),
        out_shape=jax.ShapeDtypeStruct((n, D_MODEL), F32),
        compiler_params=_params(1), name="merge_out",
    )(xn, hm, ha, qc, kvc, gbr, w["w_branch_mlstm"],
      w["w_branch_attn"], w["w_branch_mem"], w["w_out"], w["ln1_g"], w["ln1_b"])


GELU_C = 0.7978845608028654
GELU_C3 = GELU_C * 0.044715


def _gelu_tanh_x2(x):
    return x * (1.0 + jnp.tanh(x * (GELU_C + GELU_C3 * (x * x))))


def _ffn_kernel(xp_ref, x_ref, xn_ref, wup_ref, cw_ref, cb_ref, wdn_ref, l2g_ref, l2b_ref,
                y_ref, xe_s, act_s, *, tiles_per_seq):
    tm = x_ref.shape[0]
    i = pl.program_id(0)
    j = i % tiles_per_seq
    prev = xp_ref[...] * jnp.where(j > 0, 1.0, 0.0).astype(F32)
    nxt = xn_ref[...] * jnp.where(j < tiles_per_seq - 1, 1.0, 0.0).astype(F32)
    xe_s[...] = jnp.concatenate([prev, x_ref[...], nxt], axis=0).astype(BF16)
    te = tm + 2 * SUBLANES

    def conv(u, cw, cb):
        up = pltpu.roll(u, 1, 0)[SUBLANES:SUBLANES + tm, :]
        uc = u[SUBLANES:SUBLANES + tm, :]
        un = pltpu.roll(u, te - 1, 0)[SUBLANES:SUBLANES + tm, :]
        y = up * cw[0:1, :] + cb
        y = y + uc * cw[1:2, :]
        return y + un * cw[2:3, :]

    def up_conv(lo):
        cols = slice(lo, lo + FFN_CHUNK)
        return conv(_dot(xe_s[...], wup_ref[:, cols]), cw_ref[:, cols], cb_ref[:, cols])

    for c in range(0, D_FF, FFN_CHUNK):
        act_s[:, c:c + FFN_CHUNK] = (_gelu_tanh_x2(up_conv(c)) * up_conv(D_FF + c)).astype(BF16)

    ff = _dot(act_s[...], wdn_ref[...])
    y_ref[...] = _layer_norm(DEEPNORM_ALPHA * x_ref[...] + ff, l2g_ref[...], l2b_ref[...])


def _ffn(x1, T, w, tm):
    n = x1.shape[0]
    tiles_per_seq = T // tm
    r8 = tm // SUBLANES
    nblk8 = n // SUBLANES
    row = lambda i: (i, 0)
    return pl.pallas_call(
        functools.partial(_ffn_kernel, tiles_per_seq=tiles_per_seq), grid=(n // tm,),
        in_specs=[
            pl.BlockSpec((SUBLANES, D_MODEL), lambda i: (jnp.maximum(i * r8 - 1, 0), 0)),
            pl.BlockSpec((tm, D_MODEL), row),
            pl.BlockSpec((SUBLANES, D_MODEL), lambda i: (jnp.minimum((i + 1) * r8, nblk8 - 1), 0)),
            _const_spec((D_MODEL, 2 * D_FF)),
            _const_spec((CONV_W, 2 * D_FF)),
            _const_spec((1, 2 * D_FF)),
            _const_spec((D_FF, D_MODEL)),
            _const_spec((1, D_MODEL)), _const_spec((1, D_MODEL)),
        ],
        out_specs=pl.BlockSpec((tm, D_MODEL), row),
        out_shape=jax.ShapeDtypeStruct((n, D_MODEL), F32),
        scratch_shapes=[pltpu.VMEM((tm + 2 * SUBLANES, D_MODEL), BF16),
                        pltpu.VMEM((tm, D_FF), BF16)],
        compiler_params=_params(1), name="conv_ffn",
    )(x1, x1, x1, w["w_ffn_up"], w["ffn_conv_w"], w["ffn_conv_b"], w["w_ffn_down"],
      w["ln2_g"], w["ln2_b"])


def _rope_tables(T):
    rows = T // GRID_W
    row = jnp.repeat(jnp.arange(rows, dtype=F32), GRID_W)
    col = jnp.tile(jnp.arange(GRID_W, dtype=F32), rows)
    inv_freq = ROPE_THETA ** (-jnp.arange(0, ROPE_AXIS_DIM, 2, dtype=F32) / ROPE_AXIS_DIM)
    ang_r = row[:, None] * inv_freq
    ang_c = col[:, None] * inv_freq
    cos = jnp.concatenate([jnp.cos(ang_r), jnp.cos(ang_c)] * 2, axis=-1)
    sin = jnp.concatenate([-jnp.sin(ang_r), -jnp.sin(ang_c), jnp.sin(ang_r), jnp.sin(ang_c)], axis=-1)
    return cos, sin


def _rope_head_perm():
    p = jnp.arange(ATTN_DH)
    quarter = ROPE_AXIS_DIM // 2
    half, axis, j = p // ROPE_AXIS_DIM, (p % ROPE_AXIS_DIM) // quarter, p % quarter
    return axis * ROPE_AXIS_DIM + half * quarter + j


def _gained_tables(cos, sin, g, scale):
    g = g.astype(F32)[_rope_head_perm()]
    return cos * (g * scale), sin * (jnp.roll(g, ATTN_DH // 2) * scale)


def _prep_weights(l, T, ln_in_g, ln_in_b, w_in, mlstm_gate_bias, mlstm_conv_w, mlstm_conv_b,
                  mlstm_norm_g, attn_q_norm_g, attn_k_norm_g, w_mem_kv, w_branch_mlstm,
                  w_branch_attn, w_branch_mem, w_out, ln1_g, ln1_b, w_ffn_up, ffn_conv_w,
                  ffn_conv_b, w_ffn_down, ln2_g, ln2_b):
    wi = w_in[l]
    g0 = 4 * MLSTM_W
    g1 = g0 + N_GATES
    w_gate = wi[:, g0:g1]
    qk0 = g1
    qk1 = qk0 + ATTN_W + KV_W
    qc0 = qk1 + KV_W
    qc1 = qc0 + MEM_W
    quarter = ROPE_AXIS_DIM // 2
    w_qk = wi[:, qk0:qk1].reshape(D_MODEL, ATTN_HEADS + KV_HEADS, 2, 2, quarter)
    w_qk = jnp.swapaxes(w_qk, 2, 3).reshape(D_MODEL, ATTN_W + KV_W)
    w_qc = wi[:, qc0:qc1] * (MEM_DH ** -0.5 * LOG2_E)
    w_cat = jnp.concatenate([wi[:, :g0], w_qk, wi[:, qk1:qc0], w_qc, wi[:, qc1:]], axis=1).astype(BF16)
    gate_rows = jnp.array(GATE_ROW_ORDER)
    w_gate = w_gate[:, gate_rows]
    gb = mlstm_gate_bias[l].astype(F32)[gate_rows]
    cos, sin = _rope_tables(T)
    cos_q, sin_q = _gained_tables(cos, sin, attn_q_norm_g[l], ATTN_DH ** -0.5 * LOG2_E)
    cos_k, sin_k = _gained_tables(cos, sin, attn_k_norm_g[l], 1.0)
    r2 = lambda a: a.reshape(1, -1).astype(F32)
    ffn_half = jnp.where(jnp.arange(2 * D_FF) >= D_FF, 0.5, 1.0).astype(F32)[None, :]
    return {
        "ln_in_g": r2(ln_in_g), "ln_in_b": r2(ln_in_b),
        "w_cat": w_cat, "w_gate": jnp.pad(w_gate, ((0, 0), (0, LANES - N_GATES))).astype(BF16),
        "gb_row": jnp.broadcast_to(gb[:, None], (N_GATES, LANES)),
        "mlstm_sel": _mlstm_select(),
        "rope_cos_q": cos_q, "rope_sin_q": sin_q, "rope_cos_k": cos_k, "rope_sin_k": sin_k,
        "mlstm_conv_w": mlstm_conv_w[l].astype(F32), "mlstm_conv_b": r2(mlstm_conv_b[l]),
        "mlstm_norm_g": r2(mlstm_norm_g[l]),
        "w_mem_kv": w_mem_kv[l].astype(BF16),
        "w_branch_mlstm": w_branch_mlstm[l].astype(BF16),
        "w_branch_attn": w_branch_attn[l].astype(BF16),
        "w_branch_mem": w_branch_mem[l].astype(BF16),
        "w_out": w_out[l].astype(BF16),
        "ln1_g": r2(ln1_g[l]), "ln1_b": r2(ln1_b[l]),
        "w_ffn_up": w_ffn_up[l].astype(BF16),
        "ffn_conv_w": ffn_conv_w[l].astype(F32) * ffn_half,
        "ffn_conv_b": r2(ffn_conv_b[l]) * ffn_half,
        "w_ffn_down": w_ffn_down[l].astype(BF16),
        "ln2_g": r2(ln2_g[l]), "ln2_b": r2(ln2_b[l]),
    }


def _tile(T, want):
    t = min(want, T)
    assert T % t == 0
    return t


def _trunk(x, mem, w):
    B, T, _ = x.shape
    M = mem.shape[1]
    assert T % MLSTM_CHUNK == 0 and T % GRID_W == 0
    x2 = x.reshape(B * T, D_MODEL)
    mem2 = mem.reshape(B * M, D_MODEL)
    xn, qm, ktm, vom, qa, ka, va, qc, gbr, gi, gf = _in_proj(x2, T, w, _tile(T, 512))
    hm = _mlstm(qm, ktm, vom, gi, gf, T, w)
    ha = _attention(qa, ka, va, T, _tile(T, 2048))
    kvc = _mem_kv(mem2, w)
    x1 = _merge(xn, hm, ha, qc, kvc, gbr, T, M, w, _tile(T, 512))
    y = _ffn(x1, T, w, _tile(T, 512))
    return y.reshape(B, T, D_MODEL)


def kernel(x_prompt, x_sample, mem_prompt, mem_sample, ln_in_g, ln_in_b, w_in, mlstm_gate_bias, mlstm_conv_w, mlstm_conv_b, mlstm_norm_g, attn_q_norm_g, attn_k_norm_g, w_mem_kv, w_branch_mlstm, w_branch_attn, w_branch_mem, w_out, ln1_g, ln1_b, w_ffn_up, ffn_conv_w, ffn_conv_b, w_ffn_down, ln2_g, ln2_b):
    assert DEPTH == 1 and x_prompt.shape[1] == x_sample.shape[1]
    w = _prep_weights(0, x_prompt.shape[1], ln_in_g, ln_in_b, w_in, mlstm_gate_bias, mlstm_conv_w,
                      mlstm_conv_b, mlstm_norm_g, attn_q_norm_g, attn_k_norm_g, w_mem_kv,
                      w_branch_mlstm, w_branch_attn, w_branch_mem, w_out, ln1_g, ln1_b, w_ffn_up,
                      ffn_conv_w, ffn_conv_b, w_ffn_down, ln2_g, ln2_b)
    return (_trunk(x_prompt, mem_prompt, w), _trunk(x_sample, mem_sample, w))
```

```python
import functools

import jax
import jax.numpy as jnp
from jax import lax
from jax.experimental import pallas as pl
from jax.experimental.pallas import tpu as pltpu

F32 = jnp.float32
BF16 = jnp.bfloat16

D_MODEL = 1024
DEPTH = 1
GRID_W = 64
MLSTM_HEADS = 4
MLSTM_DH = 128
MLSTM_W = MLSTM_HEADS * MLSTM_DH
MLSTM_CHUNK = 128
N_GATES = 4 * MLSTM_HEADS
ATTN_DH = 128
ATTN_HEADS = 8
KV_HEADS = 2
ATTN_GROUP = ATTN_HEADS // KV_HEADS
ATTN_W = ATTN_HEADS * ATTN_DH
KV_W = KV_HEADS * ATTN_DH
ROPE_AXIS_DIM = ATTN_DH // 2
ROPE_THETA = 10000.0
MEM_HEADS = 4
MEM_DH = 128
MEM_W = MEM_HEADS * MEM_DH
N_BRANCH = 3
D_FF = ((8 * D_MODEL // 3 + 127) // 128) * 128
CONV_W = 3
DEEPNORM_ALPHA = (2.0 * DEPTH) ** 0.25
LN_EPS = 1e-5
LOG2_E = 1.4426950408889634

LANES = 128
SUBLANES = 8
VMEM_LIMIT_BYTES = 56 * 1024 * 1024

SEG_QKM = (0, 2 * MLSTM_W)
SEG_VOM = (SEG_QKM[1], SEG_QKM[1] + 2 * MLSTM_W)
SEG_QA = (SEG_VOM[1], SEG_VOM[1] + ATTN_W)
SEG_KA = (SEG_QA[1], SEG_QA[1] + KV_W)
SEG_VA = (SEG_KA[1], SEG_KA[1] + KV_W)
SEG_QC = (SEG_VA[1], SEG_VA[1] + MEM_W)
SEG_GBR = (SEG_QC[1], SEG_QC[1] + N_BRANCH * D_MODEL)
W_CAT = SEG_GBR[1]
GATE_ROW_ORDER = tuple(g * MLSTM_HEADS + h for g in (0, 2, 1, 3) for h in range(MLSTM_HEADS))

FFN_CHUNK = 256
TAIL_ROWS = 128
ATTN_UNIT_ROWS = 128


def _params(n_axes):
    return pltpu.CompilerParams(dimension_semantics=("arbitrary",) * n_axes,
                                vmem_limit_bytes=VMEM_LIMIT_BYTES)


def _const_spec(shape):
    nd = len(shape)
    return pl.BlockSpec(shape, lambda *_: (0,) * nd, pipeline_mode=pl.Buffered(1))


def _layer_norm(x, g, b):
    mu = jnp.mean(x, axis=-1, keepdims=True)
    xc = x - mu
    var = jnp.mean(xc * xc, axis=-1, keepdims=True)
    return xc * lax.rsqrt(var + LN_EPS) * g + b


def _sigmoid(x):
    return 1.0 / (1.0 + jnp.exp(-x))


def _log_sigmoid(x):
    return jnp.minimum(x, 0.0) - jnp.log(1.0 + jnp.exp(-jnp.abs(x)))


def _dot(a, b):
    return jnp.dot(a, b, preferred_element_type=F32)


def _dot_nt(a, b):
    return lax.dot_general(a, b, (((1,), (1,)), ((), ())), preferred_element_type=F32)


def _split3(x):
    x1 = x.astype(BF16)
    r1 = x - x1.astype(F32)
    x2 = r1.astype(BF16)
    r2 = r1 - x2.astype(F32)
    return x1, x2, r2.astype(BF16)


def _rms_rope(xh, cos_g, sin_g):
    ms = jnp.mean(xh * xh, axis=-1, keepdims=True)
    xh = xh * lax.rsqrt(ms + LN_EPS)
    return xh * cos_g + pltpu.roll(xh, LANES // 2, 1) * sin_g


def _proj_kernel(xp_ref, x_ref, xn_ref, lng_ref, lnb_ref, w_ref, wg_ref, gbr_row_ref,
                 cq_ref, sq_ref, ck_ref, sk_ref, cw_ref, cb_ref,
                 xno_ref, qm_ref, ktm_ref, vom_ref, qa_ref, ka_ref, va_ref, qc_ref, gbr_ref, gi_ref, gf_ref,
                 wgt_s, *, tiles_per_seq):
    tm = x_ref.shape[0]

    @pl.when(pl.program_id(0) == 0)
    def _():
        wgt_s[...] = wg_ref[...].astype(F32).T[:N_GATES].astype(BF16)

    lng = lng_ref[...]
    lnb = lnb_ref[...]
    xn = _layer_norm(x_ref[...], lng, lnb)
    xno_ref[...] = xn
    xb = xn.astype(BF16)

    def proj(lo, hi):
        return _dot(xb, w_ref[:, lo:hi])

    j = pl.program_id(0) % tiles_per_seq
    hr = xp_ref.shape[0]
    te = tm + 2 * hr
    xbe = jnp.concatenate([_layer_norm(xp_ref[...], lng, lnb).astype(BF16), xb,
                           _layer_norm(xn_ref[...], lng, lnb).astype(BF16)], axis=0)
    keep_prev = jnp.where(j > 0, 1.0, 0.0).astype(F32)
    keep_next = jnp.where(j < tiles_per_seq - 1, 1.0, 0.0).astype(F32)

    def conv_silu(lo, scale):
        cols = slice(lo, lo + MLSTM_W)
        ue = _dot(xbe, w_ref[:, SEG_QKM[0] + lo:SEG_QKM[0] + lo + MLSTM_W])
        ue = jnp.concatenate([ue[:hr] * keep_prev, ue[hr:hr + tm], ue[hr + tm:] * keep_next], axis=0)
        u_prev = pltpu.roll(ue, 1, 0)[hr:hr + tm]
        u_next = pltpu.roll(ue, te - 1, 0)[hr:hr + tm]
        y = u_prev * cw_ref[0:1, cols] + cb_ref[:, cols]
        y = y + ue[hr:hr + tm] * cw_ref[1:2, cols]
        y = y + u_next * cw_ref[2:3, cols]
        y = y * _sigmoid(y)
        return y if scale == 1.0 else y * scale

    def q_job():
        qm_ref[...] = conv_silu(0, 1.0).astype(BF16)

    def k_job():
        yk = conv_silu(MLSTM_W, MLSTM_DH ** -0.5)
        for a in range(tm // MLSTM_CHUNK):
            for h in range(MLSTM_HEADS):
                blk = yk[a * MLSTM_CHUNK:(a + 1) * MLSTM_CHUNK, h * MLSTM_DH:(h + 1) * MLSTM_DH]
                r0 = a * MLSTM_W + h * MLSTM_DH
                ktm_ref[r0:r0 + MLSTM_DH, :] = blk.T.astype(BF16)

    def plain_job(out_ref, dst, src, piece):
        def job():
            out_ref[:, dst:dst + piece] = proj(src, src + piece).astype(out_ref.dtype)
        return job

    def rope_job(out_ref, dst, src, c_ref, s_ref):
        def job():
            blk = proj(src, src + pair)
            for jj in range(2):
                xh = _rms_rope(blk[:, jj * ATTN_DH:(jj + 1) * ATTN_DH], c_ref[...], s_ref[...])
                out_ref[:, dst + jj * ATTN_DH:dst + (jj + 1) * ATTN_DH] = xh.astype(BF16)
        return job

    pair = 2 * ATTN_DH
    vector_jobs = [q_job, k_job]
    vector_jobs += [rope_job(qa_ref, c - SEG_QA[0], c, cq_ref, sq_ref) for c in range(SEG_QA[0], SEG_QA[1], pair)]
    vector_jobs += [rope_job(ka_ref, c - SEG_KA[0], c, ck_ref, sk_ref) for c in range(SEG_KA[0], SEG_KA[1], pair)]
    plain_jobs = []
    for out_ref, seg, piece in ((vom_ref, SEG_VOM, 512), (va_ref, SEG_VA, KV_W),
                                (qc_ref, SEG_QC, 512), (gbr_ref, SEG_GBR, 512)):
        plain_jobs += [plain_job(out_ref, c - seg[0], c, piece) for c in range(seg[0], seg[1], piece)]

    while vector_jobs or plain_jobs:
        if plain_jobs:
            plain_jobs.pop(0)()
        if vector_jobs:
            vector_jobs.pop(0)()

    grow = _dot_nt(wgt_s[...], xb)
    nd = N_GATES // 2
    for j in range(tm // LANES):
        blk = grow[:, j * LANES:(j + 1) * LANES] + gbr_row_ref[...]
        gi_ref[j * nd:(j + 1) * nd, :] = blk[:nd]
        gf_ref[j * nd:(j + 1) * nd, :] = blk[nd:]


def _in_proj(x2, T, w, tm):
    n = x2.shape[0]
    grid = (n // tm,)
    tiles_per_seq = T // tm
    row = lambda i: (i, 0)
    hb = 2 * SUBLANES
    nhb = n // hb
    out_shapes = (
        jax.ShapeDtypeStruct((n, D_MODEL), F32),
        jax.ShapeDtypeStruct((n, MLSTM_W), BF16),
        jax.ShapeDtypeStruct((n // MLSTM_CHUNK * MLSTM_W, MLSTM_CHUNK), BF16),
        jax.ShapeDtypeStruct((n, 2 * MLSTM_W), BF16),
        jax.ShapeDtypeStruct((n, ATTN_W), BF16),
        jax.ShapeDtypeStruct((n, KV_W), BF16),
        jax.ShapeDtypeStruct((n, KV_W), BF16),
        jax.ShapeDtypeStruct((n, MEM_W), BF16),
        jax.ShapeDtypeStruct((n, N_BRANCH * D_MODEL), BF16),
        jax.ShapeDtypeStruct((n // LANES * (N_GATES // 2), LANES), F32),
        jax.ShapeDtypeStruct((n // LANES * (N_GATES // 2), LANES), F32),
    )
    out_specs = (
        pl.BlockSpec((tm, D_MODEL), row),
        pl.BlockSpec((tm, MLSTM_W), row),
        pl.BlockSpec((tm // MLSTM_CHUNK * MLSTM_W, MLSTM_CHUNK), row),
        pl.BlockSpec((tm, 2 * MLSTM_W), row),
        pl.BlockSpec((tm, ATTN_W), row),
        pl.BlockSpec((tm, KV_W), row),
        pl.BlockSpec((tm, KV_W), row),
        pl.BlockSpec((tm, MEM_W), row),
        pl.BlockSpec((tm, N_BRANCH * D_MODEL), row),
        pl.BlockSpec((tm // LANES * (N_GATES // 2), LANES), row),
        pl.BlockSpec((tm // LANES * (N_GATES // 2), LANES), row),
    )
    in_specs = [
        pl.BlockSpec((hb, D_MODEL), lambda i: (jnp.maximum(i * (tm // hb) - 1, 0), 0)),
        pl.BlockSpec((tm, D_MODEL), row),
        pl.BlockSpec((hb, D_MODEL), lambda i: (jnp.minimum((i + 1) * (tm // hb), nhb - 1), 0)),
        _const_spec((1, D_MODEL)), _const_spec((1, D_MODEL)),
        _const_spec((D_MODEL, W_CAT)), _const_spec((D_MODEL, LANES)),
        _const_spec((N_GATES, LANES)),
    ] + [pl.BlockSpec((tm, LANES), lambda i: (i % tiles_per_seq, 0))] * 4 + [
        _const_spec((CONV_W, 2 * MLSTM_W)), _const_spec((1, 2 * MLSTM_W)),
    ]
    return pl.pallas_call(
        functools.partial(_proj_kernel, tiles_per_seq=tiles_per_seq),
        grid=grid, in_specs=in_specs, out_specs=out_specs, out_shape=out_shapes,
        scratch_shapes=[pltpu.VMEM((N_GATES, D_MODEL), BF16)],
        compiler_params=_params(1), name="in_proj",
    )(x2, x2, x2, w["ln_in_g"], w["ln_in_b"], w["w_cat"], w["w_gate"], w["gb_row"],
      w["rope_cos_q"], w["rope_sin_q"], w["rope_cos_k"], w["rope_sin_k"],
      w["mlstm_conv_w"], w["mlstm_conv_b"])


N_DIRHEADS = 2 * MLSTM_HEADS
STAT_GROUPS = 6
STAT_BLOCK = (0, 0, 0, 1, 1, 1)
STAT_BLOCKS = 2


def _mlstm_select():
    lane = jnp.arange(LANES)
    grp, j = lane // N_DIRHEADS, lane % N_DIRHEADS
    blk = jnp.array(STAT_BLOCK + (-1,) * (LANES // N_DIRHEADS - STAT_GROUPS))[grp]
    out_blk = jnp.arange(STAT_BLOCKS * LANES) // LANES
    sel = (j[None, :, None] == jnp.arange(N_DIRHEADS)[:, None, None]) & (blk[None, :, None] == out_blk[None, None, :])
    return sel.astype(BF16)


def _mlstm_kernel(q_ref, kt_ref, vom_ref, gi_ref, gf_ref, sel_ref, ng_ref, hm_ref,
                  ktw_s, lhs_s, e_s, stat_s, gt_s, ma_s, mpf_s, mpb_s, r_s, wa_s, sp_s,
                  hf_s, st_s):
    T = q_ref.shape[0]
    L = MLSTM_CHUNK
    NC = T // L
    DH = MLSTM_DH
    H = MLSTM_HEADS

    ri = lax.broadcasted_iota(jnp.int32, (L, L), 0)
    ci = lax.broadcasted_iota(jnp.int32, (L, L), 1)
    causal = ci <= ri
    anti = ci >= ri
    tril = jnp.where(causal, 1.0, 0.0).astype(BF16)
    triu = jnp.where(anti, 1.0, 0.0).astype(BF16)
    ones_ll = jnp.ones((L, L), BF16)

    R = NC * N_DIRHEADS
    row_r = lax.broadcasted_iota(jnp.int32, (R, L), 0)
    lane_r = lax.broadcasted_iota(jnp.int32, (R, L), 1)
    is_fwd = (row_r % N_DIRHEADS) < H
    is_fwd8 = lax.broadcasted_iota(jnp.int32, (N_DIRHEADS, L), 0) < H

    li = gi_ref[...]
    y1, y2, y3 = _split3(_log_sigmoid(gf_ref[...]))

    def lane_sums(m):
        return _dot(y1, m) + _dot(y2, m) + _dot(y3, m)

    bl = jnp.where(is_fwd, lane_sums(triu), lane_sums(tril))
    gt = lane_sums(ones_ll)
    r = li - bl
    pm = r
    sm = r
    for sh in (1, 2, 4, 8, 16, 32, 64):
        pm = jnp.maximum(pm, jnp.where(lane_r >= sh, pltpu.roll(pm, sh, 1), -jnp.inf))
        sm = jnp.maximum(sm, jnp.where(lane_r < L - sh, pltpu.roll(sm, L - sh, 1), -jnp.inf))
    cm = jnp.where(is_fwd, pm, sm)
    a = gt - bl + li
    ma = jnp.broadcast_to(jnp.max(a, axis=-1, keepdims=True), (R, L))
    gt_s[...] = gt
    ma_s[...] = ma

    def chunk_rows(c):
        return pl.ds(pl.multiple_of(c * N_DIRHEADS, N_DIRHEADS), N_DIRHEADS)

    def stabiliser_scan(i, m):
        rf = chunk_rows(i)
        rb = chunk_rows(NC - 1 - i)
        mpf_s[rf, :] = m
        mpb_s[rb, :] = m
        gt8 = jnp.where(is_fwd8, gt_s[rf, :], gt_s[rb, :])
        ma8 = jnp.where(is_fwd8, ma_s[rf, :], ma_s[rb, :])
        return jnp.maximum(gt8 + m, ma8)

    lax.fori_loop(0, NC, stabiliser_scan, jnp.zeros((N_DIRHEADS, L), F32))

    m_prev = jnp.where(is_fwd, mpf_s[...], mpb_s[...])
    m_new = jnp.maximum(gt + m_prev, ma)
    sp_s[...] = jnp.exp(gt + m_prev - m_new)
    wa_s[...] = jnp.exp(a - m_new)
    r_s[...] = r * LOG2_E
    ma_s[...] = m_prev * LOG2_E
    u = -jnp.maximum(m_prev, cm)
    stat_vals = list(_split3(u * LOG2_E)) + list(_split3(jnp.exp(u - bl)))
    for k, t in enumerate(stat_vals):
        stat_s[k] = t.astype(F32)
    stat_pad = jnp.zeros((L - STAT_GROUPS * N_DIRHEADS, L), F32)

    def prologue(c, carry):
        r0 = pl.multiple_of(c * L, L)
        rows = pl.ds(r0, L)
        rows8 = chunk_rows(c)
        stats = [stat_s[k, rows8, :] for k in range(STAT_GROUPS)] + [stat_pad]
        stats_t = jnp.concatenate(stats, axis=0).T.astype(BF16)
        wa = wa_s[rows8, :]
        r8 = r_s[rows8, :]
        mp8 = ma_s[rows8, :]
        kt_rows = pl.multiple_of(c * MLSTM_W, MLSTM_W)
        for h in range(H):
            hs = slice(h * DH, (h + 1) * DH)
            q = q_ref[rows, hs]
            kt_b = kt_ref[pl.ds(kt_rows + h * DH, DH), :]
            s = _dot(q, kt_b)
            kt = kt_b.astype(F32)
            for d in range(2):
                j = H * d + h
                bc = _dot(stats_t, sel_ref[j])
                u_b = bc[:, :L]
                p = jnp.where(causal if d == 0 else anti, jnp.exp2(u_b + r8[j:j + 1, :]), 0.0)
                w_inter = jnp.exp2(u_b + mp8[j:j + 1, :])
                lhs_s[d, rows, 2 * h * DH:(2 * h + 1) * DH] = (s * p).astype(BF16)
                lhs_s[d, rows, (2 * h + 1) * DH:(2 * h + 2) * DH] = w_inter.astype(BF16) * q
                e_s[j, rows, :] = bc[:, L:]
                ktw_s[d, c, hs, :] = (kt * wa[j:j + 1, :]).astype(BF16)
        return carry

    lax.fori_loop(0, NC, prologue, 0, unroll=4)

    ones_blk = jnp.ones((L, DH), BF16)
    ng = ng_ref[...]

    def run_direction(d):
        st_s[...] = jnp.zeros_like(st_s)

        def body(i, carry):
            c = i if d == 0 else NC - 1 - i
            r0 = pl.multiple_of(c * L, L)
            rows = pl.ds(r0, L)
            for h in range(H):
                j = H * d + h
                hs = slice(h * DH, (h + 1) * DH)
                lhs = lhs_s[d, rows, 2 * h * DH:(2 * h + 2) * DH]
                vaug = jnp.concatenate([vom_ref[rows, hs], ones_blk], axis=1)
                c_prev = st_s[h]
                tot = _dot(lhs, jnp.concatenate([vaug, c_prev.astype(BF16)], axis=0))
                h_out = tot[:, :DH] / jnp.maximum(jnp.abs(tot[:, DH:]), e_s[j, rows, :])
                sp = sp_s[pl.ds(c * N_DIRHEADS + j, 1), :]
                st_s[h] = jnp.concatenate([sp, sp], axis=1) * c_prev + _dot(ktw_s[d, c, hs, :], vaug)

                if d == 0:
                    hf_s[rows, hs] = h_out
                else:
                    hh = hf_s[rows, hs] + h_out
                    mu = jnp.mean(hh, axis=-1, keepdims=True)
                    hc = hh - mu
                    var = jnp.mean(hc * hc, axis=-1, keepdims=True)
                    hn = hc * lax.rsqrt(var + LN_EPS) * ng[:, hs]
                    o = vom_ref[rows, MLSTM_W + h * DH:MLSTM_W + (h + 1) * DH].astype(F32)
                    hm_ref[rows, hs] = (hn * _sigmoid(o)).astype(BF16)
            return carry

        lax.fori_loop(0, NC, body, 0, unroll=4)

    run_direction(0)
    run_direction(1)


def _mlstm(qm, ktm, vom, gi, gf, T, w):
    n = qm.shape[0]
    B = n // T
    NC = T // MLSTM_CHUNK
    seq = lambda b: (b, 0)
    gate_major = pltpu.VMEM((NC * N_DIRHEADS, MLSTM_CHUNK), F32)
    return pl.pallas_call(
        _mlstm_kernel, grid=(B,),
        in_specs=[
            pl.BlockSpec((T, MLSTM_W), seq),
            pl.BlockSpec((NC * MLSTM_W, MLSTM_CHUNK), seq),
            pl.BlockSpec((T, 2 * MLSTM_W), seq),
            pl.BlockSpec((NC * N_DIRHEADS, MLSTM_CHUNK), seq),
            pl.BlockSpec((NC * N_DIRHEADS, MLSTM_CHUNK), seq),
            _const_spec((N_DIRHEADS, LANES, STAT_BLOCKS * MLSTM_CHUNK)),
            _const_spec((1, MLSTM_W)),
        ],
        out_specs=pl.BlockSpec((T, MLSTM_W), seq),
        out_shape=jax.ShapeDtypeStruct((n, MLSTM_W), BF16),
        scratch_shapes=[
            pltpu.VMEM((2, NC, MLSTM_W, MLSTM_CHUNK), BF16),
            pltpu.VMEM((2, T, 2 * MLSTM_W), BF16),
            pltpu.VMEM((N_DIRHEADS, T, LANES), F32),
            pltpu.VMEM((STAT_GROUPS, NC * N_DIRHEADS, MLSTM_CHUNK), F32),
            gate_major, gate_major, gate_major, gate_major,
            gate_major, gate_major, gate_major,
            pltpu.VMEM((T, MLSTM_W), F32),
            pltpu.VMEM((MLSTM_HEADS, MLSTM_DH, 2 * MLSTM_DH), F32),
        ],
        compiler_params=_params(1), name="mlstm",
    )(qm, ktm, vom, gi, gf, w["mlstm_sel"], w["mlstm_norm_g"])


def _attn_kernel(q_ref, k_ref, v_ref, o_ref, vaug_s):
    @pl.when(pl.program_id(2) == 0)
    def _():
        vaug_s[:, :ATTN_DH] = v_ref[...]
        vaug_s[:, ATTN_DH:] = jnp.ones((v_ref.shape[0], ATTN_DH), BF16)

    k = k_ref[...]
    tq = q_ref.shape[0]
    rows_per_unit = min(tq, ATTN_UNIT_ROWS)
    for g in range(ATTN_GROUP):
        hs = slice(g * ATTN_DH, (g + 1) * ATTN_DH)
        for r0 in range(0, tq, rows_per_unit):
            rs = slice(r0, r0 + rows_per_unit)
            s = _dot_nt(q_ref[rs, hs], k)
            m = jnp.max(s, axis=-1, keepdims=True)
            p = jnp.exp2(s - m)
            oa = _dot(p.astype(BF16), vaug_s[...])
            o_ref[rs, hs] = (oa[:, :ATTN_DH] / oa[:, ATTN_DH:ATTN_DH + 1]).astype(o_ref.dtype)


def _attention(qa, ka, va, T, tq):
    n = qa.shape[0]
    B = n // T
    nq = T // tq
    gw = ATTN_GROUP * ATTN_DH
    return pl.pallas_call(
        _attn_kernel, grid=(B, KV_HEADS, nq),
        in_specs=[
            pl.BlockSpec((tq, gw), lambda b, h, i: (b * nq + i, h)),
            pl.BlockSpec((T, ATTN_DH), lambda b, h, i: (b, h)),
            pl.BlockSpec((T, ATTN_DH), lambda b, h, i: (b, h)),
        ],
        out_specs=pl.BlockSpec((tq, gw), lambda b, h, i: (b * nq + i, h)),
        out_shape=jax.ShapeDtypeStruct((n, ATTN_W), BF16),
        scratch_shapes=[pltpu.VMEM((T, 2 * ATTN_DH), BF16)],
        compiler_params=_params(3), name="gqa_attn",
    )(qa, ka, va)


def _mem_kv_kernel(m_ref, w_ref, o_ref):
    o_ref[...] = _dot(m_ref[...].astype(BF16), w_ref[...]).astype(o_ref.dtype)


def _mem_kv(mem2, w):
    n = mem2.shape[0]
    tm = _tile(n, 1024)
    return pl.pallas_call(
        _mem_kv_kernel, grid=(n // tm,),
        in_specs=[pl.BlockSpec((tm, D_MODEL), lambda b: (b, 0)),
                  _const_spec((D_MODEL, 2 * MEM_W))],
        out_specs=pl.BlockSpec((tm, 2 * MEM_W), lambda b: (b, 0)),
        out_shape=jax.ShapeDtypeStruct((n, 2 * MEM_W), BF16),
        compiler_params=_params(1), name="mem_kv",
    )(mem2, w["w_mem_kv"])


def _merge_kernel(xn_ref, hm_ref, ha_ref, qc_ref, kv_ref, gbr_ref,
                  wbm_ref, wba_ref, wbc_ref, wo_ref, l1g_ref, l1b_ref, x1_ref):
    ones_blk = jnp.ones((kv_ref.shape[0], MEM_DH), BF16)
    hc_parts = []
    for h in range(MEM_HEADS):
        hs = slice(h * MEM_DH, (h + 1) * MEM_DH)
        kc = kv_ref[:, hs]
        vaug = jnp.concatenate([kv_ref[:, MEM_W + h * MEM_DH:MEM_W + (h + 1) * MEM_DH], ones_blk], axis=1)
        s = _dot_nt(qc_ref[:, hs], kc)
        p = jnp.exp2(s - jnp.max(s, axis=-1, keepdims=True))
        oa = _dot(p.astype(BF16), vaug)
        hc_parts.append((oa[:, :MEM_DH] / oa[:, MEM_DH:MEM_DH + 1]).astype(BF16))
    hc = jnp.concatenate(hc_parts, axis=1)

    def gate(j):
        return _sigmoid(gbr_ref[:, j * D_MODEL:(j + 1) * D_MODEL].astype(F32))

    merged = gate(0) * _dot(hm_ref[...], wbm_ref[...])
    merged = merged + gate(1) * _dot(ha_ref[...], wba_ref[...])
    merged = merged + gate(2) * _dot(hc, wbc_ref[...])
    mb = merged.astype(BF16)
    rb = min(mb.shape[0], TAIL_ROWS)
    for r0 in range(0, mb.shape[0], rb):
        mixed = _dot(mb[r0:r0 + rb], wo_ref[...])
        x1_ref[r0:r0 + rb, :] = _layer_norm(DEEPNORM_ALPHA * xn_ref[r0:r0 + rb, :] + mixed,
                                            l1g_ref[...], l1b_ref[...])


def _merge(xn, hm, ha, qc, kvc, gbr, T, M, w, tm):
    n = xn.shape[0]
    tiles_per_seq = T // tm
    row = lambda i: (i, 0)
    return pl.pallas_call(
        _merge_kernel, grid=(n // tm,),
        in_specs=[
            pl.BlockSpec((tm, D_MODEL), row),
            pl.BlockSpec((tm, MLSTM_W), row),
            pl.BlockSpec((tm, ATTN_W), row),
            pl.BlockSpec((tm, MEM_W), row),
            pl.BlockSpec((M, 2 * MEM_W), lambda i: (i // tiles_per_seq, 0)),
            pl.BlockSpec((tm, N_BRANCH * D_MODEL), row),
            _const_spec((MLSTM_W, D_MODEL)), _const_spec((ATTN_W, D_MODEL)),
            _const_spec((MEM_W, D_MODEL)), _const_spec((D_MODEL, D_MODEL)),
            _const_spec((1, D_MODEL)), _const_spec((1, D_MODEL)),
        ],
        out_specs=pl.BlockSpec((tm, D_MODEL), row),
        out_shape=jax.ShapeDtypeStruct((n, D_MODEL), F32),
        compiler_params=_params(1), name="merge_out",
    )(xn, hm, ha, qc, kvc, gbr, w["w_branch_mlstm"],
      w["w_branch_attn"], w["w_branch_mem"], w["w_out"], w["ln1_g"], w["ln1_b"])


GELU_C = 0.7978845608028654
GELU_C3 = GELU_C * 0.044715


def _gelu_tanh_x2(x):
    return x * (1.0 + jnp.tanh(x * (GELU_C + GELU_C3 * (x * x))))


def _ffn_kernel(xp_ref, x_ref, xn_ref, wup_ref, cw_ref, cb_ref, wdn_ref, l2g_ref, l2b_ref,
                y_ref, xe_s, act_s, *, tiles_per_seq):
    tm = x_ref.shape[0]
    i = pl.program_id(0)
    j = i % tiles_per_seq
    prev = xp_ref[...] * jnp.where(j > 0, 1.0, 0.0).astype(F32)
    nxt = xn_ref[...] * jnp.where(j < tiles_per_seq - 1, 1.0, 0.0).astype(F32)
    xe_s[...] = jnp.concatenate([prev, x_ref[...], nxt], axis=0).astype(BF16)
    te = tm + 2 * SUBLANES

    def conv(u, cw, cb):
        up = pltpu.roll(u, 1, 0)[SUBLANES:SUBLANES + tm, :]
        uc = u[SUBLANES:SUBLANES + tm, :]
        un = pltpu.roll(u, te - 1, 0)[SUBLANES:SUBLANES + tm, :]
        y = up * cw[0:1, :] + cb
        y = y + uc * cw[1:2, :]
        return y + un * cw[2:3, :]

    def up_conv(lo):
        cols = slice(lo, lo + FFN_CHUNK)
        return conv(_dot(xe_s[...], wup_ref[:, cols]), cw_ref[:, cols], cb_ref[:, cols])

    for c in range(0, D_FF, FFN_CHUNK):
        act_s[:, c:c + FFN_CHUNK] = (_gelu_tanh_x2(up_conv(c)) * up_conv(D_FF + c)).astype(BF16)

    rb = min(tm, 2 * TAIL_ROWS)
    for r0 in range(0, tm, rb):
        ff = _dot(act_s[r0:r0 + rb, :], wdn_ref[...])
        y_ref[r0:r0 + rb, :] = _layer_norm(DEEPNORM_ALPHA * x_ref[r0:r0 + rb, :] + ff,
                                           l2g_ref[...], l2b_ref[...])


def _ffn(x1, T, w, tm):
    n = x1.shape[0]
    tiles_per_seq = T // tm
    r8 = tm // SUBLANES
    nblk8 = n // SUBLANES
    row = lambda i: (i, 0)
    return pl.pallas_call(
        functools.partial(_ffn_kernel, tiles_per_seq=tiles_per_seq), grid=(n // tm,),
        in_specs=[
            pl.BlockSpec((SUBLANES, D_MODEL), lambda i: (jnp.maximum(i * r8 - 1, 0), 0)),
            pl.BlockSpec((tm, D_MODEL), row),
            pl.BlockSpec((SUBLANES, D_MODEL), lambda i: (jnp.minimum((i + 1) * r8, nblk8 - 1), 0)),
            _const_spec((D_MODEL, 2 * D_FF)),
            _const_spec((CONV_W, 2 * D_FF)),
            _const_spec((1, 2 * D_FF)),
            _const_spec((D_FF, D_MODEL)),
            _const_spec((1, D_MODEL)), _const_spec((1, D_MODEL)),
        ],
        out_specs=pl.BlockSpec((tm, D_MODEL), row),
        out_shape=jax.ShapeDtypeStruct((n, D_MODEL), F32),
        scratch_shapes=[pltpu.VMEM((tm + 2 * SUBLANES, D_MODEL), BF16),
                        pltpu.VMEM((tm, D_FF), BF16)],
        compiler_params=_params(1), name="conv_ffn",
    )(x1, x1, x1, w["w_ffn_up"], w["ffn_conv_w"], w["ffn_conv_b"], w["w_ffn_down"],
      w["ln2_g"], w["ln2_b"])


def _rope_tables(T):
    rows = T // GRID_W
    row = jnp.repeat(jnp.arange(rows, dtype=F32), GRID_W)
    col = jnp.tile(jnp.arange(GRID_W, dtype=F32), rows)
    inv_freq = ROPE_THETA ** (-jnp.arange(0, ROPE_AXIS_DIM, 2, dtype=F32) / ROPE_AXIS_DIM)
    ang_r = row[:, None] * inv_freq
    ang_c = col[:, None] * inv_freq
    cos = jnp.concatenate([jnp.cos(ang_r), jnp.cos(ang_c)] * 2, axis=-1)
    sin = jnp.concatenate([-jnp.sin(ang_r), -jnp.sin(ang_c), jnp.sin(ang_r), jnp.sin(ang_c)], axis=-1)
    return cos, sin


def _rope_head_perm():
    p = jnp.arange(ATTN_DH)
    quarter = ROPE_AXIS_DIM // 2
    half, axis, j = p // ROPE_AXIS_DIM, (p % ROPE_AXIS_DIM) // quarter, p % quarter
    return axis * ROPE_AXIS_DIM + half * quarter + j


def _gained_tables(cos, sin, g, scale):
    g = g.astype(F32)[_rope_head_perm()]
    return cos * (g * scale), sin * (jnp.roll(g, ATTN_DH // 2) * scale)


def _prep_weights(l, T, ln_in_g, ln_in_b, w_in, mlstm_gate_bias, mlstm_conv_w, mlstm_conv_b,
                  mlstm_norm_g, attn_q_norm_g, attn_k_norm_g, w_mem_kv, w_branch_mlstm,
                  w_branch_attn, w_branch_mem, w_out, ln1_g, ln1_b, w_ffn_up, ffn_conv_w,
                  ffn_conv_b, w_ffn_down, ln2_g, ln2_b):
    wi = w_in[l]
    g0 = 4 * MLSTM_W
    g1 = g0 + N_GATES
    w_gate = wi[:, g0:g1]
    qk0 = g1
    qk1 = qk0 + ATTN_W + KV_W
    qc0 = qk1 + KV_W
    qc1 = qc0 + MEM_W
    quarter = ROPE_AXIS_DIM // 2
    w_qk = wi[:, qk0:qk1].reshape(D_MODEL, ATTN_HEADS + KV_HEADS, 2, 2, quarter)
    w_qk = jnp.swapaxes(w_qk, 2, 3).reshape(D_MODEL, ATTN_W + KV_W)
    w_qc = wi[:, qc0:qc1] * (MEM_DH ** -0.5 * LOG2_E)
    w_cat = jnp.concatenate([wi[:, :g0], w_qk, wi[:, qk1:qc0], w_qc, wi[:, qc1:]], axis=1).astype(BF16)
    gate_rows = jnp.array(GATE_ROW_ORDER)
    w_gate = w_gate[:, gate_rows]
    gb = mlstm_gate_bias[l].astype(F32)[gate_rows]
    cos, sin = _rope_tables(T)
    cos_q, sin_q = _gained_tables(cos, sin, attn_q_norm_g[l], ATTN_DH ** -0.5 * LOG2_E)
    cos_k, sin_k = _gained_tables(cos, sin, attn_k_norm_g[l], 1.0)
    r2 = lambda a: a.reshape(1, -1).astype(F32)
    ffn_half = jnp.where(jnp.arange(2 * D_FF) >= D_FF, 0.5, 1.0).astype(F32)[None, :]
    return {
        "ln_in_g": r2(ln_in_g), "ln_in_b": r2(ln_in_b),
        "w_cat": w_cat, "w_gate": jnp.pad(w_gate, ((0, 0), (0, LANES - N_GATES))).astype(BF16),
        "gb_row": jnp.broadcast_to(gb[:, None], (N_GATES, LANES)),
        "mlstm_sel": _mlstm_select(),
        "rope_cos_q": cos_q, "rope_sin_q": sin_q, "rope_cos_k": cos_k, "rope_sin_k": sin_k,
        "mlstm_conv_w": mlstm_conv_w[l].astype(F32), "mlstm_conv_b": r2(mlstm_conv_b[l]),
        "mlstm_norm_g": r2(mlstm_norm_g[l]),
        "w_mem_kv": w_mem_kv[l].astype(BF16),
        "w_branch_mlstm": w_branch_mlstm[l].astype(BF16),
        "w_branch_attn": w_branch_attn[l].astype(BF16),
        "w_branch_mem": w_branch_mem[l].astype(BF16),
        "w_out": w_out[l].astype(BF16),
        "ln1_g": r2(ln1_g[l]), "ln1_b": r2(ln1_b[l]),
        "w_ffn_up": w_ffn_up[l].astype(BF16),
        "ffn_conv_w": ffn_conv_w[l].astype(F32) * ffn_half,
        "ffn_conv_b": r2(ffn_conv_b[l]) * ffn_half,
        "w_ffn_down": w_ffn_down[l].astype(BF16),
        "ln2_g": r2(ln2_g[l]), "ln2_b": r2(ln2_b[l]),
    }


def _tile(T, want):
    t = min(want, T)
    assert T % t == 0
    return t


def _trunk(x, mem, w):
    B, T, _ = x.shape
    M = mem.shape[1]
    assert T % MLSTM_CHUNK == 0 and T % GRID_W == 0
    x2 = x.reshape(B * T, D_MODEL)
    mem2 = mem.reshape(B * M, D_MODEL)
    xn, qm, ktm, vom, qa, ka, va, qc, gbr, gi, gf = _in_proj(x2, T, w, _tile(T, 512))
    hm = _mlstm(qm, ktm, vom, gi, gf, T, w)
    ha = _attention(qa, ka, va, T, _tile(T, 2048))
    kvc = _mem_kv(mem2, w)
    x1 = _merge(xn, hm, ha, qc, kvc, gbr, T, M, w, _tile(T, 512))
    y = _ffn(x1, T, w, _tile(T, 512))
    return y.reshape(B, T, D_MODEL)


def kernel(x_prompt, x_sample, mem_prompt, mem_sample, ln_in_g, ln_in_b, w_in, mlstm_gate_bias, mlstm_conv_w, mlstm_conv_b, mlstm_norm_g, attn_q_norm_g, attn_k_norm_g, w_mem_kv, w_branch_mlstm, w_branch_attn, w_branch_mem, w_out, ln1_g, ln1_b, w_ffn_up, ffn_conv_w, ffn_conv_b, w_ffn_down, ln2_g, ln2_b):
    assert DEPTH == 1 and x_prompt.shape[1] == x_sample.shape[1]
    w = _prep_weights(0, x_prompt.shape[1], ln_in_g, ln_in_b, w_in, mlstm_gate_bias, mlstm_conv_w,
                      mlstm_conv_b, mlstm_norm_g, attn_q_norm_g, attn_k_norm_g, w_mem_kv,
                      w_branch_mlstm, w_branch_attn, w_branch_mem, w_out, ln1_g, ln1_b, w_ffn_up,
                      ffn_conv_w, ffn_conv_b, w_ffn_down, ln2_g, ln2_b)
    return (_trunk(x_prompt, mem_prompt, w), _trunk(x_sample, mem_sample, w))
```
